```python
import math
import jax
import jax.numpy as jnp
from jax import lax
import numpy as np

D_MODEL = 2048
BATCH = 4
SEQ = 2048
DEPTH = 4

D_MIX = D_MODEL
POOL_WINDOWS = (2, 4, 8, 16)
POOL_GROUPS = 4
POOL_WIDTH = D_MIX // 4
POOL_GROUP_DIM = POOL_WIDTH // POOL_GROUPS
MLA_NOPE = 128
MLA_ROPE = 64
MLA_V = 128
MLA_HEADS = (3 * D_MIX // 8) // MLA_V
MLA_Q_RANK = 512
MLA_KV_RANK = 256
ROPE_THETA = 10000.0
DIFF_QK = 64
DIFF_V = 2 * DIFF_QK
DIFF_HEADS = (D_MIX - POOL_WIDTH - MLA_HEADS * MLA_V) // DIFF_V
REL_BUCKETS = 32
REL_MAX_EXACT = 16
REL_MAX_DISTANCE = 128
Q_BLOCK = 128
IN_SPLITS = (POOL_WIDTH, MLA_Q_RANK, MLA_KV_RANK, MLA_ROPE,
             DIFF_HEADS * 2 * DIFF_QK, DIFF_HEADS * 2 * DIFF_QK, DIFF_HEADS * DIFF_V)
N_IN = 3648
N_GROUPS = 4
EXPERTS_PER_GROUP = 8
N_EXPERTS = N_GROUPS * EXPERTS_PER_GROUP
TOP_K = 2
EXPERT_HIDDEN = 512
MOE_BLOCK = 128
NORM_EPS = 1e-6

kernel_name = 'hymba_pool_mla_diff_hiermoe'


def rmsnorm(x, g):
    xf = x.astype(jnp.float32)
    y = xf * lax.rsqrt(jnp.mean(xf * xf, axis=-1, keepdims=True) + NORM_EPS)
    return (y * g.astype(jnp.float32)).astype(x.dtype)


def rope_tables(positions, dtype):
    inv = 1.0 / (ROPE_THETA ** (jnp.arange(0, MLA_ROPE, 2, dtype=jnp.float32) / MLA_ROPE))
    ang = positions.astype(jnp.float32)[..., None] * inv
    return jnp.cos(ang).astype(dtype), jnp.sin(ang).astype(dtype)


def apply_rope(x, cos, sin):
    half = x.shape[-1] // 2
    x1, x2 = x[..., :half], x[..., half:]
    return jnp.concatenate([x1 * cos - x2 * sin, x1 * sin + x2 * cos], axis=-1)


def causal_mask(q0, nq, nk):
    return jnp.arange(nk)[None, :] <= (q0 + jnp.arange(nq))[:, None]


def t5_bucket(rel):
    n = jnp.maximum(rel, 0)
    nf = jnp.maximum(n, 1).astype(jnp.float32)
    large = REL_MAX_EXACT + (jnp.log(nf / REL_MAX_EXACT)
                             / math.log(REL_MAX_DISTANCE / REL_MAX_EXACT)
                             * (REL_BUCKETS - REL_MAX_EXACT)).astype(jnp.int32)
    large = jnp.minimum(large, REL_BUCKETS - 1)
    return jnp.where(n < REL_MAX_EXACT, n, large)


def pool_mixer(u, w, scale):
    B, S, _ = u.shape
    ug = u.reshape(B, S, POOL_GROUPS, POOL_GROUP_DIM)
    cs = lax.cumsum(ug.astype(jnp.float32), axis=1)
    t = jnp.arange(S)
    means = []
    for g, win in enumerate(POOL_WINDOWS):
        c = cs[:, :, g]
        lag = jnp.concatenate([jnp.zeros((B, win, POOL_GROUP_DIM), jnp.float32), c[:, :S - win]], axis=1)
        cnt = jnp.minimum(t + 1, win).astype(jnp.float32)[None, :, None]
        means.append((c - lag) / cnt)
    d = jnp.stack(means, axis=2).astype(u.dtype) - ug
    y = jnp.einsum('bsgc,gcd->bsgd', d, w).reshape(B, S, POOL_WIDTH)
    return y * scale


def mla_mixer(c_q, c_kv, k_r, q_norm, w_uq, kv_norm, w_ukv, cos, sin):
    B, S, _ = c_q.shape
    q = (rmsnorm(c_q, q_norm) @ w_uq).reshape(B, S, MLA_HEADS, MLA_NOPE + MLA_ROPE)
    q_nope = q[..., :MLA_NOPE]
    q_rope = apply_rope(q[..., MLA_NOPE:], cos[:, :, None], sin[:, :, None])
    kv = (rmsnorm(c_kv, kv_norm) @ w_ukv).reshape(B, S, MLA_HEADS, MLA_NOPE + MLA_V)
    k_nope, v = kv[..., :MLA_NOPE], kv[..., MLA_NOPE:]
    k_rope = apply_rope(k_r, cos, sin)
    scale = (MLA_NOPE + MLA_ROPE) ** -0.5
    outs = []
    for i in range(S // Q_BLOCK):
        q0, end = i * Q_BLOCK, (i + 1) * Q_BLOCK
        s = (jnp.einsum('bqhd,bkhd->bhqk', q_nope[:, q0:end], k_nope[:, :end])
             + jnp.einsum('bqhd,bkd->bhqk', q_rope[:, q0:end], k_rope[:, :end])).astype(jnp.float32) * scale
        s = jnp.where(causal_mask(q0, Q_BLOCK, end), s, -jnp.inf)
        p = jax.nn.softmax(s, axis=-1).astype(v.dtype)
        outs.append(jnp.einsum('bhqk,bkhd->bqhd', p, v[:, :end]))
    return jnp.concatenate(outs, axis=1).reshape(B, S, MLA_HEADS * MLA_V)


def diff_mixer(q, k, v, lam_vecs, subln, rel_bias, positions, lam_init):
    B, S, _ = q.shape
    q = q.reshape(B, S, DIFF_HEADS, 2, DIFF_QK)
    k = k.reshape(B, S, DIFF_HEADS, 2, DIFF_QK)
    v = v.reshape(B, S, DIFF_HEADS, DIFF_V)
    lv = lam_vecs.astype(jnp.float32)
    lam = jnp.exp(jnp.sum(lv[0] * lv[1])) - jnp.exp(jnp.sum(lv[2] * lv[3])) + lam_init
    scale = DIFF_QK ** -0.5
    outs = []
    for i in range(S // Q_BLOCK):
        q0, end = i * Q_BLOCK, (i + 1) * Q_BLOCK
        rel = positions[:, q0:end, None] - positions[:, None, :end]
        bias = jnp.transpose(rel_bias[t5_bucket(rel)], (0, 3, 1, 2)).astype(jnp.float32)
        s = jnp.einsum('bqhcd,bkhcd->cbhqk', q[:, q0:end], k[:, :end]).astype(jnp.float32) * scale + bias[None]
        s = jnp.where(causal_mask(q0, Q_BLOCK, end), s, -jnp.inf)
        p = jax.nn.softmax(s, axis=-1)
        a = (p[0] - lam * p[1]).astype(v.dtype)
        outs.append(jnp.einsum('bhqk,bkhd->bqhd', a, v[:, :end]))
    o = jnp.concatenate(outs, axis=1)
    o = rmsnorm(o, subln) * (1.0 - lam_init)
    return o.reshape(B, S, DIFF_HEADS * DIFF_V)


def hier_moe(x, wg_r, bg_r, we_r, be_r, w_gate, w_up, w_down):
    B, S, D = x.shape
    T = B * S
    xf = x.reshape(T, D)
    g_logits = (xf @ wg_r + bg_r).astype(jnp.float32)
    g_prob = jax.nn.softmax(g_logits, axis=-1)
    g_idx = jnp.argmax(g_logits, axis=-1)
    g_gate = jnp.take_along_axis(g_prob, g_idx[:, None], axis=1)[:, 0]
    e_logits = (xf @ we_r + be_r).astype(jnp.float32).reshape(T, N_GROUPS, EXPERTS_PER_GROUP)
    e_in = jnp.take_along_axis(e_logits, g_idx[:, None, None], axis=1)[:, 0]
    top_p, top_i = lax.top_k(jax.nn.softmax(e_in, axis=-1), TOP_K)
    top_p = top_p / jnp.sum(top_p, axis=-1, keepdims=True)
    expert_id = (g_idx[:, None] * EXPERTS_PER_GROUP + top_i).reshape(-1)
    gate = (g_gate[:, None] * top_p).reshape(-1)
    tok = jnp.repeat(jnp.arange(T, dtype=jnp.int32), TOP_K)
    order = jnp.argsort(expert_id)
    se = expert_id[order]
    counts = jnp.zeros((N_EXPERTS,), jnp.int32).at[expert_id].add(1)
    padded = (counts + MOE_BLOCK - 1) // MOE_BLOCK * MOE_BLOCK
    start = jnp.cumsum(counts) - counts
    ends = jnp.cumsum(padded)
    pstart = ends - padded
    dest = pstart[se] + (jnp.arange(T * TOP_K) - start[se])
    n_blocks = -(-(T * TOP_K) // MOE_BLOCK) + N_EXPERTS
    P = n_blocks * MOE_BLOCK
    buf_tok = jnp.full((P,), T, jnp.int32).at[dest].set(tok[order])
    buf_gate = jnp.zeros((P,), x.dtype).at[dest].set(gate[order].astype(x.dtype))
    block_e = jnp.minimum(jnp.searchsorted(ends, jnp.arange(n_blocks) * MOE_BLOCK, side='right'), N_EXPERTS - 1)
    x_pad = jnp.concatenate([xf, jnp.zeros((1, D), x.dtype)], axis=0)
    xb = x_pad[buf_tok].reshape(n_blocks, MOE_BLOCK, D)

    def expert_block(args):
        xblk, e = args
        h = jax.nn.silu(xblk @ w_gate[e]) * (xblk @ w_up[e])
        return h @ w_down[e]

    yb = lax.map(expert_block, (xb, block_e)).reshape(P, D) * buf_gate[:, None]
    out = jnp.zeros((T + 1, D), x.dtype).at[buf_tok].add(yb)[:T]
    return out.reshape(B, S, D)


def setup_inputs(seed: int = 0) -> dict:
    key = jax.random.key(seed)
    ks = list(jax.random.split(key, 24))
    f32 = jnp.float32

    def nrm(k, shape, scale):
        return jax.random.normal(k, shape, f32) * scale

    def gain(k, shape):
        return 1.0 + 0.02 * jax.random.normal(k, shape, f32)

    L = DEPTH
    positions = (jax.random.randint(ks[1], (BATCH, 1), 0, 1024, dtype=jnp.int32)
                 + jnp.arange(SEQ, dtype=jnp.int32)[None, :])
    return {
        'x': nrm(ks[0], (BATCH, SEQ, D_MODEL), 1.0),
        'positions': positions,
        'attn_norm': gain(ks[2], (L, D_MODEL)),
        'w_in': nrm(ks[3], (L, D_MODEL, N_IN), D_MODEL ** -0.5),
        'pool_w': nrm(ks[4], (L, POOL_GROUPS, POOL_GROUP_DIM, POOL_GROUP_DIM), POOL_GROUP_DIM ** -0.5),
        'pool_scale': gain(ks[5], (L, POOL_WIDTH)),
        'mla_q_norm': gain(ks[6], (L, MLA_Q_RANK)),
        'mla_w_uq': nrm(ks[7], (L, MLA_Q_RANK, MLA_HEADS * (MLA_NOPE + MLA_ROPE)), MLA_Q_RANK ** -0.5),
        'mla_kv_norm': gain(ks[8], (L, MLA_KV_RANK)),
        'mla_w_ukv': nrm(ks[9], (L, MLA_KV_RANK, MLA_HEADS * (MLA_NOPE + MLA_V)), MLA_KV_RANK ** -0.5),
        'diff_lambda': nrm(ks[10], (L, 4, DIFF_QK), 0.1),
        'diff_subln': gain(ks[11], (L, DIFF_V)),
        'rel_bias': nrm(ks[12], (REL_BUCKETS, DIFF_HEADS), 0.3),
        'w_out': nrm(ks[13], (L, D_MIX, D_MODEL), D_MIX ** -0.5),
        'ffn_norm': gain(ks[14], (L, D_MODEL)),
        'router_group_w': nrm(ks[15], (L, D_MODEL, N_GROUPS), D_MODEL ** -0.5),
        'router_group_b': nrm(ks[16], (L, N_GROUPS), 0.01),
        'router_expert_w': nrm(ks[17], (L, D_MODEL, N_EXPERTS), D_MODEL ** -0.5),
        'router_expert_b': nrm(ks[18], (L, N_EXPERTS), 0.01),
        'expert_w_gate': nrm(ks[19], (L, N_EXPERTS, D_MODEL, EXPERT_HIDDEN), D_MODEL ** -0.5),
        'expert_w_up': nrm(ks[20], (L, N_EXPERTS, D_MODEL, EXPERT_HIDDEN), D_MODEL ** -0.5),
        'expert_w_down': nrm(ks[21], (L, N_EXPERTS, EXPERT_HIDDEN, D_MODEL), EXPERT_HIDDEN ** -0.5),
        'final_norm': gain(ks[22], (D_MODEL,)),
    }


def reference(x, positions, attn_norm, w_in, pool_w, pool_scale, mla_q_norm, mla_w_uq,
              mla_kv_norm, mla_w_ukv, diff_lambda, diff_subln, rel_bias, w_out, ffn_norm,
              router_group_w, router_group_b, router_expert_w, router_expert_b,
              expert_w_gate, expert_w_up, expert_w_down, final_norm):
    cos, sin = rope_tables(positions, x.dtype)
    offs = [0]
    for w in IN_SPLITS:
        offs.append(offs[-1] + w)
    h = x
    for l in range(DEPTH):
        u = rmsnorm(h, attn_norm[l])
        proj = u @ w_in[l]
        p_pool, c_q, c_kv, k_r, dq, dk, dv = [proj[..., offs[j]:offs[j + 1]] for j in range(len(IN_SPLITS))]
        lam_init = 0.8 - 0.6 * math.exp(-0.3 * l)
        a_out = pool_mixer(p_pool, pool_w[l], pool_scale[l])
        b_out = mla_mixer(c_q, c_kv, k_r, mla_q_norm[l], mla_w_uq[l], mla_kv_norm[l], mla_w_ukv[l], cos, sin)
        c_out = diff_mixer(dq, dk, dv, diff_lambda[l], diff_subln[l], rel_bias, positions, lam_init)
        mix = jnp.concatenate([a_out, b_out, c_out], axis=-1)
        h = h + mix @ w_out[l]
        h = h + hier_moe(rmsnorm(h, ffn_norm[l]), router_group_w[l], router_group_b[l],
                         router_expert_w[l], router_expert_b[l],
                         expert_w_gate[l], expert_w_up[l], expert_w_down[l])
    return rmsnorm(h, final_norm)
```

```python
import functools
import math

import jax
import jax.numpy as jnp
from jax import lax
from jax.experimental import pallas as pl
from jax.experimental.pallas import tpu as pltpu

f32, bf16, i32 = jnp.float32, jnp.bfloat16, jnp.int32

D_MODEL = 2048
POOL_WINDOWS = (2, 4, 8, 16)
POOL_GROUPS = 4
POOL_WIDTH = 512
MLA_NOPE, MLA_ROPE, MLA_V, MLA_HEADS = 128, 64, 128, 6
MLA_Q_RANK, MLA_KV_RANK = 512, 256
ROPE_THETA = 10000.0
DIFF_QK, DIFF_V, DIFF_HEADS = 64, 128, 6
REL_BUCKETS, REL_MAX_EXACT, REL_MAX_DISTANCE = 32, 16, 128
N_GROUPS, EXPERTS_PER_GROUP, N_EXPERTS, TOP_K, EXPERT_HIDDEN = 4, 8, 32, 2, 512
NORM_EPS = 1e-6

LANE = 128
MXU_DIM = 256
VMEM_LIMIT = 56 * 1024 * 1024

N_IN_PAD = 3712
COL_CQ, COL_CKV, COL_KR, COL_DQ, COL_DK, COL_DV = 512, 1024, 1280, 1408, 2176, 2944
QK_PAD = 2 * LANE

TM = 256
TQ = 512
TK = 256
MOE_BLK = 256
REL_CLIP = LANE - 1


def _cparams(n_axes):
    return pltpu.CompilerParams(dimension_semantics=("arbitrary",) * n_axes, vmem_limit_bytes=VMEM_LIMIT)


def _rms(x, g):
    return x * lax.rsqrt(jnp.mean(x * x, axis=-1, keepdims=True) + NORM_EPS) * g


def _inproj_kernel(*refs, combine):
    if combine:
        h_ref, y_ref, g_ref, w_ref, proj_ref, hout_ref = refs
        h = h_ref[...] + y_ref[:, :D_MODEL] + y_ref[:, D_MODEL:]
        hout_ref[...] = h
    else:
        h_ref, g_ref, w_ref, proj_ref = refs
        h = h_ref[...]
    u = _rms(h, g_ref[...]).astype(bf16)
    for n0 in range(0, N_IN_PAD, 512):
        n1 = min(n0 + 512, N_IN_PAD)
        proj_ref[:, n0:n1] = jnp.dot(u, w_ref[:, n0:n1], preferred_element_type=f32).astype(bf16)


def _inproj(h, y2, g, w):
    T = h.shape[0]
    combine = y2 is not None
    row = lambda i: (i, 0)
    const = lambda i: (0, 0)
    in_specs = [pl.BlockSpec((TM, D_MODEL), row)]
    args = [h]
    if combine:
        in_specs.append(pl.BlockSpec((TM, 2 * D_MODEL), row))
        args.append(y2)
    in_specs += [pl.BlockSpec((1, D_MODEL), const),
                 pl.BlockSpec((D_MODEL, N_IN_PAD), const, pipeline_mode=pl.Buffered(1))]
    args += [g, w]
    out_shape = [jax.ShapeDtypeStruct((T, N_IN_PAD), bf16)]
    out_specs = [pl.BlockSpec((TM, N_IN_PAD), row)]
    if combine:
        out_shape.append(jax.ShapeDtypeStruct((T, D_MODEL), f32))
        out_specs.append(pl.BlockSpec((TM, D_MODEL), row))
    res = pl.pallas_call(
        functools.partial(_inproj_kernel, combine=combine),
        grid=(T // TM,), in_specs=in_specs, out_specs=out_specs, out_shape=out_shape,
        compiler_params=_cparams(1), name="inproj_c" if combine else "inproj")(*args)
    return (res[0], res[1]) if combine else (res[0], h)


def _pool_kernel(x_ref, w_ref, sc_ref, o_ref):
    g = pl.program_id(1)
    x = x_ref[...].astype(f32)
    t = lax.broadcasted_iota(i32, x.shape, 0)

    def lagged(v, k):
        return jnp.where(t >= k, pltpu.roll(v, k, axis=0), 0.0)

    s2 = x + lagged(x, 1)
    s4 = s2 + lagged(s2, 2)
    s8 = s4 + lagged(s4, 4)
    s16 = s8 + lagged(s8, 8)
    wsum = jnp.where(g == 0, s2, jnp.where(g == 1, s4, jnp.where(g == 2, s8, s16)))
    win = jnp.where(g == 0, 2, jnp.where(g == 1, 4, jnp.where(g == 2, 8, 16)))
    cnt = jnp.minimum(t + 1, win).astype(f32)
    d = (wsum / cnt - x).astype(bf16)
    y = jnp.dot(d, w_ref[...], preferred_element_type=f32) * sc_ref[...]
    o_ref[...] = y.astype(bf16)


def _pool(proj, w, scale, B, S):
    T = B * S
    return pl.pallas_call(
        _pool_kernel, grid=(B, POOL_GROUPS),
        in_specs=[pl.BlockSpec((S, LANE), lambda b, g: (b, g)),
                  pl.BlockSpec((None, LANE, LANE), lambda b, g: (g, 0, 0)),
                  pl.BlockSpec((1, LANE), lambda b, g: (0, g))],
        out_specs=pl.BlockSpec((S, LANE), lambda b, g: (b, g)),
        out_shape=jax.ShapeDtypeStruct((T, POOL_WIDTH), bf16),
        compiler_params=_cparams(2), name="pool")(proj, w, scale)


def _rope128(r, cos, sin):
    lane = lax.broadcasted_iota(i32, r.shape, 1)
    half = MLA_ROPE // 2
    partner = jnp.where(lane < half, -pltpu.roll(r, LANE - half, axis=1), pltpu.roll(r, half, axis=1))
    return r * cos + partner * sin


def _mla_prep_kernel(cq_ref, ckv_ref, kr_ref, cos_ref, sin_ref, qn_ref, kvn_ref, wq_ref, wk_ref, wv_ref,
                     q_ref, k_ref, v_ref):
    cos, sin = cos_ref[...], sin_ref[...]
    scale = (MLA_NOPE + MLA_ROPE) ** -0.5
    cq = _rms(cq_ref[...].astype(f32), qn_ref[...]).astype(bf16)
    ckv = _rms(ckv_ref[...].astype(f32), kvn_ref[...]).astype(bf16)
    kr = _rope128(kr_ref[...].astype(f32), cos, sin).astype(bf16)
    v_ref[...] = jnp.dot(ckv, wv_ref[...], preferred_element_type=f32).astype(bf16)
    for h in range(MLA_HEADS):
        c0 = h * QK_PAD
        q = jnp.dot(cq, wq_ref[:, c0:c0 + QK_PAD], preferred_element_type=f32)
        q_ref[:, c0:c0 + LANE] = (q[:, :LANE] * scale).astype(bf16)
        q_ref[:, c0 + LANE:c0 + QK_PAD] = (_rope128(q[:, LANE:], cos, sin) * scale).astype(bf16)
        kn = jnp.dot(ckv, wk_ref[:, h * LANE:(h + 1) * LANE], preferred_element_type=f32)
        k_ref[:, c0:c0 + LANE] = kn.astype(bf16)
        k_ref[:, c0 + LANE:c0 + QK_PAD] = kr


def _mla_prep(proj, cos, sin, qn, kvn, wq, wk, wv):
    T = proj.shape[0]
    const = lambda i: (0, 0)
    nq, nk, nv = MLA_HEADS * QK_PAD, MLA_HEADS * LANE, MLA_HEADS * MLA_V
    return pl.pallas_call(
        _mla_prep_kernel, grid=(T // TM,),
        in_specs=[pl.BlockSpec((TM, MLA_Q_RANK), lambda i: (i, COL_CQ // MLA_Q_RANK)),
                  pl.BlockSpec((TM, MLA_KV_RANK), lambda i: (i, COL_CKV // MLA_KV_RANK)),
                  pl.BlockSpec((TM, LANE), lambda i: (i, COL_KR // LANE)),
                  pl.BlockSpec((TM, LANE), lambda i: (i, 0)),
                  pl.BlockSpec((TM, LANE), lambda i: (i, 0)),
                  pl.BlockSpec((1, MLA_Q_RANK), const),
                  pl.BlockSpec((1, MLA_KV_RANK), const),
                  pl.BlockSpec((MLA_Q_RANK, nq), const),
                  pl.BlockSpec((MLA_KV_RANK, nk), const),
                  pl.BlockSpec((MLA_KV_RANK, nv), const)],
        out_specs=[pl.BlockSpec((TM, nq), lambda i: (i, 0)),
                   pl.BlockSpec((TM, nq), lambda i: (i, 0)),
                   pl.BlockSpec((TM, nv), lambda i: (i, 0))],
        out_shape=[jax.ShapeDtypeStruct((T, nq), bf16), jax.ShapeDtypeStruct((T, nq), bf16),
                   jax.ShapeDtypeStruct((T, nv), bf16)],
        compiler_params=_cparams(1), name="mla_prep")(proj, proj, proj, cos, sin, qn, kvn, wq, wk, wv)


def _nt_dot(a, b):
    return lax.dot_general(a, b, (((1,), (1,)), ((), ())), preferred_element_type=f32)


def _mla_attn_kernel(q_ref, k_ref, v_ref, o_ref):
    i = pl.program_id(2)
    q = q_ref[...]
    row = i * TQ + lax.broadcasted_iota(i32, (TQ, TK), 0)
    col0 = lax.broadcasted_iota(i32, (TQ, TK), 1)

    def body(j, carry):
        m, l, acc = carry
        off = pl.multiple_of(j * TK, TK)
        s = _nt_dot(q, k_ref[pl.ds(off, TK), :])
        s = jnp.where(col0 + off <= row, s, -jnp.inf)
        m_new = jnp.maximum(m, jnp.max(s, axis=1, keepdims=True))
        p = jnp.exp(s - m_new)
        alpha = jnp.exp(m - m_new)
        l = alpha * l + jnp.sum(p, axis=1, keepdims=True)
        acc = alpha * acc + jnp.dot(p.astype(bf16), v_ref[pl.ds(off, TK), :], preferred_element_type=f32)
        return m_new, l, acc

    init = (jnp.full((TQ, 1), -jnp.inf, f32), jnp.zeros((TQ, 1), f32), jnp.zeros((TQ, MLA_V), f32))
    _, l, acc = lax.fori_loop(0, (i + 1) * (TQ // TK), body, init)
    o_ref[...] = (acc / l).astype(bf16)


def _mla_attn(q, k, v, B, S):
    T = B * S
    nq = S // TQ
    return pl.pallas_call(
        _mla_attn_kernel, grid=(B, MLA_HEADS, nq),
        in_specs=[pl.BlockSpec((TQ, QK_PAD), lambda b, h, i: (b * nq + i, h)),
                  pl.BlockSpec((S, QK_PAD), lambda b, h, i: (b, h)),
                  pl.BlockSpec((S, MLA_V), lambda b, h, i: (b, h))],
        out_specs=pl.BlockSpec((TQ, MLA_V), lambda b, h, i: (b * nq + i, h)),
        out_shape=jax.ShapeDtypeStruct((T, MLA_HEADS * MLA_V), bf16),
        compiler_params=_cparams(3), name="mla_attn")(q, k, v)


def _diff_attn_kernel(lam_ref, q_ref, k_ref, v_ref, posq_ref, posk_ref, tab_ref, sub_ref, o_ref, *, lam_init):
    h = pl.program_id(1)
    i = pl.program_id(2)
    lv = lam_ref[...]
    lam = (jnp.exp(jnp.sum(lv[0:1] * lv[1:2], axis=1, keepdims=True))
           - jnp.exp(jnp.sum(lv[2:3] * lv[3:4], axis=1, keepdims=True)) + lam_init)
    q = q_ref[...] * (DIFF_QK ** -0.5)
    lane = lax.broadcasted_iota(i32, q.shape, 1)
    zero = jnp.zeros_like(q)
    qq = jnp.concatenate([jnp.where(lane < DIFF_QK, q, zero), jnp.where(lane >= DIFF_QK, q, zero)], axis=0)
    posq = posq_ref[...]
    tab = jnp.broadcast_to(tab_ref[pl.ds(h, 1), :], (TQ, LANE))
    row = i * TQ + lax.broadcasted_iota(i32, (TQ, TK), 0)
    col0 = lax.broadcasted_iota(i32, (TQ, TK), 1)

    def body(j, carry):
        m0, l0, a0, m1, l1, a1 = carry
        off = pl.multiple_of(j * TK, TK)
        s2 = _nt_dot(qq, k_ref[pl.ds(off, TK), :])
        n = jnp.clip(posq - posk_ref[j], 0, REL_CLIP)
        bias = jnp.concatenate(
            [jnp.take_along_axis(tab, n[:, c * LANE:(c + 1) * LANE], axis=1) for c in range(TK // LANE)], axis=1)
        visible = col0 + off <= row
        v = v_ref[pl.ds(off, TK), :]

        def update(s, m, l, acc):
            s = jnp.where(visible, s + bias, -jnp.inf)
            m_new = jnp.maximum(m, jnp.max(s, axis=1, keepdims=True))
            p = jnp.exp(s - m_new)
            alpha = jnp.exp(m - m_new)
            l = alpha * l + jnp.sum(p, axis=1, keepdims=True)
            acc = alpha * acc + jnp.dot(p.astype(bf16), v, preferred_element_type=f32)
            return m_new, l, acc

        return update(s2[:TQ], m0, l0, a0) + update(s2[TQ:], m1, l1, a1)

    one = (jnp.full((TQ, 1), -jnp.inf, f32), jnp.zeros((TQ, 1), f32), jnp.zeros((TQ, DIFF_V), f32))
    _, l0, a0, _, l1, a1 = lax.fori_loop(0, (i + 1) * (TQ // TK), body, one + one)
    o = a0 / l0 - lam * (a1 / l1)
    o_ref[...] = (_rms(o, sub_ref[...]) * (1.0 - lam_init)).astype(bf16)


def _diff_attn(proj, lam_vecs, posq, posk, tab, subln, lam_init, B, S):
    T = B * S
    nq = S // TQ
    cq, ck, cv = COL_DQ // LANE, COL_DK // LANE, COL_DV // LANE
    return pl.pallas_call(
        functools.partial(_diff_attn_kernel, lam_init=lam_init), grid=(B, DIFF_HEADS, nq),
        in_specs=[pl.BlockSpec((4, DIFF_QK), lambda b, h, i: (0, 0)),
                  pl.BlockSpec((TQ, LANE), lambda b, h, i: (b * nq + i, cq + h)),
                  pl.BlockSpec((S, LANE), lambda b, h, i: (b, ck + h)),
                  pl.BlockSpec((S, LANE), lambda b, h, i: (b, cv + h)),
                  pl.BlockSpec((TQ, 1), lambda b, h, i: (b * nq + i, 0)),
                  pl.BlockSpec((None, S // TK, 1, TK), lambda b, h, i: (b, 0, 0, 0)),
                  pl.BlockSpec((8, LANE), lambda b, h, i: (0, 0)),
                  pl.BlockSpec((1, DIFF_V), lambda b, h, i: (0, 0))],
        out_specs=pl.BlockSpec((TQ, DIFF_V), lambda b, h, i: (b * nq + i, h)),
        out_shape=jax.ShapeDtypeStruct((T, DIFF_HEADS * DIFF_V), bf16),
        compiler_params=_cparams(3), name="diff_attn")(lam_vecs, proj, proj, proj, posq, posk, tab, subln)


ROUTER_ROWS = 64


def _outproj_router_kernel(a_ref, b_ref, c_ref, wa_ref, wb_ref, wc_ref, h_ref, g_ref, wr_ref, br_ref,
                           h2_ref, xn_ref, eid_ref, gate_ref):
    mixed = (jnp.dot(a_ref[...], wa_ref[...], preferred_element_type=f32)
             + jnp.dot(b_ref[...], wb_ref[...], preferred_element_type=f32)
             + jnp.dot(c_ref[...], wc_ref[...], preferred_element_type=f32))
    h2 = h_ref[...] + mixed
    h2_ref[...] = h2
    xn = _rms(h2, g_ref[...])
    xn_ref[...] = xn
    xh = xn.astype(bf16)
    xl = (xn - xh.astype(f32)).astype(bf16)
    wr = wr_ref[...]
    ah, al = _nt_dot(wr, xh), _nt_dot(wr, xl)
    logit = ah[:ROUTER_ROWS] + ah[ROUTER_ROWS:] + al[:ROUTER_ROWS] + br_ref[...]

    iota8 = lax.broadcasted_iota(i32, (8, TM), 0).astype(f32)
    gl = jnp.where(iota8 < N_GROUPS, logit[0:8], -jnp.inf)
    gmax = jnp.max(gl, axis=0, keepdims=True)
    gidx = jnp.min(jnp.where(gl == gmax, iota8, 8.0), axis=0, keepdims=True)
    ggate = 1.0 / jnp.sum(jnp.exp(gl - gmax), axis=0, keepdims=True)

    e = jnp.zeros((EXPERTS_PER_GROUP, TM), f32)
    for g in range(N_GROUPS):
        e = jnp.where(gidx == g, logit[8 + 8 * g:16 + 8 * g], e)
    m1 = jnp.max(e, axis=0, keepdims=True)
    i1 = jnp.min(jnp.where(e == m1, iota8, 8.0), axis=0, keepdims=True)
    e2 = jnp.where(iota8 == i1, -jnp.inf, e)
    m2 = jnp.max(e2, axis=0, keepdims=True)
    i2 = jnp.min(jnp.where(e2 == m2, iota8, 8.0), axis=0, keepdims=True)
    r = jnp.exp(m2 - m1)
    p1 = 1.0 / (1.0 + r)
    p2 = r / (1.0 + r)
    eid_ref[0:1, :] = (gidx * EXPERTS_PER_GROUP + i1).astype(i32)
    eid_ref[1:2, :] = (gidx * EXPERTS_PER_GROUP + i2).astype(i32)
    gate_ref[0:1, :] = ggate * p1
    gate_ref[1:2, :] = ggate * p2


def _outproj_router(a, b, c, wa, wb, wc, h, g, wr, br):
    T = h.shape[0]
    row = lambda i: (i, 0)
    const = lambda i: (0, 0)
    na, nb, nc = a.shape[1], b.shape[1], c.shape[1]
    return pl.pallas_call(
        _outproj_router_kernel, grid=(T // TM,),
        in_specs=[pl.BlockSpec((TM, na), row), pl.BlockSpec((TM, nb), row), pl.BlockSpec((TM, nc), row),
                  pl.BlockSpec((na, D_MODEL), const, pipeline_mode=pl.Buffered(1)),
                  pl.BlockSpec((nb, D_MODEL), const, pipeline_mode=pl.Buffered(1)),
                  pl.BlockSpec((nc, D_MODEL), const, pipeline_mode=pl.Buffered(1)),
                  pl.BlockSpec((TM, D_MODEL), row),
                  pl.BlockSpec((1, D_MODEL), const),
                  pl.BlockSpec((2 * ROUTER_ROWS, D_MODEL), const),
                  pl.BlockSpec((ROUTER_ROWS, 1), const)],
        out_specs=[pl.BlockSpec((TM, D_MODEL), row), pl.BlockSpec((TM, D_MODEL), row),
                   pl.BlockSpec((TOP_K, TM), lambda i: (0, i)), pl.BlockSpec((TOP_K, TM), lambda i: (0, i))],
        out_shape=[jax.ShapeDtypeStruct((T, D_MODEL), f32), jax.ShapeDtypeStruct((T, D_MODEL), f32),
                   jax.ShapeDtypeStruct((TOP_K, T), i32), jax.ShapeDtypeStruct((TOP_K, T), f32)],
        compiler_params=_cparams(1), name="outproj_router")(a, b, c, wa, wb, wc, h, g, wr, br)


def _moe_kernel(slot_ref, be_ref, gate_ref, x_hbm, wg_ref, wu_ref, wd_ref, y_hbm, xbuf, ybuf, gsem, ssem,
                *, n_tokens, n_blocks):
    del be_ref
    b = pl.program_id(0)
    slot = b % 2
    n_assign = TOP_K * n_tokens

    def gather_row(blk, buf, r):
        a = slot_ref[blk * MOE_BLK + r]
        tok = jnp.minimum(a >> 1, n_tokens - 1)
        return pltpu.make_async_copy(x_hbm.at[pl.ds(tok, 1)], xbuf.at[buf, pl.ds(r, 1)], gsem.at[buf])

    def scatter_row(blk, buf, r):
        a = slot_ref[blk * MOE_BLK + r]
        dst = jnp.where(a < n_assign, a, n_assign + buf * MOE_BLK + r)
        return pltpu.make_async_copy(ybuf.at[buf, pl.ds(r, 1)], y_hbm.at[pl.ds(dst, 1)], ssem.at[buf])

    def start_gather(blk, buf):
        def body(r, carry):
            gather_row(blk, buf, r).start()
            return carry
        lax.fori_loop(0, MOE_BLK, body, 0, unroll=8)

    def wait_rows(make_row, blk, buf):
        def body(r, carry):
            make_row(blk, buf, r).wait()
            return carry
        lax.fori_loop(0, MOE_BLK, body, 0, unroll=8)

    @pl.when(b == 0)
    def _():
        start_gather(0, 0)

    @pl.when(b + 1 < n_blocks)
    def _():
        start_gather(b + 1, 1 - slot)

    wait_rows(gather_row, b, slot)

    @pl.when(b >= 2)
    def _():
        wait_rows(scatter_row, b - 2, slot)

    x = xbuf[slot]
    hg = jnp.dot(x, wg_ref[...], preferred_element_type=f32)
    hu = jnp.dot(x, wu_ref[...], preferred_element_type=f32)
    hid = hg * jax.nn.sigmoid(hg) * hu * gate_ref[...]
    ybuf[slot] = jnp.dot(hid, wd_ref[...], preferred_element_type=f32)

    def body(r, carry):
        scatter_row(b, slot, r).start()
        return carry
    lax.fori_loop(0, MOE_BLK, body, 0, unroll=8)

    @pl.when(b == n_blocks - 1)
    def _():
        if n_blocks > 1:
            wait_rows(scatter_row, b - 1, 1 - slot)
        wait_rows(scatter_row, b, slot)


def _moe(xn, slot_a, block_e, slot_gate, w_gate, w_up, w_down, layer):
    T = xn.shape[0]
    n_blocks = block_e.shape[0]
    wspec = lambda shape: pl.BlockSpec((None, None) + shape, lambda b, s, e: (layer, e[b], 0, 0))
    grid_spec = pltpu.PrefetchScalarGridSpec(
        num_scalar_prefetch=2, grid=(n_blocks,),
        in_specs=[pl.BlockSpec((MOE_BLK, 1), lambda b, s, e: (b, 0)),
                  pl.BlockSpec(memory_space=pl.ANY),
                  wspec((D_MODEL, EXPERT_HIDDEN)), wspec((D_MODEL, EXPERT_HIDDEN)), wspec((EXPERT_HIDDEN, D_MODEL))],
        out_specs=pl.BlockSpec(memory_space=pl.ANY),
        scratch_shapes=[pltpu.VMEM((2, MOE_BLK, D_MODEL), f32), pltpu.VMEM((2, MOE_BLK, D_MODEL), f32),
                        pltpu.SemaphoreType.DMA((2,)), pltpu.SemaphoreType.DMA((2,))])
    return pl.pallas_call(
        functools.partial(_moe_kernel, n_tokens=T, n_blocks=n_blocks),
        grid_spec=grid_spec,
        out_shape=jax.ShapeDtypeStruct((TOP_K * T + 2 * MOE_BLK, D_MODEL), f32),
        compiler_params=_cparams(1), name="moe")(slot_a, block_e, slot_gate, xn, w_gate, w_up, w_down)


def _dispatch(eid, gate, T):
    n_assign = TOP_K * T
    e_flat = eid.T.reshape(-1)
    g_flat = gate.T.reshape(-1)
    onehot = (e_flat[:, None] == jnp.arange(N_EXPERTS, dtype=i32)[None, :]).astype(i32)
    csum = jnp.cumsum(onehot, axis=0)
    rank = jnp.take_along_axis(csum, e_flat[:, None], axis=1)[:, 0] - 1
    counts = csum[-1]
    padded = (counts + MOE_BLK - 1) // MOE_BLK * MOE_BLK
    ends = jnp.cumsum(padded)
    dest = (ends - padded)[e_flat] + rank
    n_blocks = n_assign // MOE_BLK + N_EXPERTS
    P = n_blocks * MOE_BLK
    slot_a = jnp.full((P,), n_assign, i32).at[dest].set(jnp.arange(n_assign, dtype=i32))
    slot_gate = jnp.zeros((P,), f32).at[dest].set(g_flat)
    block_e = jnp.minimum(jnp.searchsorted(ends, jnp.arange(n_blocks, dtype=i32) * MOE_BLK, side='right'),
                          N_EXPERTS - 1).astype(i32)
    return slot_a, block_e, slot_gate.reshape(P, 1)


def _final_kernel(h_ref, y_ref, g_ref, o_ref):
    h = h_ref[...] + y_ref[:, :D_MODEL] + y_ref[:, D_MODEL:]
    o_ref[...] = _rms(h, g_ref[...])


def _final(h, y2, g):
    T = h.shape[0]
    row = lambda i: (i, 0)
    return pl.pallas_call(
        _final_kernel, grid=(T // TM,),
        in_specs=[pl.BlockSpec((TM, D_MODEL), row), pl.BlockSpec((TM, 2 * D_MODEL), row),
                  pl.BlockSpec((1, D_MODEL), lambda i: (0, 0))],
        out_specs=pl.BlockSpec((TM, D_MODEL), row),
        out_shape=jax.ShapeDtypeStruct((T, D_MODEL), f32),
        compiler_params=_cparams(1), name="final_norm")(h, y2, g)


def _t5_bucket(n):
    nf = jnp.maximum(n, 1).astype(f32)
    large = REL_MAX_EXACT + (jnp.log(nf / REL_MAX_EXACT) / math.log(REL_MAX_DISTANCE / REL_MAX_EXACT)
                             * (REL_BUCKETS - REL_MAX_EXACT)).astype(i32)
    return jnp.where(n < REL_MAX_EXACT, n, jnp.minimum(large, REL_BUCKETS - 1))


def _pad_cols(w, width):
    return jnp.pad(w, ((0, 0), (0, width - w.shape[1])))


def _split_hi_lo(w):
    hi = w.astype(bf16)
    return hi, (w - hi.astype(f32)).astype(bf16)


def kernel(x, positions, attn_norm, w_in, pool_w, pool_scale, mla_q_norm, mla_w_uq, mla_kv_norm, mla_w_ukv,
           diff_lambda, diff_subln, rel_bias, w_out, ffn_norm, router_group_w, router_group_b,
           router_expert_w, router_expert_b, expert_w_gate, expert_w_up, expert_w_down, final_norm):
    B, S, D = x.shape
    T = B * S
    depth = w_in.shape[0]
    assert D == D_MODEL and S % TQ == 0 and T % TM == 0 and (TOP_K * T) % MOE_BLK == 0

    inv = 1.0 / (ROPE_THETA ** (jnp.arange(0, MLA_ROPE, 2, dtype=f32) / MLA_ROPE))
    ang = positions.astype(f32).reshape(T, 1) * inv
    zeros = jnp.zeros((T, LANE - MLA_ROPE), f32)
    cos = jnp.concatenate([jnp.cos(ang), jnp.cos(ang), zeros], axis=1)
    sin = jnp.concatenate([jnp.sin(ang), jnp.sin(ang), zeros], axis=1)
    posq = positions.reshape(T, 1)
    posk = positions.reshape(B, S // TK, 1, TK)
    tab = jnp.pad(rel_bias[_t5_bucket(jnp.arange(LANE, dtype=i32))].T.astype(f32), ((0, 8 - DIFF_HEADS), (0, 0)))

    h = x.reshape(T, D)
    y2 = None
    for l in range(depth):
        wl = w_in[l]
        w_in_p = jnp.concatenate([wl[:, :COL_KR + MLA_ROPE], jnp.zeros((D, LANE - MLA_ROPE), f32),
                                  wl[:, COL_KR + MLA_ROPE:]], axis=1).astype(bf16)
        wq = mla_w_uq[l].reshape(MLA_Q_RANK, MLA_HEADS, MLA_NOPE + MLA_ROPE)
        wq = jnp.pad(wq, ((0, 0), (0, 0), (0, QK_PAD - MLA_NOPE - MLA_ROPE))).reshape(MLA_Q_RANK, -1).astype(bf16)
        wkv = mla_w_ukv[l].reshape(MLA_KV_RANK, MLA_HEADS, MLA_NOPE + MLA_V)
        wk = wkv[:, :, :MLA_NOPE].reshape(MLA_KV_RANK, -1).astype(bf16)
        wv = wkv[:, :, MLA_NOPE:].reshape(MLA_KV_RANK, -1).astype(bf16)
        wo = w_out[l].astype(bf16)
        wa, wb, wc = wo[:POOL_WIDTH], wo[POOL_WIDTH:POOL_WIDTH + MLA_HEADS * MLA_V], wo[POOL_WIDTH + MLA_HEADS * MLA_V:]
        wr = jnp.zeros((D, ROUTER_ROWS), f32).at[:, :N_GROUPS].set(router_group_w[l]).at[:, 8:8 + N_EXPERTS].set(router_expert_w[l])
        wr_hi, wr_lo = _split_hi_lo(wr)
        wr_t = jnp.concatenate([wr_hi, wr_lo], axis=1).T
        br = jnp.zeros((ROUTER_ROWS,), f32).at[:N_GROUPS].set(router_group_b[l]).at[8:8 + N_EXPERTS].set(router_expert_b[l])
        lam_init = 0.8 - 0.6 * math.exp(-0.3 * l)

        proj, h = _inproj(h, y2, attn_norm[l].reshape(1, D), w_in_p)
        a_out = _pool(proj, pool_w[l].astype(bf16), pool_scale[l].reshape(1, POOL_WIDTH), B, S)
        q, k, v = _mla_prep(proj, cos, sin, mla_q_norm[l].reshape(1, -1), mla_kv_norm[l].reshape(1, -1), wq, wk, wv)
        b_out = _mla_attn(q, k, v, B, S)
        c_out = _diff_attn(proj, diff_lambda[l], posq, posk, tab, diff_subln[l].reshape(1, -1), lam_init, B, S)
        h, xn, eid, gate = _outproj_router(a_out, b_out, c_out, wa, wb, wc, h, ffn_norm[l].reshape(1, D),
                                           wr_t, br.reshape(ROUTER_ROWS, 1))
        slot_a, block_e, slot_gate = _dispatch(eid, gate, T)
        y = _moe(xn, slot_a, block_e, slot_gate, expert_w_gate, expert_w_up, expert_w_down, l)
        y2 = y.reshape(-1, 2 * D)
    out = _final(h, y2, final_norm.reshape(1, D))
    return out.reshape(B, S, D)
```

```python
import functools
import math

import jax
import jax.numpy as jnp
from jax import lax
from jax.experimental import pallas as pl
from jax.experimental.pallas import tpu as pltpu

f32, bf16, i32 = jnp.float32, jnp.bfloat16, jnp.int32

D_MODEL = 2048
POOL_WINDOWS = (2, 4, 8, 16)
POOL_GROUPS = 4
POOL_WIDTH = 512
MLA_NOPE, MLA_ROPE, MLA_V, MLA_HEADS = 128, 64, 128, 6
MLA_Q_RANK, MLA_KV_RANK = 512, 256
ROPE_THETA = 10000.0
DIFF_QK, DIFF_V, DIFF_HEADS = 64, 128, 6
REL_BUCKETS, REL_MAX_EXACT, REL_MAX_DISTANCE = 32, 16, 128
N_GROUPS, EXPERTS_PER_GROUP, N_EXPERTS, TOP_K, EXPERT_HIDDEN = 4, 8, 32, 2, 512
NORM_EPS = 1e-6

LANE = 128
MXU_DIM = 256
VMEM_LIMIT = 56 * 1024 * 1024

N_IN_PAD = 3712
COL_CQ, COL_CKV, COL_KR, COL_DQ, COL_DK, COL_DV = 512, 1024, 1280, 1408, 2176, 2944
QK_PAD = 2 * LANE

TM = 256
TQ = 512
TK = 256
MOE_BLK = 256
REL_CLIP = LANE - 1


def _cparams(n_axes):
    return pltpu.CompilerParams(dimension_semantics=("arbitrary",) * n_axes, vmem_limit_bytes=VMEM_LIMIT)


def _rms(x, g):
    return x * lax.rsqrt(jnp.mean(x * x, axis=-1, keepdims=True) + NORM_EPS) * g


def _combine(h_ref, y0_ref, y1_ref, gate_ref):
    gate = gate_ref[...]
    return h_ref[...] + y0_ref[...] * gate[:, 0:1] + y1_ref[...] * gate[:, 1:2]


def _combine_specs(T):
    plane1 = T // TM
    return [pl.BlockSpec((TM, D_MODEL), lambda i: (i, 0)),
            pl.BlockSpec((TM, D_MODEL), lambda i: (i, 0)),
            pl.BlockSpec((TM, D_MODEL), lambda i: (plane1 + i, 0)),
            pl.BlockSpec((TM, TOP_K), lambda i: (i, 0))]


def _inproj_kernel(*refs, combine):
    if combine:
        h_ref, y0_ref, y1_ref, gate_ref, g_ref, w_ref, proj_ref, hout_ref = refs
        h = _combine(h_ref, y0_ref, y1_ref, gate_ref)
        hout_ref[...] = h
    else:
        h_ref, g_ref, w_ref, proj_ref = refs
        h = h_ref[...]
    u = _rms(h, g_ref[...]).astype(bf16)
    for n0 in range(0, N_IN_PAD, 512):
        n1 = min(n0 + 512, N_IN_PAD)
        proj_ref[:, n0:n1] = jnp.dot(u, w_ref[:, n0:n1], preferred_element_type=f32).astype(bf16)


def _inproj(h, moe_out, g, w):
    T = h.shape[0]
    combine = moe_out is not None
    row = lambda i: (i, 0)
    const = lambda i: (0, 0)
    if combine:
        y, gate = moe_out
        in_specs = _combine_specs(T)
        args = [h, y, y, gate]
    else:
        in_specs = [pl.BlockSpec((TM, D_MODEL), row)]
        args = [h]
    in_specs += [pl.BlockSpec((1, D_MODEL), const),
                 pl.BlockSpec((D_MODEL, N_IN_PAD), const, pipeline_mode=pl.Buffered(1))]
    args += [g, w]
    out_shape = [jax.ShapeDtypeStruct((T, N_IN_PAD), bf16)]
    out_specs = [pl.BlockSpec((TM, N_IN_PAD), row)]
    if combine:
        out_shape.append(jax.ShapeDtypeStruct((T, D_MODEL), f32))
        out_specs.append(pl.BlockSpec((TM, D_MODEL), row))
    res = pl.pallas_call(
        functools.partial(_inproj_kernel, combine=combine),
        grid=(T // TM,), in_specs=in_specs, out_specs=out_specs, out_shape=out_shape,
        compiler_params=_cparams(1), name="inproj_c" if combine else "inproj")(*args)
    return (res[0], res[1]) if combine else (res[0], h)


def _pool_kernel(x_ref, w_ref, sc_ref, o_ref):
    g = pl.program_id(1)
    x = x_ref[...].astype(f32)
    t = lax.broadcasted_iota(i32, x.shape, 0)

    def lagged(v, k):
        return jnp.where(t >= k, pltpu.roll(v, k, axis=0), 0.0)

    s2 = x + lagged(x, 1)
    s4 = s2 + lagged(s2, 2)
    s8 = s4 + lagged(s4, 4)
    s16 = s8 + lagged(s8, 8)
    wsum = jnp.where(g == 0, s2, jnp.where(g == 1, s4, jnp.where(g == 2, s8, s16)))
    win = jnp.where(g == 0, 2, jnp.where(g == 1, 4, jnp.where(g == 2, 8, 16)))
    cnt = jnp.minimum(t + 1, win).astype(f32)
    d = (wsum / cnt - x).astype(bf16)
    y = jnp.dot(d, w_ref[...], preferred_element_type=f32) * sc_ref[...]
    o_ref[...] = y.astype(bf16)


def _pool(proj, w, scale, B, S):
    T = B * S
    return pl.pallas_call(
        _pool_kernel, grid=(B, POOL_GROUPS),
        in_specs=[pl.BlockSpec((S, LANE), lambda b, g: (b, g)),
                  pl.BlockSpec((None, LANE, LANE), lambda b, g: (g, 0, 0)),
                  pl.BlockSpec((1, LANE), lambda b, g: (0, g))],
        out_specs=pl.BlockSpec((S, LANE), lambda b, g: (b, g)),
        out_shape=jax.ShapeDtypeStruct((T, POOL_WIDTH), bf16),
        compiler_params=_cparams(2), name="pool")(proj, w, scale)


def _rope128(r, cos, sin):
    lane = lax.broadcasted_iota(i32, r.shape, 1)
    half = MLA_ROPE // 2
    partner = jnp.where(lane < half, -pltpu.roll(r, LANE - half, axis=1), pltpu.roll(r, half, axis=1))
    return r * cos + partner * sin


def _mla_prep_kernel(cq_ref, ckv_ref, kr_ref, cos_ref, sin_ref, qn_ref, kvn_ref, wq_ref, wk_ref, wv_ref,
                     q_ref, k_ref, v_ref):
    cos, sin = cos_ref[...], sin_ref[...]
    scale = (MLA_NOPE + MLA_ROPE) ** -0.5
    cq = _rms(cq_ref[...].astype(f32), qn_ref[...]).astype(bf16)
    ckv = _rms(ckv_ref[...].astype(f32), kvn_ref[...]).astype(bf16)
    kr = _rope128(kr_ref[...].astype(f32), cos, sin).astype(bf16)
    v_ref[...] = jnp.dot(ckv, wv_ref[...], preferred_element_type=f32).astype(bf16)
    for h in range(MLA_HEADS):
        c0 = h * QK_PAD
        q = jnp.dot(cq, wq_ref[:, c0:c0 + QK_PAD], preferred_element_type=f32)
        q_ref[:, c0:c0 + LANE] = (q[:, :LANE] * scale).astype(bf16)
        q_ref[:, c0 + LANE:c0 + QK_PAD] = (_rope128(q[:, LANE:], cos, sin) * scale).astype(bf16)
        kn = jnp.dot(ckv, wk_ref[:, h * LANE:(h + 1) * LANE], preferred_element_type=f32)
        k_ref[:, c0:c0 + LANE] = kn.astype(bf16)
        k_ref[:, c0 + LANE:c0 + QK_PAD] = kr


def _mla_prep(proj, cos, sin, qn, kvn, wq, wk, wv):
    T = proj.shape[0]
    const = lambda i: (0, 0)
    nq, nk, nv = MLA_HEADS * QK_PAD, MLA_HEADS * LANE, MLA_HEADS * MLA_V
    return pl.pallas_call(
        _mla_prep_kernel, grid=(T // TM,),
        in_specs=[pl.BlockSpec((TM, MLA_Q_RANK), lambda i: (i, COL_CQ // MLA_Q_RANK)),
                  pl.BlockSpec((TM, MLA_KV_RANK), lambda i: (i, COL_CKV // MLA_KV_RANK)),
                  pl.BlockSpec((TM, LANE), lambda i: (i, COL_KR // LANE)),
                  pl.BlockSpec((TM, LANE), lambda i: (i, 0)),
                  pl.BlockSpec((TM, LANE), lambda i: (i, 0)),
                  pl.BlockSpec((1, MLA_Q_RANK), const),
                  pl.BlockSpec((1, MLA_KV_RANK), const),
                  pl.BlockSpec((MLA_Q_RANK, nq), const),
                  pl.BlockSpec((MLA_KV_RANK, nk), const),
                  pl.BlockSpec((MLA_KV_RANK, nv), const)],
        out_specs=[pl.BlockSpec((TM, nq), lambda i: (i, 0)),
                   pl.BlockSpec((TM, nq), lambda i: (i, 0)),
                   pl.BlockSpec((TM, nv), lambda i: (i, 0))],
        out_shape=[jax.ShapeDtypeStruct((T, nq), bf16), jax.ShapeDtypeStruct((T, nq), bf16),
                   jax.ShapeDtypeStruct((T, nv), bf16)],
        compiler_params=_cparams(1), name="mla_prep")(proj, proj, proj, cos, sin, qn, kvn, wq, wk, wv)


def _nt_dot(a, b):
    return lax.dot_general(a, b, (((1,), (1,)), ((), ())), preferred_element_type=f32)


def _mla_attn_kernel(q_ref, k_ref, v_ref, o_ref):
    i = pl.program_id(2)
    q = q_ref[...]
    row = i * TQ + lax.broadcasted_iota(i32, (TQ, TK), 0)
    col0 = lax.broadcasted_iota(i32, (TQ, TK), 1)

    def body(j, carry):
        m, l, acc = carry
        off = pl.multiple_of(j * TK, TK)
        s = _nt_dot(q, k_ref[pl.ds(off, TK), :])
        s = jnp.where(col0 + off <= row, s, -jnp.inf)
        m_new = jnp.maximum(m, jnp.max(s, axis=1, keepdims=True))
        p = jnp.exp(s - m_new)
        alpha = jnp.exp(m - m_new)
        l = alpha * l + jnp.sum(p, axis=1, keepdims=True)
        acc = alpha * acc + jnp.dot(p.astype(bf16), v_ref[pl.ds(off, TK), :], preferred_element_type=f32)
        return m_new, l, acc

    init = (jnp.full((TQ, 1), -jnp.inf, f32), jnp.zeros((TQ, 1), f32), jnp.zeros((TQ, MLA_V), f32))
    _, l, acc = lax.fori_loop(0, (i + 1) * (TQ // TK), body, init)
    o_ref[...] = (acc / l).astype(bf16)


def _mla_attn(q, k, v, B, S):
    T = B * S
    nq = S // TQ
    return pl.pallas_call(
        _mla_attn_kernel, grid=(B, MLA_HEADS, nq),
        in_specs=[pl.BlockSpec((TQ, QK_PAD), lambda b, h, i: (b * nq + i, h)),
                  pl.BlockSpec((S, QK_PAD), lambda b, h, i: (b, h)),
                  pl.BlockSpec((S, MLA_V), lambda b, h, i: (b, h))],
        out_specs=pl.BlockSpec((TQ, MLA_V), lambda b, h, i: (b * nq + i, h)),
        out_shape=jax.ShapeDtypeStruct((T, MLA_HEADS * MLA_V), bf16),
        compiler_params=_cparams(3), name="mla_attn")(q, k, v)


def _diff_attn_kernel(lam_ref, q_ref, k_ref, v_ref, posq_ref, posk_ref, tab_ref, sub_ref, o_ref, *, lam_init):
    h = pl.program_id(1)
    i = pl.program_id(2)
    lv = lam_ref[...]
    lam = (jnp.exp(jnp.sum(lv[0:1] * lv[1:2], axis=1, keepdims=True))
           - jnp.exp(jnp.sum(lv[2:3] * lv[3:4], axis=1, keepdims=True)) + lam_init)
    q = q_ref[...] * (DIFF_QK ** -0.5)
    lane = lax.broadcasted_iota(i32, q.shape, 1)
    zero = jnp.zeros_like(q)
    qq = jnp.concatenate([jnp.where(lane < DIFF_QK, q, zero), jnp.where(lane >= DIFF_QK, q, zero)], axis=0)
    posq = posq_ref[...]
    tab = jnp.broadcast_to(tab_ref[pl.ds(h, 1), :], (TQ, LANE))
    row = i * TQ + lax.broadcasted_iota(i32, (TQ, TK), 0)
    col0 = lax.broadcasted_iota(i32, (TQ, TK), 1)

    def body(j, carry):
        m0, l0, a0, m1, l1, a1 = carry
        off = pl.multiple_of(j * TK, TK)
        s2 = _nt_dot(qq, k_ref[pl.ds(off, TK), :])
        n = jnp.clip(posq - posk_ref[j], 0, REL_CLIP)
        bias = jnp.concatenate(
            [jnp.take_along_axis(tab, n[:, c * LANE:(c + 1) * LANE], axis=1) for c in range(TK // LANE)], axis=1)
        visible = col0 + off <= row
        v = v_ref[pl.ds(off, TK), :]

        def update(s, m, l, acc):
            s = jnp.where(visible, s + bias, -jnp.inf)
            m_new = jnp.maximum(m, jnp.max(s, axis=1, keepdims=True))
            p = jnp.exp(s - m_new)
            alpha = jnp.exp(m - m_new)
            l = alpha * l + jnp.sum(p, axis=1, keepdims=True)
            acc = alpha * acc + jnp.dot(p.astype(bf16), v, preferred_element_type=f32)
            return m_new, l, acc

        return update(s2[:TQ], m0, l0, a0) + update(s2[TQ:], m1, l1, a1)

    one = (jnp.full((TQ, 1), -jnp.inf, f32), jnp.zeros((TQ, 1), f32), jnp.zeros((TQ, DIFF_V), f32))
    _, l0, a0, _, l1, a1 = lax.fori_loop(0, (i + 1) * (TQ // TK), body, one + one)
    o = a0 / l0 - lam * (a1 / l1)
    o_ref[...] = (_rms(o, sub_ref[...]) * (1.0 - lam_init)).astype(bf16)


def _diff_attn(proj, lam_vecs, posq, posk, tab, subln, lam_init, B, S):
    T = B * S
    nq = S // TQ
    cq, ck, cv = COL_DQ // LANE, COL_DK // LANE, COL_DV // LANE
    return pl.pallas_call(
        functools.partial(_diff_attn_kernel, lam_init=lam_init), grid=(B, DIFF_HEADS, nq),
        in_specs=[pl.BlockSpec((4, DIFF_QK), lambda b, h, i: (0, 0)),
                  pl.BlockSpec((TQ, LANE), lambda b, h, i: (b * nq + i, cq + h)),
                  pl.BlockSpec((S, LANE), lambda b, h, i: (b, ck + h)),
                  pl.BlockSpec((S, LANE), lambda b, h, i: (b, cv + h)),
                  pl.BlockSpec((TQ, 1), lambda b, h, i: (b * nq + i, 0)),
                  pl.BlockSpec((None, S // TK, 1, TK), lambda b, h, i: (b, 0, 0, 0)),
                  pl.BlockSpec((8, LANE), lambda b, h, i: (0, 0)),
                  pl.BlockSpec((1, DIFF_V), lambda b, h, i: (0, 0))],
        out_specs=pl.BlockSpec((TQ, DIFF_V), lambda b, h, i: (b * nq + i, h)),
        out_shape=jax.ShapeDtypeStruct((T, DIFF_HEADS * DIFF_V), bf16),
        compiler_params=_cparams(3), name="diff_attn")(lam_vecs, proj, proj, proj, posq, posk, tab, subln)


ROUTER_ROWS = 64


def _outproj_router_kernel(a_ref, b_ref, c_ref, wa_ref, wb_ref, wc_ref, h_ref, g_ref, wr_ref, br_ref,
                           h2_ref, xn_ref, eid_ref, gate_ref):
    mixed = (jnp.dot(a_ref[...], wa_ref[...], preferred_element_type=f32)
             + jnp.dot(b_ref[...], wb_ref[...], preferred_element_type=f32)
             + jnp.dot(c_ref[...], wc_ref[...], preferred_element_type=f32))
    h2 = h_ref[...] + mixed
    h2_ref[...] = h2
    xn = _rms(h2, g_ref[...])
    xn_ref[...] = xn
    xh = xn.astype(bf16)
    xl = (xn - xh.astype(f32)).astype(bf16)
    wr = wr_ref[...]
    ah, al = _nt_dot(wr, xh), _nt_dot(wr, xl)
    logit = ah[:ROUTER_ROWS] + ah[ROUTER_ROWS:] + al[:ROUTER_ROWS] + br_ref[...]

    iota8 = lax.broadcasted_iota(i32, (8, TM), 0).astype(f32)
    gl = jnp.where(iota8 < N_GROUPS, logit[0:8], -jnp.inf)
    gmax = jnp.max(gl, axis=0, keepdims=True)
    gidx = jnp.min(jnp.where(gl == gmax, iota8, 8.0), axis=0, keepdims=True)
    ggate = 1.0 / jnp.sum(jnp.exp(gl - gmax), axis=0, keepdims=True)

    e = jnp.zeros((EXPERTS_PER_GROUP, TM), f32)
    for g in range(N_GROUPS):
        e = jnp.where(gidx == g, logit[8 + 8 * g:16 + 8 * g], e)
    m1 = jnp.max(e, axis=0, keepdims=True)
    i1 = jnp.min(jnp.where(e == m1, iota8, 8.0), axis=0, keepdims=True)
    e2 = jnp.where(iota8 == i1, -jnp.inf, e)
    m2 = jnp.max(e2, axis=0, keepdims=True)
    i2 = jnp.min(jnp.where(e2 == m2, iota8, 8.0), axis=0, keepdims=True)
    r = jnp.exp(m2 - m1)
    p1 = 1.0 / (1.0 + r)
    p2 = r / (1.0 + r)
    eid_ref[0:1, :] = (gidx * EXPERTS_PER_GROUP + i1).astype(i32)
    eid_ref[1:2, :] = (gidx * EXPERTS_PER_GROUP + i2).astype(i32)
    gate_ref[0:1, :] = ggate * p1
    gate_ref[1:2, :] = ggate * p2


def _outproj_router(a, b, c, wa, wb, wc, h, g, wr, br):
    T = h.shape[0]
    row = lambda i: (i, 0)
    const = lambda i: (0, 0)
    na, nb, nc = a.shape[1], b.shape[1], c.shape[1]
    return pl.pallas_call(
        _outproj_router_kernel, grid=(T // TM,),
        in_specs=[pl.BlockSpec((TM, na), row), pl.BlockSpec((TM, nb), row), pl.BlockSpec((TM, nc), row),
                  pl.BlockSpec((na, D_MODEL), const, pipeline_mode=pl.Buffered(1)),
                  pl.BlockSpec((nb, D_MODEL), const, pipeline_mode=pl.Buffered(1)),
                  pl.BlockSpec((nc, D_MODEL), const, pipeline_mode=pl.Buffered(1)),
                  pl.BlockSpec((TM, D_MODEL), row),
                  pl.BlockSpec((1, D_MODEL), const),
                  pl.BlockSpec((2 * ROUTER_ROWS, D_MODEL), const),
                  pl.BlockSpec((ROUTER_ROWS, 1), const)],
        out_specs=[pl.BlockSpec((TM, D_MODEL), row), pl.BlockSpec((TM, D_MODEL), row),
                   pl.BlockSpec((TOP_K, TM), lambda i: (0, i)), pl.BlockSpec((TOP_K, TM), lambda i: (0, i))],
        out_shape=[jax.ShapeDtypeStruct((T, D_MODEL), f32), jax.ShapeDtypeStruct((T, D_MODEL), f32),
                   jax.ShapeDtypeStruct((TOP_K, T), i32), jax.ShapeDtypeStruct((TOP_K, T), f32)],
        compiler_params=_cparams(1), name="outproj_router")(a, b, c, wa, wb, wc, h, g, wr, br)


def _moe_kernel(asg_ref, cstart_ref, nvalid_ref, be_ref, nused_ref, x_hbm, wg_ref, wu_ref, wd_ref, y_hbm,
                xbuf, ybuf, gsem, ssem, *, n_tokens):
    del be_ref
    b = pl.program_id(0)
    slot = b % 2
    n_used = nused_ref[0]

    def gather_row(blk, buf, r):
        tok = asg_ref[cstart_ref[blk] + r] >> 1
        return pltpu.make_async_copy(x_hbm.at[pl.ds(tok, 1)], xbuf.at[buf, pl.ds(r, 1)], gsem.at[buf])

    def scatter_row(blk, buf, r):
        a = asg_ref[cstart_ref[blk] + r]
        return pltpu.make_async_copy(ybuf.at[buf, pl.ds(r, 1)], y_hbm.at[pl.ds((a & 1) * n_tokens + (a >> 1), 1)],
                                     ssem.at[buf])

    def start_gather(blk, buf):
        def body(r2, carry):
            gather_row(blk, buf, 2 * r2).start(priority=0)
            gather_row(blk, buf, 2 * r2 + 1).start(priority=1)
            return carry
        lax.fori_loop(0, MOE_BLK // 2, body, 0, unroll=4)

    def start_scatter(blk, buf):
        nv = nvalid_ref[blk]

        def body(r2, carry):
            scatter_row(blk, buf, 2 * r2).start(priority=0)
            scatter_row(blk, buf, 2 * r2 + 1).start(priority=1)
            return carry
        lax.fori_loop(0, nv >> 1, body, 0)

        @pl.when((nv & 1) == 1)
        def _():
            scatter_row(blk, buf, nv - 1).start(priority=0)

    def wait_gather(buf):
        pltpu.make_async_copy(x_hbm.at[pl.ds(0, MOE_BLK)], xbuf.at[buf], gsem.at[buf]).wait()

    def wait_scatter(blk, buf):
        nv = nvalid_ref[blk]
        rows = MOE_BLK
        while rows >= 1:
            @pl.when((nv & rows) != 0)
            def _(rows=rows):
                pltpu.make_async_copy(ybuf.at[buf, pl.ds(0, rows)], y_hbm.at[pl.ds(0, rows)], ssem.at[buf]).wait()
            rows //= 2

    @pl.when(b == 0)
    def _():
        start_gather(0, 0)

    @pl.when(b + 1 < n_used)
    def _():
        start_gather(b + 1, 1 - slot)

    @pl.when(b < n_used)
    def _():
        wait_gather(slot)

        @pl.when(b >= 2)
        def _():
            wait_scatter(b - 2, slot)

        x = xbuf[slot]
        hg = jnp.dot(x, wg_ref[...], preferred_element_type=f32)
        hu = jnp.dot(x, wu_ref[...], preferred_element_type=f32)
        hid = hg * jax.nn.sigmoid(hg) * hu
        ybuf[slot] = jnp.dot(hid, wd_ref[...], preferred_element_type=f32)
        start_scatter(b, slot)

    @pl.when(b == n_used - 1)
    def _():
        wait_scatter(b, slot)

        @pl.when(b >= 1)
        def _():
            wait_scatter(b - 1, 1 - slot)


def _moe(xn, tables, w_gate, w_up, w_down, layer):
    T = xn.shape[0]
    n_blocks = tables[3].shape[0]
    wspec = lambda shape: pl.BlockSpec((None, None) + shape, lambda b, a, c, n, e, u: (layer, e[b], 0, 0))
    grid_spec = pltpu.PrefetchScalarGridSpec(
        num_scalar_prefetch=5, grid=(n_blocks,),
        in_specs=[pl.BlockSpec(memory_space=pl.ANY),
                  wspec((D_MODEL, EXPERT_HIDDEN)), wspec((D_MODEL, EXPERT_HIDDEN)), wspec((EXPERT_HIDDEN, D_MODEL))],
        out_specs=pl.BlockSpec(memory_space=pl.ANY),
        scratch_shapes=[pltpu.VMEM((2, MOE_BLK, D_MODEL), f32), pltpu.VMEM((2, MOE_BLK, D_MODEL), f32),
                        pltpu.SemaphoreType.DMA((2,)), pltpu.SemaphoreType.DMA((2,))])
    return pl.pallas_call(
        functools.partial(_moe_kernel, n_tokens=T), grid_spec=grid_spec,
        out_shape=jax.ShapeDtypeStruct((TOP_K * T, D_MODEL), f32),
        compiler_params=_cparams(1), name="moe")(*tables, xn, w_gate, w_up, w_down)


def _dispatch(eid, T):
    n_assign = TOP_K * T
    n_blocks = n_assign // MOE_BLK + N_EXPERTS
    e_flat = eid.T.reshape(-1)
    keys = jnp.sort(e_flat * n_assign + jnp.arange(n_assign, dtype=i32))
    asg = jnp.concatenate([keys % n_assign, jnp.zeros((MOE_BLK,), i32)])
    bounds = jnp.searchsorted(keys, jnp.arange(N_EXPERTS + 1, dtype=i32) * n_assign).astype(i32)
    start, counts = bounds[:-1], bounds[1:] - bounds[:-1]
    nblk = (counts + MOE_BLK - 1) // MOE_BLK
    bend = jnp.cumsum(nblk)
    n_used = bend[-1]
    b = jnp.arange(n_blocks, dtype=i32)
    be = jnp.minimum(jnp.searchsorted(bend, jnp.minimum(b, n_used - 1), side='right'), N_EXPERTS - 1).astype(i32)
    j = b - (bend - nblk)[be]
    used = b < n_used
    cstart = jnp.where(used, start[be] + j * MOE_BLK, 0).astype(i32)
    nvalid = jnp.where(used, jnp.clip(counts[be] - j * MOE_BLK, 0, MOE_BLK), 0).astype(i32)
    return asg, cstart, nvalid, be, n_used.reshape(1).astype(i32)


def _final_kernel(h_ref, y0_ref, y1_ref, gate_ref, g_ref, o_ref):
    o_ref[...] = _rms(_combine(h_ref, y0_ref, y1_ref, gate_ref), g_ref[...])


def _final(h, moe_out, g):
    T = h.shape[0]
    y, gate = moe_out
    return pl.pallas_call(
        _final_kernel, grid=(T // TM,),
        in_specs=_combine_specs(T) + [pl.BlockSpec((1, D_MODEL), lambda i: (0, 0))],
        out_specs=pl.BlockSpec((TM, D_MODEL), lambda i: (i, 0)),
        out_shape=jax.ShapeDtypeStruct((T, D_MODEL), f32),
        compiler_params=_cparams(1), name="final_norm")(h, y, y, gate, g)


def _t5_bucket(n):
    nf = jnp.maximum(n, 1).astype(f32)
    large = REL_MAX_EXACT + (jnp.log(nf / REL_MAX_EXACT) / math.log(REL_MAX_DISTANCE / REL_MAX_EXACT)
                             * (REL_BUCKETS - REL_MAX_EXACT)).astype(i32)
    return jnp.where(n < REL_MAX_EXACT, n, jnp.minimum(large, REL_BUCKETS - 1))


def _pad_cols(w, width):
    return jnp.pad(w, ((0, 0), (0, width - w.shape[1])))


def _split_hi_lo(w):
    hi = w.astype(bf16)
    return hi, (w - hi.astype(f32)).astype(bf16)


def kernel(x, positions, attn_norm, w_in, pool_w, pool_scale, mla_q_norm, mla_w_uq, mla_kv_norm, mla_w_ukv,
           diff_lambda, diff_subln, rel_bias, w_out, ffn_norm, router_group_w, router_group_b,
           router_expert_w, router_expert_b, expert_w_gate, expert_w_up, expert_w_down, final_norm):
    B, S, D = x.shape
    T = B * S
    depth = w_in.shape[0]
    assert D == D_MODEL and S % TQ == 0 and T % TM == 0 and (TOP_K * T) % MOE_BLK == 0

    inv = 1.0 / (ROPE_THETA ** (jnp.arange(0, MLA_ROPE, 2, dtype=f32) / MLA_ROPE))
    ang = positions.astype(f32).reshape(T, 1) * inv
    zeros = jnp.zeros((T, LANE - MLA_ROPE), f32)
    cos = jnp.concatenate([jnp.cos(ang), jnp.cos(ang), zeros], axis=1)
    sin = jnp.concatenate([jnp.sin(ang), jnp.sin(ang), zeros], axis=1)
    posq = positions.reshape(T, 1)
    posk = positions.reshape(B, S // TK, 1, TK)
    tab = jnp.pad(rel_bias[_t5_bucket(jnp.arange(LANE, dtype=i32))].T.astype(f32), ((0, 8 - DIFF_HEADS), (0, 0)))

    h = x.reshape(T, D)
    moe_out = None
    for l in range(depth):
        wl = w_in[l]
        w_in_p = jnp.concatenate([wl[:, :COL_KR + MLA_ROPE], jnp.zeros((D, LANE - MLA_ROPE), f32),
                                  wl[:, COL_KR + MLA_ROPE:]], axis=1).astype(bf16)
        wq = mla_w_uq[l].reshape(MLA_Q_RANK, MLA_HEADS, MLA_NOPE + MLA_ROPE)
        wq = jnp.pad(wq, ((0, 0), (0, 0), (0, QK_PAD - MLA_NOPE - MLA_ROPE))).reshape(MLA_Q_RANK, -1).astype(bf16)
        wkv = mla_w_ukv[l].reshape(MLA_KV_RANK, MLA_HEADS, MLA_NOPE + MLA_V)
        wk = wkv[:, :, :MLA_NOPE].reshape(MLA_KV_RANK, -1).astype(bf16)
        wv = wkv[:, :, MLA_NOPE:].reshape(MLA_KV_RANK, -1).astype(bf16)
        wo = w_out[l].astype(bf16)
        wa, wb, wc = wo[:POOL_WIDTH], wo[POOL_WIDTH:POOL_WIDTH + MLA_HEADS * MLA_V], wo[POOL_WIDTH + MLA_HEADS * MLA_V:]
        wr = jnp.zeros((D, ROUTER_ROWS), f32).at[:, :N_GROUPS].set(router_group_w[l]).at[:, 8:8 + N_EXPERTS].set(router_expert_w[l])
        wr_hi, wr_lo = _split_hi_lo(wr)
        wr_t = jnp.concatenate([wr_hi, wr_lo], axis=1).T
        br = jnp.zeros((ROUTER_ROWS,), f32).at[:N_GROUPS].set(router_group_b[l]).at[8:8 + N_EXPERTS].set(router_expert_b[l])
        lam_init = 0.8 - 0.6 * math.exp(-0.3 * l)

        proj, h = _inproj(h, moe_out, attn_norm[l].reshape(1, D), w_in_p)
        a_out = _pool(proj, pool_w[l].astype(bf16), pool_scale[l].reshape(1, POOL_WIDTH), B, S)
        q, k, v = _mla_prep(proj, cos, sin, mla_q_norm[l].reshape(1, -1), mla_kv_norm[l].reshape(1, -1), wq, wk, wv)
        b_out = _mla_attn(q, k, v, B, S)
        c_out = _diff_attn(proj, diff_lambda[l], posq, posk, tab, diff_subln[l].reshape(1, -1), lam_init, B, S)
        h, xn, eid, gate = _outproj_router(a_out, b_out, c_out, wa, wb, wc, h, ffn_norm[l].reshape(1, D),
                                           wr_t, br.reshape(ROUTER_ROWS, 1))
        y = _moe(xn, _dispatch(eid, T), expert_w_gate, expert_w_up, expert_w_down, l)
        moe_out = (y, gate.T)
    out = _final(h, moe_out, final_norm.reshape(1, D))
    return out.reshape(B, S, D)
```

```python
import functools
import math

import jax
import jax.numpy as jnp
from jax import lax
from jax.experimental import pallas as pl
from jax.experimental.pallas import tpu as pltpu

f32, bf16, i32 = jnp.float32, jnp.bfloat16, jnp.int32

D_MODEL = 2048
POOL_WINDOWS = (2, 4, 8, 16)
POOL_GROUPS = 4
POOL_WIDTH = 512
MLA_NOPE, MLA_ROPE, MLA_V, MLA_HEADS = 128, 64, 128, 6
MLA_Q_RANK, MLA_KV_RANK = 512, 256
ROPE_THETA = 10000.0
DIFF_QK, DIFF_V, DIFF_HEADS = 64, 128, 6
REL_BUCKETS, REL_MAX_EXACT, REL_MAX_DISTANCE = 32, 16, 128
N_GROUPS, EXPERTS_PER_GROUP, N_EXPERTS, TOP_K, EXPERT_HIDDEN = 4, 8, 32, 2, 512
NORM_EPS = 1e-6

LANE = 128
MXU_DIM = 256
VMEM_LIMIT = 56 * 1024 * 1024

N_IN_PAD = 3712
COL_CQ, COL_CKV, COL_KR, COL_DQ, COL_DK, COL_DV = 512, 1024, 1280, 1408, 2176, 2944
QK_PAD = 2 * LANE

TM = 256
TQ = 512
TK = 256
MOE_BLK = 256
REL_CLIP = REL_MAX_DISTANCE - 1
REL_FAR = REL_MAX_DISTANCE


def _cparams(n_axes):
    return pltpu.CompilerParams(dimension_semantics=("arbitrary",) * n_axes, vmem_limit_bytes=VMEM_LIMIT)


def _rms(x, g):
    return x * lax.rsqrt(jnp.mean(x * x, axis=-1, keepdims=True) + NORM_EPS) * g


def _combine(h_ref, y0_ref, y1_ref, gate_ref):
    gate = gate_ref[...]
    return h_ref[...] + y0_ref[...] * gate[:, 0:1] + y1_ref[...] * gate[:, 1:2]


def _combine_specs(T):
    plane1 = T // TM
    return [pl.BlockSpec((TM, D_MODEL), lambda i: (i, 0)),
            pl.BlockSpec((TM, D_MODEL), lambda i: (i, 0)),
            pl.BlockSpec((TM, D_MODEL), lambda i: (plane1 + i, 0)),
            pl.BlockSpec((TM, TOP_K), lambda i: (i, 0))]


def _inproj_kernel(*refs, combine):
    if combine:
        h_ref, y0_ref, y1_ref, gate_ref, g_ref, w_ref, proj_ref, hout_ref = refs
        h = _combine(h_ref, y0_ref, y1_ref, gate_ref)
        hout_ref[...] = h
    else:
        h_ref, g_ref, w_ref, proj_ref = refs
        h = h_ref[...]
    u = _rms(h, g_ref[...]).astype(bf16)
    for n0 in range(0, N_IN_PAD, 512):
        n1 = min(n0 + 512, N_IN_PAD)
        proj_ref[:, n0:n1] = jnp.dot(u, w_ref[:, n0:n1], preferred_element_type=f32).astype(bf16)


def _inproj(h, moe_out, g, w):
    T = h.shape[0]
    combine = moe_out is not None
    row = lambda i: (i, 0)
    const = lambda i: (0, 0)
    if combine:
        y, gate = moe_out
        in_specs = _combine_specs(T)
        args = [h, y, y, gate]
    else:
        in_specs = [pl.BlockSpec((TM, D_MODEL), row)]
        args = [h]
    in_specs += [pl.BlockSpec((1, D_MODEL), const),
                 pl.BlockSpec((D_MODEL, N_IN_PAD), const, pipeline_mode=pl.Buffered(1))]
    args += [g, w]
    out_shape = [jax.ShapeDtypeStruct((T, N_IN_PAD), bf16)]
    out_specs = [pl.BlockSpec((TM, N_IN_PAD), row)]
    if combine:
        out_shape.append(jax.ShapeDtypeStruct((T, D_MODEL), f32))
        out_specs.append(pl.BlockSpec((TM, D_MODEL), row))
    res = pl.pallas_call(
        functools.partial(_inproj_kernel, combine=combine),
        grid=(T // TM,), in_specs=in_specs, out_specs=out_specs, out_shape=out_shape,
        compiler_params=_cparams(1), name="inproj_c" if combine else "inproj")(*args)
    return (res[0], res[1]) if combine else (res[0], h)


def _pool_kernel(x_ref, w_ref, sc_ref, o_ref):
    g = pl.program_id(1)
    x = x_ref[...].astype(f32)
    t = lax.broadcasted_iota(i32, x.shape, 0)

    def lagged(v, k):
        return jnp.where(t >= k, pltpu.roll(v, k, axis=0), 0.0)

    s2 = x + lagged(x, 1)
    s4 = s2 + lagged(s2, 2)
    s8 = s4 + lagged(s4, 4)
    s16 = s8 + lagged(s8, 8)
    wsum = jnp.where(g == 0, s2, jnp.where(g == 1, s4, jnp.where(g == 2, s8, s16)))
    win = jnp.where(g == 0, 2, jnp.where(g == 1, 4, jnp.where(g == 2, 8, 16)))
    cnt = jnp.minimum(t + 1, win).astype(f32)
    d = (wsum / cnt - x).astype(bf16)
    y = jnp.dot(d, w_ref[...], preferred_element_type=f32) * sc_ref[...]
    o_ref[...] = y.astype(bf16)


def _pool(proj, w, scale, B, S):
    T = B * S
    return pl.pallas_call(
        _pool_kernel, grid=(B, POOL_GROUPS),
        in_specs=[pl.BlockSpec((S, LANE), lambda b, g: (b, g)),
                  pl.BlockSpec((None, LANE, LANE), lambda b, g: (g, 0, 0)),
                  pl.BlockSpec((1, LANE), lambda b, g: (0, g))],
        out_specs=pl.BlockSpec((S, LANE), lambda b, g: (b, g)),
        out_shape=jax.ShapeDtypeStruct((T, POOL_WIDTH), bf16),
        compiler_params=_cparams(2), name="pool")(proj, w, scale)


VT_ROWS = MLA_V + 16


def _nt_dot(a, b):
    return lax.dot_general(a, b, (((1,), (1,)), ((), ())), preferred_element_type=f32)


def _rope128(r, cos, sin):
    lane = lax.broadcasted_iota(i32, r.shape, 1)
    half = MLA_ROPE // 2
    partner = jnp.where(lane < half, -pltpu.roll(r, LANE - half, axis=1), pltpu.roll(r, half, axis=1))
    return r * cos + partner * sin


def _ones_row_pad(n_cols):
    row = lax.broadcasted_iota(i32, (VT_ROWS - MLA_V, n_cols), 0)
    return jnp.where(row == 0, 1.0, 0.0).astype(bf16)


def _mla_prep_kernel(cq_ref, ckv_ref, kr_ref, cos_ref, sin_ref, cost_ref, sint_ref, qn_ref, kvn_ref,
                     wqt_ref, wk_ref, wvt_ref, qt_ref, k_ref, vt_ref):
    scale = (MLA_NOPE + MLA_ROPE) ** -0.5
    half = MLA_ROPE // 2
    cost, sint = cost_ref[...], sint_ref[...]
    cq = _rms(cq_ref[...].astype(f32), qn_ref[...]).astype(bf16)
    ckv = _rms(ckv_ref[...].astype(f32), kvn_ref[...]).astype(bf16)
    kr = _rope128(kr_ref[...].astype(f32), cos_ref[...], sin_ref[...]).astype(bf16)
    pad = _ones_row_pad(TM)
    for h in range(MLA_HEADS):
        r0 = h * QK_PAD
        qt = _nt_dot(wqt_ref[r0:r0 + QK_PAD, :], cq) * scale
        x1, x2 = qt[MLA_NOPE:MLA_NOPE + half], qt[MLA_NOPE + half:MLA_NOPE + MLA_ROPE]
        qt_ref[r0:r0 + MLA_NOPE, :] = qt[:MLA_NOPE].astype(bf16)
        qt_ref[r0 + MLA_NOPE:r0 + MLA_NOPE + half, :] = (x1 * cost - x2 * sint).astype(bf16)
        qt_ref[r0 + MLA_NOPE + half:r0 + MLA_NOPE + MLA_ROPE, :] = (x1 * sint + x2 * cost).astype(bf16)
        qt_ref[r0 + MLA_NOPE + MLA_ROPE:r0 + QK_PAD, :] = qt[MLA_NOPE + MLA_ROPE:].astype(bf16)
        kn = jnp.dot(ckv, wk_ref[:, h * LANE:(h + 1) * LANE], preferred_element_type=f32)
        k_ref[:, r0:r0 + LANE] = kn.astype(bf16)
        k_ref[:, r0 + LANE:r0 + QK_PAD] = kr
        vt = _nt_dot(wvt_ref[h * MLA_V:(h + 1) * MLA_V, :], ckv)
        vt_ref[h, 0:MLA_V, :] = vt.astype(bf16)
        vt_ref[h, MLA_V:VT_ROWS, :] = pad


def _mla_prep(proj, cos, sin, cost, sint, qn, kvn, wqt, wk, wvt, B, S):
    T = B * S
    assert TM == TK
    nk = S // TK
    const = lambda i: (0, 0)
    nq, nkc = MLA_HEADS * QK_PAD, MLA_HEADS * LANE
    return pl.pallas_call(
        _mla_prep_kernel, grid=(T // TM,),
        in_specs=[pl.BlockSpec((TM, MLA_Q_RANK), lambda i: (i, COL_CQ // MLA_Q_RANK)),
                  pl.BlockSpec((TM, MLA_KV_RANK), lambda i: (i, COL_CKV // MLA_KV_RANK)),
                  pl.BlockSpec((TM, LANE), lambda i: (i, COL_KR // LANE)),
                  pl.BlockSpec((TM, LANE), lambda i: (i, 0)),
                  pl.BlockSpec((TM, LANE), lambda i: (i, 0)),
                  pl.BlockSpec((MLA_ROPE // 2, TM), lambda i: (0, i)),
                  pl.BlockSpec((MLA_ROPE // 2, TM), lambda i: (0, i)),
                  pl.BlockSpec((1, MLA_Q_RANK), const),
                  pl.BlockSpec((1, MLA_KV_RANK), const),
                  pl.BlockSpec((nq, MLA_Q_RANK), const),
                  pl.BlockSpec((MLA_KV_RANK, nkc), const),
                  pl.BlockSpec((MLA_HEADS * MLA_V, MLA_KV_RANK), const)],
        out_specs=[pl.BlockSpec((nq, TM), lambda i: (0, i)),
                   pl.BlockSpec((TM, nq), lambda i: (i, 0)),
                   pl.BlockSpec((None, MLA_HEADS, None, VT_ROWS, TK), lambda i: (i // nk, 0, i % nk, 0, 0))],
        out_shape=[jax.ShapeDtypeStruct((nq, T), bf16), jax.ShapeDtypeStruct((T, nq), bf16),
                   jax.ShapeDtypeStruct((B, MLA_HEADS, nk, VT_ROWS, TK), bf16)],
        compiler_params=_cparams(1), name="mla_prep")(proj, proj, proj, cos, sin, cost, sint, qn, kvn, wqt, wk, wvt)


def _softmax_step(s, m, shift, acc_ref, idx, vt, lo=0):
    m_old = m[:, lo:]
    m_new = jnp.maximum(m_old, jnp.max(s, axis=0, keepdims=True) + shift)
    alpha = jnp.exp(m_old - m_new)
    p = jnp.exp(s - (m_new - shift)).astype(bf16)
    acc_ref[idx, :, lo:] = alpha * acc_ref[idx, :, lo:] + jnp.dot(vt, p, preferred_element_type=f32)
    return m_new if lo == 0 else jnp.concatenate([m[:, :lo], m_new], axis=1)


def _attn_out(acc):
    return (acc[0:MLA_V] / acc[MLA_V:MLA_V + 1]).T


def _diag_visible(lo):
    krow = lax.broadcasted_iota(i32, (TK, TQ - lo), 0)
    qcol = lax.broadcasted_iota(i32, (TK, TQ - lo), 1)
    return krow <= qcol


MLA_HEADS_PER_STEP = 2


def _mla_attn_kernel(qt_ref, k_ref, vt_ref, o_ref, acc_ref):
    i = pl.program_id(2)
    acc_ref[...] = jnp.zeros_like(acc_ref)
    n_full = i * (TQ // TK)
    heads = range(MLA_HEADS_PER_STEP)

    def scores(c, j, lo=0):
        k = k_ref[pl.ds(pl.multiple_of(j * TK, TK), TK), c * QK_PAD:(c + 1) * QK_PAD]
        return jnp.dot(k, qt_ref[c * QK_PAD:(c + 1) * QK_PAD, lo:], preferred_element_type=f32)

    def full(j, ms):
        return tuple(_softmax_step(scores(c, j), ms[c], 0.0, acc_ref, c, vt_ref[c, j]) for c in heads)

    minf = jnp.full((1, TQ), -jnp.inf, f32)
    ms = lax.fori_loop(0, n_full, full, (minf,) * MLA_HEADS_PER_STEP)
    for d in range(TQ // TK):
        j, lo = n_full + d, d * TK
        ms = tuple(_softmax_step(jnp.where(_diag_visible(lo), scores(c, j, lo), -jnp.inf), ms[c], 0.0, acc_ref, c,
                                 vt_ref[c, j], lo) for c in heads)
    for c in heads:
        o_ref[:, c * MLA_V:(c + 1) * MLA_V] = _attn_out(acc_ref[c]).astype(bf16)


def _mla_attn(qt, k, vt, B, S):
    T = B * S
    nq, nk = S // TQ, S // TK
    hs = MLA_HEADS_PER_STEP
    return pl.pallas_call(
        _mla_attn_kernel, grid=(B, MLA_HEADS // hs, nq),
        in_specs=[pl.BlockSpec((hs * QK_PAD, TQ), lambda b, h, i: (h, b * nq + i)),
                  pl.BlockSpec((S, hs * QK_PAD), lambda b, h, i: (b, h)),
                  pl.BlockSpec((None, hs, nk, VT_ROWS, TK), lambda b, h, i: (b, h, 0, 0, 0))],
        out_specs=pl.BlockSpec((TQ, hs * MLA_V), lambda b, h, i: (b * nq + i, h)),
        out_shape=jax.ShapeDtypeStruct((T, MLA_HEADS * MLA_V), bf16),
        scratch_shapes=[pltpu.VMEM((hs, VT_ROWS, TQ), f32)],
        compiler_params=_cparams(3), name="mla_attn")(qt, k, vt)


def _diff_attn_kernel(pmin_ref, kmax_ref, lam_ref, q_ref, k_ref, v_ref, posq_ref, posk_ref, tab_ref, sub_ref, o_ref,
                      vt_ref, acc_ref, *, lam_init, nq, nk):
    b, h, i = pl.program_id(0), pl.program_id(1), pl.program_id(2)

    @pl.when(i == 0)
    def _():
        pad = _ones_row_pad(TK)
        for j in range(nk):
            vt_ref[j, 0:DIFF_V, :] = v_ref[j * TK:(j + 1) * TK, :].astype(f32).T.astype(bf16)
            vt_ref[j, DIFF_V:VT_ROWS, :] = pad

    lv = lam_ref[...]
    lam = (jnp.exp(jnp.sum(lv[0:1] * lv[1:2], axis=1, keepdims=True))
           - jnp.exp(jnp.sum(lv[2:3] * lv[3:4], axis=1, keepdims=True)) + lam_init)
    qt = (q_ref[...].astype(f32) * (DIFF_QK ** -0.5)).T
    frow = lax.broadcasted_iota(i32, qt.shape, 0)
    qm = (jnp.where(frow < DIFF_QK, qt, 0.0).astype(bf16), jnp.where(frow >= DIFF_QK, qt, 0.0).astype(bf16))
    acc_ref[...] = jnp.zeros_like(acc_ref)
    posq = posq_ref[...]
    tab = jnp.broadcast_to(tab_ref[pl.ds(h, 1), :], (TK, LANE))
    far_bias = tab_ref[pl.ds(h, 1), REL_CLIP:REL_CLIP + 1]
    n_full = i * (TQ // TK)

    def scores(j, lo=0):
        off = pl.multiple_of(j * TK, TK)
        q_both = jnp.concatenate([qm[0][:, lo:], qm[1][:, lo:]], axis=1)
        return jnp.dot(k_ref[pl.ds(off, TK), :], q_both, preferred_element_type=f32), off

    def bias_tile(off, lo):
        n = jnp.clip(posq[:, lo:] - posk_ref[pl.ds(off, TK), :], 0, REL_CLIP)
        return jnp.concatenate(
            [jnp.take_along_axis(tab, n[:, c * LANE:(c + 1) * LANE], axis=1) for c in range((TQ - lo) // LANE)],
            axis=1)

    def far(j, ms):
        s2, _ = scores(j)
        return (_softmax_step(s2[:, :TQ], ms[0], far_bias, acc_ref, 0, vt_ref[j]),
                _softmax_step(s2[:, TQ:], ms[1], far_bias, acc_ref, 1, vt_ref[j]))

    def near(j, ms, lo=0, visible=None):
        s2, off = scores(j, lo)
        bias = bias_tile(off, lo)
        new = []
        for c in range(2):
            s = s2[:, c * (TQ - lo):(c + 1) * (TQ - lo)] + bias
            if visible is not None:
                s = jnp.where(visible, s, -jnp.inf)
            new.append(_softmax_step(s, ms[c], 0.0, acc_ref, c, vt_ref[j], lo))
        return tuple(new)

    def below_diagonal(j, ms):
        is_far = pmin_ref[b * nq + i] - kmax_ref[b * nk + j] >= REL_FAR
        return lax.cond(is_far, far, near, j, ms)

    minf = jnp.full((1, TQ), -jnp.inf, f32)
    ms = lax.fori_loop(0, n_full, below_diagonal, (minf, minf))
    for d in range(TQ // TK):
        ms = near(n_full + d, ms, d * TK, _diag_visible(d * TK))
    o = _attn_out(acc_ref[0]) - lam * _attn_out(acc_ref[1])
    o_ref[...] = (_rms(o, sub_ref[...]) * (1.0 - lam_init)).astype(bf16)


def _diff_attn(proj, lam_vecs, positions, tab, subln, lam_init, B, S):
    T = B * S
    nq, nk = S // TQ, S // TK
    cq, ck, cv = COL_DQ // LANE, COL_DK // LANE, COL_DV // LANE
    pmin = positions.reshape(B * nq, TQ).min(axis=1)
    kmax = positions.reshape(B * nk, TK).max(axis=1)
    grid_spec = pltpu.PrefetchScalarGridSpec(
        num_scalar_prefetch=2, grid=(B, DIFF_HEADS, nq),
        in_specs=[pl.BlockSpec((4, DIFF_QK), lambda b, h, i, *_: (0, 0)),
                  pl.BlockSpec((TQ, LANE), lambda b, h, i, *_: (b * nq + i, cq + h)),
                  pl.BlockSpec((S, LANE), lambda b, h, i, *_: (b, ck + h)),
                  pl.BlockSpec((S, LANE), lambda b, h, i, *_: (b, cv + h)),
                  pl.BlockSpec((1, TQ), lambda b, h, i, *_: (0, b * nq + i)),
                  pl.BlockSpec((S, 1), lambda b, h, i, *_: (b, 0)),
                  pl.BlockSpec((8, LANE), lambda b, h, i, *_: (0, 0)),
                  pl.BlockSpec((1, DIFF_V), lambda b, h, i, *_: (0, 0))],
        out_specs=pl.BlockSpec((TQ, DIFF_V), lambda b, h, i, *_: (b * nq + i, h)),
        scratch_shapes=[pltpu.VMEM((nk, VT_ROWS, TK), bf16), pltpu.VMEM((2, VT_ROWS, TQ), f32)])
    return pl.pallas_call(
        functools.partial(_diff_attn_kernel, lam_init=lam_init, nq=nq, nk=nk), grid_spec=grid_spec,
        out_shape=jax.ShapeDtypeStruct((T, DIFF_HEADS * DIFF_V), bf16),
        compiler_params=_cparams(3), name="diff_attn")(
            pmin, kmax, lam_vecs, proj, proj, proj, positions.reshape(1, T), positions.reshape(T, 1), tab, subln)


ROUTER_ROWS = 64


def _outproj_router_kernel(a_ref, b_ref, c_ref, wa_ref, wb_ref, wc_ref, h_ref, g_ref, wr_ref, br_ref,
                           h2_ref, xn_ref, eid_ref, gate_ref):
    mixed = (jnp.dot(a_ref[...], wa_ref[...], preferred_element_type=f32)
             + jnp.dot(b_ref[...], wb_ref[...], preferred_element_type=f32)
             + jnp.dot(c_ref[...], wc_ref[...], preferred_element_type=f32))
    h2 = h_ref[...] + mixed
    h2_ref[...] = h2
    xn = _rms(h2, g_ref[...])
    xn_ref[...] = xn
    xh = xn.astype(bf16)
    xl = (xn - xh.astype(f32)).astype(bf16)
    wr = wr_ref[...]
    ah, al = _nt_dot(wr, xh), _nt_dot(wr, xl)
    logit = ah[:ROUTER_ROWS] + ah[ROUTER_ROWS:] + al[:ROUTER_ROWS] + br_ref[...]

    iota8 = lax.broadcasted_iota(i32, (8, TM), 0).astype(f32)
    gl = jnp.where(iota8 < N_GROUPS, logit[0:8], -jnp.inf)
    gmax = jnp.max(gl, axis=0, keepdims=True)
    gidx = jnp.min(jnp.where(gl == gmax, iota8, 8.0), axis=0, keepdims=True)
    ggate = 1.0 / jnp.sum(jnp.exp(gl - gmax), axis=0, keepdims=True)

    e = jnp.zeros((EXPERTS_PER_GROUP, TM), f32)
    for g in range(N_GROUPS):
        e = jnp.where(gidx == g, logit[8 + 8 * g:16 + 8 * g], e)
    m1 = jnp.max(e, axis=0, keepdims=True)
    i1 = jnp.min(jnp.where(e == m1, iota8, 8.0), axis=0, keepdims=True)
    e2 = jnp.where(iota8 == i1, -jnp.inf, e)
    m2 = jnp.max(e2, axis=0, keepdims=True)
    i2 = jnp.min(jnp.where(e2 == m2, iota8, 8.0), axis=0, keepdims=True)
    r = jnp.exp(m2 - m1)
    p1 = 1.0 / (1.0 + r)
    p2 = r / (1.0 + r)
    eid_ref[0:1, :] = (gidx * EXPERTS_PER_GROUP + i1).astype(i32)
    eid_ref[1:2, :] = (gidx * EXPERTS_PER_GROUP + i2).astype(i32)
    gate_ref[0:1, :] = ggate * p1
    gate_ref[1:2, :] = ggate * p2


def _outproj_router(a, b, c, wa, wb, wc, h, g, wr, br):
    T = h.shape[0]
    row = lambda i: (i, 0)
    const = lambda i: (0, 0)
    na, nb, nc = a.shape[1], b.shape[1], c.shape[1]
    return pl.pallas_call(
        _outproj_router_kernel, grid=(T // TM,),
        in_specs=[pl.BlockSpec((TM, na), row), pl.BlockSpec((TM, nb), row), pl.BlockSpec((TM, nc), row),
                  pl.BlockSpec((na, D_MODEL), const, pipeline_mode=pl.Buffered(1)),
                  pl.BlockSpec((nb, D_MODEL), const, pipeline_mode=pl.Buffered(1)),
                  pl.BlockSpec((nc, D_MODEL), const, pipeline_mode=pl.Buffered(1)),
                  pl.BlockSpec((TM, D_MODEL), row),
                  pl.BlockSpec((1, D_MODEL), const),
                  pl.BlockSpec((2 * ROUTER_ROWS, D_MODEL), const),
                  pl.BlockSpec((ROUTER_ROWS, 1), const)],
        out_specs=[pl.BlockSpec((TM, D_MODEL), row), pl.BlockSpec((TM, D_MODEL), row),
                   pl.BlockSpec((TOP_K, TM), lambda i: (0, i)), pl.BlockSpec((TOP_K, TM), lambda i: (0, i))],
        out_shape=[jax.ShapeDtypeStruct((T, D_MODEL), f32), jax.ShapeDtypeStruct((T, D_MODEL), f32),
                   jax.ShapeDtypeStruct((TOP_K, T), i32), jax.ShapeDtypeStruct((TOP_K, T), f32)],
        compiler_params=_cparams(1), name="outproj_router")(a, b, c, wa, wb, wc, h, g, wr, br)


def _moe_kernel(asg_ref, cstart_ref, nvalid_ref, be_ref, nused_ref, x_hbm, wg_ref, wu_ref, wd_ref, y_hbm,
                xbuf, ybuf, gsem, ssem, *, n_tokens):
    del be_ref
    b = pl.program_id(0)
    slot = b % 2
    n_used = nused_ref[0]

    def gather_row(blk, buf, r):
        tok = asg_ref[cstart_ref[blk] + r] >> 1
        return pltpu.make_async_copy(x_hbm.at[pl.ds(tok, 1)], xbuf.at[buf, pl.ds(r, 1)], gsem.at[buf])

    def scatter_row(blk, buf, r):
        a = asg_ref[cstart_ref[blk] + r]
        return pltpu.make_async_copy(ybuf.at[buf, pl.ds(r, 1)], y_hbm.at[pl.ds((a & 1) * n_tokens + (a >> 1), 1)],
                                     ssem.at[buf])

    def start_gather(blk, buf):
        def body(r2, carry):
            gather_row(blk, buf, 2 * r2).start(priority=0)
            gather_row(blk, buf, 2 * r2 + 1).start(priority=1)
            return carry
        lax.fori_loop(0, MOE_BLK // 2, body, 0, unroll=4)

    def start_scatter(blk, buf):
        nv = nvalid_ref[blk]

        def body(r2, carry):
            scatter_row(blk, buf, 2 * r2).start(priority=0)
            scatter_row(blk, buf, 2 * r2 + 1).start(priority=1)
            return carry
        lax.fori_loop(0, nv >> 1, body, 0)

        @pl.when((nv & 1) == 1)
        def _():
            scatter_row(blk, buf, nv - 1).start(priority=0)

    def wait_gather(buf):
        pltpu.make_async_copy(x_hbm.at[pl.ds(0, MOE_BLK)], xbuf.at[buf], gsem.at[buf]).wait()

    def wait_scatter(blk, buf):
        nv = nvalid_ref[blk]
        rows = MOE_BLK
        while rows >= 1:
            @pl.when((nv & rows) != 0)
            def _(rows=rows):
                pltpu.make_async_copy(ybuf.at[buf, pl.ds(0, rows)], y_hbm.at[pl.ds(0, rows)], ssem.at[buf]).wait()
            rows //= 2

    @pl.when(b == 0)
    def _():
        start_gather(0, 0)

    @pl.when(b + 1 < n_used)
    def _():
        start_gather(b + 1, 1 - slot)

    @pl.when(b < n_used)
    def _():
        wait_gather(slot)

        @pl.when(b >= 2)
        def _():
            wait_scatter(b - 2, slot)

        x = xbuf[slot]
        hg = jnp.dot(x, wg_ref[...], preferred_element_type=f32)
        hu = jnp.dot(x, wu_ref[...], preferred_element_type=f32)
        hid = hg * jax.nn.sigmoid(hg) * hu
        ybuf[slot] = jnp.dot(hid, wd_ref[...], preferred_element_type=f32)
        start_scatter(b, slot)

    @pl.when(b == n_used - 1)
    def _():
        wait_scatter(b, slot)

        @pl.when(b >= 1)
        def _():
            wait_scatter(b - 1, 1 - slot)


def _moe(xn, tables, w_gate, w_up, w_down, layer):
    T = xn.shape[0]
    n_blocks = tables[3].shape[0]
    wspec = lambda shape: pl.BlockSpec((None, None) + shape, lambda b, a, c, n, e, u: (layer, e[b], 0, 0))
    grid_spec = pltpu.PrefetchScalarGridSpec(
        num_scalar_prefetch=5, grid=(n_blocks,),
        in_specs=[pl.BlockSpec(memory_space=pl.ANY),
                  wspec((D_MODEL, EXPERT_HIDDEN)), wspec((D_MODEL, EXPERT_HIDDEN)), wspec((EXPERT_HIDDEN, D_MODEL))],
        out_specs=pl.BlockSpec(memory_space=pl.ANY),
        scratch_shapes=[pltpu.VMEM((2, MOE_BLK, D_MODEL), f32), pltpu.VMEM((2, MOE_BLK, D_MODEL), f32),
                        pltpu.SemaphoreType.DMA((2,)), pltpu.SemaphoreType.DMA((2,))])
    return pl.pallas_call(
        functools.partial(_moe_kernel, n_tokens=T), grid_spec=grid_spec,
        out_shape=jax.ShapeDtypeStruct((TOP_K * T, D_MODEL), f32),
        compiler_params=_cparams(1), name="moe")(*tables, xn, w_gate, w_up, w_down)


def _dispatch(eid, T):
    n_assign = TOP_K * T
    n_blocks = n_assign // MOE_BLK + N_EXPERTS
    e_flat = eid.T.reshape(-1)
    keys = jnp.sort(e_flat * n_assign + jnp.arange(n_assign, dtype=i32))
    asg = jnp.concatenate([keys % n_assign, jnp.zeros((MOE_BLK,), i32)])
    bounds = jnp.searchsorted(keys, jnp.arange(N_EXPERTS + 1, dtype=i32) * n_assign).astype(i32)
    start, counts = bounds[:-1], bounds[1:] - bounds[:-1]
    nblk = (counts + MOE_BLK - 1) // MOE_BLK
    bend = jnp.cumsum(nblk)
    n_used = bend[-1]
    b = jnp.arange(n_blocks, dtype=i32)
    be = jnp.minimum(jnp.searchsorted(bend, jnp.minimum(b, n_used - 1), side='right'), N_EXPERTS - 1).astype(i32)
    j = b - (bend - nblk)[be]
    used = b < n_used
    cstart = jnp.where(used, start[be] + j * MOE_BLK, 0).astype(i32)
    nvalid = jnp.where(used, jnp.clip(counts[be] - j * MOE_BLK, 0, MOE_BLK), 0).astype(i32)
    return asg, cstart, nvalid, be, n_used.reshape(1).astype(i32)


def _final_kernel(h_ref, y0_ref, y1_ref, gate_ref, g_ref, o_ref):
    o_ref[...] = _rms(_combine(h_ref, y0_ref, y1_ref, gate_ref), g_ref[...])


def _final(h, moe_out, g):
    T = h.shape[0]
    y, gate = moe_out
    return pl.pallas_call(
        _final_kernel, grid=(T // TM,),
        in_specs=_combine_specs(T) + [pl.BlockSpec((1, D_MODEL), lambda i: (0, 0))],
        out_specs=pl.BlockSpec((TM, D_MODEL), lambda i: (i, 0)),
        out_shape=jax.ShapeDtypeStruct((T, D_MODEL), f32),
        compiler_params=_cparams(1), name="final_norm")(h, y, y, gate, g)


def _t5_bucket(n):
    nf = jnp.maximum(n, 1).astype(f32)
    large = REL_MAX_EXACT + (jnp.log(nf / REL_MAX_EXACT) / math.log(REL_MAX_DISTANCE / REL_MAX_EXACT)
                             * (REL_BUCKETS - REL_MAX_EXACT)).astype(i32)
    return jnp.where(n < REL_MAX_EXACT, n, jnp.minimum(large, REL_BUCKETS - 1))


def _split_hi_lo(w):
    hi = w.astype(bf16)
    return hi, (w - hi.astype(f32)).astype(bf16)


def kernel(x, positions, attn_norm, w_in, pool_w, pool_scale, mla_q_norm, mla_w_uq, mla_kv_norm, mla_w_ukv,
           diff_lambda, diff_subln, rel_bias, w_out, ffn_norm, router_group_w, router_group_b,
           router_expert_w, router_expert_b, expert_w_gate, expert_w_up, expert_w_down, final_norm):
    B, S, D = x.shape
    T = B * S
    depth = w_in.shape[0]
    assert D == D_MODEL and S % TQ == 0 and T % TM == 0 and (TOP_K * T) % MOE_BLK == 0

    inv = 1.0 / (ROPE_THETA ** (jnp.arange(0, MLA_ROPE, 2, dtype=f32) / MLA_ROPE))
    ang = positions.astype(f32).reshape(T, 1) * inv
    zeros = jnp.zeros((T, LANE - MLA_ROPE), f32)
    cos = jnp.concatenate([jnp.cos(ang), jnp.cos(ang), zeros], axis=1)
    sin = jnp.concatenate([jnp.sin(ang), jnp.sin(ang), zeros], axis=1)
    cost, sint = jnp.cos(ang).T, jnp.sin(ang).T
    assert REL_CLIP == LANE - 1
    tab = jnp.pad(rel_bias[_t5_bucket(jnp.arange(LANE, dtype=i32))].T.astype(f32), ((0, 8 - DIFF_HEADS), (0, 0)))

    h = x.reshape(T, D)
    moe_out = None
    for l in range(depth):
        wl = w_in[l]
        w_in_p = jnp.concatenate([wl[:, :COL_KR + MLA_ROPE], jnp.zeros((D, LANE - MLA_ROPE), f32),
                                  wl[:, COL_KR + MLA_ROPE:]], axis=1).astype(bf16)
        wq = mla_w_uq[l].reshape(MLA_Q_RANK, MLA_HEADS, MLA_NOPE + MLA_ROPE)
        wqt = jnp.pad(wq, ((0, 0), (0, 0), (0, QK_PAD - MLA_NOPE - MLA_ROPE))).reshape(MLA_Q_RANK, -1).T.astype(bf16)
        wkv = mla_w_ukv[l].reshape(MLA_KV_RANK, MLA_HEADS, MLA_NOPE + MLA_V)
        wk = wkv[:, :, :MLA_NOPE].reshape(MLA_KV_RANK, -1).astype(bf16)
        wvt = wkv[:, :, MLA_NOPE:].reshape(MLA_KV_RANK, -1).T.astype(bf16)
        wo = w_out[l].astype(bf16)
        wa, wb, wc = wo[:POOL_WIDTH], wo[POOL_WIDTH:POOL_WIDTH + MLA_HEADS * MLA_V], wo[POOL_WIDTH + MLA_HEADS * MLA_V:]
        wr = jnp.zeros((D, ROUTER_ROWS), f32).at[:, :N_GROUPS].set(router_group_w[l]).at[:, 8:8 + N_EXPERTS].set(router_expert_w[l])
        wr_hi, wr_lo = _split_hi_lo(wr)
        wr_t = jnp.concatenate([wr_hi, wr_lo], axis=1).T
        br = jnp.zeros((ROUTER_ROWS,), f32).at[:N_GROUPS].set(router_group_b[l]).at[8:8 + N_EXPERTS].set(router_expert_b[l])
        lam_init = 0.8 - 0.6 * math.exp(-0.3 * l)

        proj, h = _inproj(h, moe_out, attn_norm[l].reshape(1, D), w_in_p)
        a_out = _pool(proj, pool_w[l].astype(bf16), pool_scale[l].reshape(1, POOL_WIDTH), B, S)
        qt, k, vt = _mla_prep(proj, cos, sin, cost, sint, mla_q_norm[l].reshape(1, -1), mla_kv_norm[l].reshape(1, -1),
                              wqt, wk, wvt, B, S)
        b_out = _mla_attn(qt, k, vt, B, S)
        c_out = _diff_attn(proj, diff_lambda[l], positions, tab, diff_subln[l].reshape(1, -1), lam_init, B, S)
        h, xn, eid, gate = _outproj_router(a_out, b_out, c_out, wa, wb, wc, h, ffn_norm[l].reshape(1, D),
                                           wr_t, br.reshape(ROUTER_ROWS, 1))
        y = _moe(xn, _dispatch(eid, T), expert_w_gate, expert_w_up, expert_w_down, l)
        moe_out = (y, gate.T)
    out = _final(h, moe_out, final_norm.reshape(1, D))
    return out.reshape(B, S, D)
```

```python
import functools
import math

import jax
import jax.numpy as jnp
from jax import lax
from jax.experimental import pallas as pl
from jax.experimental.pallas import tpu as pltpu

f32, bf16, i32 = jnp.float32, jnp.bfloat16, jnp.int32

D_MODEL = 2048
POOL_WINDOWS = (2, 4, 8, 16)
POOL_GROUPS = 4
POOL_WIDTH = 512
MLA_NOPE, MLA_ROPE, MLA_V, MLA_HEADS = 128, 64, 128, 6
MLA_Q_RANK, MLA_KV_RANK = 512, 256
ROPE_THETA = 10000.0
DIFF_QK, DIFF_V, DIFF_HEADS = 64, 128, 6
REL_BUCKETS, REL_MAX_EXACT, REL_MAX_DISTANCE = 32, 16, 128
N_GROUPS, EXPERTS_PER_GROUP, N_EXPERTS, TOP_K, EXPERT_HIDDEN = 4, 8, 32, 2, 512
NORM_EPS = 1e-6

LANE = 128
MXU_DIM = 256
VMEM_LIMIT = 56 * 1024 * 1024

N_IN_PAD = 3712
COL_CQ, COL_CKV, COL_KR, COL_DQ, COL_DK, COL_DV = 512, 1024, 1280, 1408, 2176, 2944
QK_PAD = 2 * LANE

TM = 256
TQ = 512
TK = 256
MOE_BLK = 256
SLAB_ROWS = D_MODEL // LANE
REL_CLIP = REL_MAX_DISTANCE - 1
REL_FAR = REL_MAX_DISTANCE


def _cparams(n_axes):
    return pltpu.CompilerParams(dimension_semantics=("arbitrary",) * n_axes, vmem_limit_bytes=VMEM_LIMIT)


def _rms(x, g):
    return x * lax.rsqrt(jnp.mean(x * x, axis=-1, keepdims=True) + NORM_EPS) * g


def _unslab(ref, rows):
    return jnp.concatenate([ref[pl.ds(s, rows, stride=SLAB_ROWS), :] for s in range(SLAB_ROWS)], axis=1)


def _to_slab(ref, val):
    for s in range(SLAB_ROWS):
        ref[pl.ds(s, val.shape[0], stride=SLAB_ROWS), :] = val[:, s * LANE:(s + 1) * LANE]


def _combine(h_ref, y0_ref, y1_ref, gate_ref):
    gate = gate_ref[...]
    return h_ref[...] + _unslab(y0_ref, TM) * gate[:, 0:1] + _unslab(y1_ref, TM) * gate[:, 1:2]


def _combine_specs(T):
    plane1 = T // TM
    return [pl.BlockSpec((TM, D_MODEL), lambda i: (i, 0)),
            pl.BlockSpec((TM * SLAB_ROWS, LANE), lambda i: (i, 0)),
            pl.BlockSpec((TM * SLAB_ROWS, LANE), lambda i: (plane1 + i, 0)),
            pl.BlockSpec((TM, TOP_K), lambda i: (i, 0))]


def _inproj_kernel(*refs, combine):
    if combine:
        h_ref, y0_ref, y1_ref, gate_ref, g_ref, w_ref, proj_ref, hout_ref = refs
        h = _combine(h_ref, y0_ref, y1_ref, gate_ref)
        hout_ref[...] = h
    else:
        h_ref, g_ref, w_ref, proj_ref = refs
        h = h_ref[...]
    u = _rms(h, g_ref[...]).astype(bf16)
    for n0 in range(0, N_IN_PAD, 512):
        n1 = min(n0 + 512, N_IN_PAD)
        proj_ref[:, n0:n1] = jnp.dot(u, w_ref[:, n0:n1], preferred_element_type=f32).astype(bf16)


def _inproj(h, moe_out, g, w):
    T = h.shape[0]
    combine = moe_out is not None
    row = lambda i: (i, 0)
    const = lambda i: (0, 0)
    if combine:
        y, gate = moe_out
        in_specs = _combine_specs(T)
        args = [h, y, y, gate]
    else:
        in_specs = [pl.BlockSpec((TM, D_MODEL), row)]
        args = [h]
    in_specs += [pl.BlockSpec((1, D_MODEL), const),
                 pl.BlockSpec((D_MODEL, N_IN_PAD), const, pipeline_mode=pl.Buffered(1))]
    args += [g, w]
    out_shape = [jax.ShapeDtypeStruct((T, N_IN_PAD), bf16)]
    out_specs = [pl.BlockSpec((TM, N_IN_PAD), row)]
    if combine:
        out_shape.append(jax.ShapeDtypeStruct((T, D_MODEL), f32))
        out_specs.append(pl.BlockSpec((TM, D_MODEL), row))
    res = pl.pallas_call(
        functools.partial(_inproj_kernel, combine=combine),
        grid=(T // TM,), in_specs=in_specs, out_specs=out_specs, out_shape=out_shape,
        compiler_params=_cparams(1), name="inproj_c" if combine else "inproj")(*args)
    return (res[0], res[1]) if combine else (res[0], h)


def _pool_kernel(x_ref, w_ref, sc_ref, o_ref):
    g = pl.program_id(1)
    x = x_ref[...].astype(f32)
    t = lax.broadcasted_iota(i32, x.shape, 0)

    def lagged(v, k):
        return jnp.where(t >= k, pltpu.roll(v, k, axis=0), 0.0)

    s2 = x + lagged(x, 1)
    s4 = s2 + lagged(s2, 2)
    s8 = s4 + lagged(s4, 4)
    s16 = s8 + lagged(s8, 8)
    wsum = jnp.where(g == 0, s2, jnp.where(g == 1, s4, jnp.where(g == 2, s8, s16)))
    win = jnp.where(g == 0, 2, jnp.where(g == 1, 4, jnp.where(g == 2, 8, 16)))
    cnt = jnp.minimum(t + 1, win).astype(f32)
    d = (wsum / cnt - x).astype(bf16)
    y = jnp.dot(d, w_ref[...], preferred_element_type=f32) * sc_ref[...]
    o_ref[...] = y.astype(bf16)


def _pool(proj, w, scale, B, S):
    T = B * S
    return pl.pallas_call(
        _pool_kernel, grid=(B, POOL_GROUPS),
        in_specs=[pl.BlockSpec((S, LANE), lambda b, g: (b, g)),
                  pl.BlockSpec((None, LANE, LANE), lambda b, g: (g, 0, 0)),
                  pl.BlockSpec((1, LANE), lambda b, g: (0, g))],
        out_specs=pl.BlockSpec((S, LANE), lambda b, g: (b, g)),
        out_shape=jax.ShapeDtypeStruct((T, POOL_WIDTH), bf16),
        compiler_params=_cparams(2), name="pool")(proj, w, scale)


VT_ROWS = MLA_V + 16


def _nt_dot(a, b):
    return lax.dot_general(a, b, (((1,), (1,)), ((), ())), preferred_element_type=f32)


def _rope128(r, cos, sin):
    lane = lax.broadcasted_iota(i32, r.shape, 1)
    half = MLA_ROPE // 2
    partner = jnp.where(lane < half, -pltpu.roll(r, LANE - half, axis=1), pltpu.roll(r, half, axis=1))
    return r * cos + partner * sin


def _ones_row_pad(n_cols):
    row = lax.broadcasted_iota(i32, (VT_ROWS - MLA_V, n_cols), 0)
    return jnp.where(row == 0, 1.0, 0.0).astype(bf16)


def _mla_prep_kernel(cq_ref, ckv_ref, kr_ref, cos_ref, sin_ref, cost_ref, sint_ref, qn_ref, kvn_ref,
                     wqt_ref, wk_ref, wvt_ref, qt_ref, k_ref, vt_ref):
    scale = (MLA_NOPE + MLA_ROPE) ** -0.5
    half = MLA_ROPE // 2
    cost, sint = cost_ref[...], sint_ref[...]
    cq = _rms(cq_ref[...].astype(f32), qn_ref[...]).astype(bf16)
    ckv = _rms(ckv_ref[...].astype(f32), kvn_ref[...]).astype(bf16)
    kr = _rope128(kr_ref[...].astype(f32), cos_ref[...], sin_ref[...]).astype(bf16)
    pad = _ones_row_pad(TM)
    for h in range(MLA_HEADS):
        r0 = h * QK_PAD
        qt = _nt_dot(wqt_ref[r0:r0 + QK_PAD, :], cq) * scale
        x1, x2 = qt[MLA_NOPE:MLA_NOPE + half], qt[MLA_NOPE + half:MLA_NOPE + MLA_ROPE]
        qt_ref[r0:r0 + MLA_NOPE, :] = qt[:MLA_NOPE].astype(bf16)
        qt_ref[r0 + MLA_NOPE:r0 + MLA_NOPE + half, :] = (x1 * cost - x2 * sint).astype(bf16)
        qt_ref[r0 + MLA_NOPE + half:r0 + MLA_NOPE + MLA_ROPE, :] = (x1 * sint + x2 * cost).astype(bf16)
        qt_ref[r0 + MLA_NOPE + MLA_ROPE:r0 + QK_PAD, :] = qt[MLA_NOPE + MLA_ROPE:].astype(bf16)
        kn = jnp.dot(ckv, wk_ref[:, h * LANE:(h + 1) * LANE], preferred_element_type=f32)
        k_ref[:, r0:r0 + LANE] = kn.astype(bf16)
        k_ref[:, r0 + LANE:r0 + QK_PAD] = kr
        vt = _nt_dot(wvt_ref[h * MLA_V:(h + 1) * MLA_V, :], ckv)
        vt_ref[h, 0:MLA_V, :] = vt.astype(bf16)
        vt_ref[h, MLA_V:VT_ROWS, :] = pad


def _mla_prep(proj, cos, sin, cost, sint, qn, kvn, wqt, wk, wvt, B, S):
    T = B * S
    assert TM == TK
    nk = S // TK
    const = lambda i: (0, 0)
    nq, nkc = MLA_HEADS * QK_PAD, MLA_HEADS * LANE
    return pl.pallas_call(
        _mla_prep_kernel, grid=(T // TM,),
        in_specs=[pl.BlockSpec((TM, MLA_Q_RANK), lambda i: (i, COL_CQ // MLA_Q_RANK)),
                  pl.BlockSpec((TM, MLA_KV_RANK), lambda i: (i, COL_CKV // MLA_KV_RANK)),
                  pl.BlockSpec((TM, LANE), lambda i: (i, COL_KR // LANE)),
                  pl.BlockSpec((TM, LANE), lambda i: (i, 0)),
                  pl.BlockSpec((TM, LANE), lambda i: (i, 0)),
                  pl.BlockSpec((MLA_ROPE // 2, TM), lambda i: (0, i)),
                  pl.BlockSpec((MLA_ROPE // 2, TM), lambda i: (0, i)),
                  pl.BlockSpec((1, MLA_Q_RANK), const),
                  pl.BlockSpec((1, MLA_KV_RANK), const),
                  pl.BlockSpec((nq, MLA_Q_RANK), const),
                  pl.BlockSpec((MLA_KV_RANK, nkc), const),
                  pl.BlockSpec((MLA_HEADS * MLA_V, MLA_KV_RANK), const)],
        out_specs=[pl.BlockSpec((nq, TM), lambda i: (0, i)),
                   pl.BlockSpec((TM, nq), lambda i: (i, 0)),
                   pl.BlockSpec((None, MLA_HEADS, None, VT_ROWS, TK), lambda i: (i // nk, 0, i % nk, 0, 0))],
        out_shape=[jax.ShapeDtypeStruct((nq, T), bf16), jax.ShapeDtypeStruct((T, nq), bf16),
                   jax.ShapeDtypeStruct((B, MLA_HEADS, nk, VT_ROWS, TK), bf16)],
        compiler_params=_cparams(1), name="mla_prep")(proj, proj, proj, cos, sin, cost, sint, qn, kvn, wqt, wk, wvt)


def _softmax_probs(s, m, shift, lo=0):
    m_old = m[:, lo:]
    m_new = jnp.maximum(m_old, jnp.max(s, axis=0, keepdims=True) + shift)
    alpha = jnp.exp(m_old - m_new)
    p = jnp.exp(s - (m_new - shift)).astype(bf16)
    return (m_new if lo == 0 else jnp.concatenate([m[:, :lo], m_new], axis=1)), alpha, p


def _accumulate(acc_ref, idx, alpha, vt, p, lo=0):
    acc_ref[idx, :, lo:] = alpha * acc_ref[idx, :, lo:] + jnp.dot(vt, p, preferred_element_type=f32)


def _attn_out(acc):
    return (acc[0:MLA_V] / acc[MLA_V:MLA_V + 1]).T


def _diag_visible(lo):
    krow = lax.broadcasted_iota(i32, (TK, TQ - lo), 0)
    qcol = lax.broadcasted_iota(i32, (TK, TQ - lo), 1)
    return krow <= qcol


MLA_HEADS_PER_STEP = 2


def _mla_attn_kernel(qt_ref, k_ref, vt_ref, o_ref, acc_ref, p_ref):
    i = pl.program_id(2)
    acc_ref[...] = jnp.zeros_like(acc_ref)
    p_ref[...] = jnp.zeros_like(p_ref)
    n_full = i * (TQ // TK)
    heads = range(MLA_HEADS_PER_STEP)

    def scores(c, j, lo=0):
        k = k_ref[pl.ds(pl.multiple_of(j * TK, TK), TK), c * QK_PAD:(c + 1) * QK_PAD]
        return jnp.dot(k, qt_ref[c * QK_PAD:(c + 1) * QK_PAD, lo:], preferred_element_type=f32)

    def flush(alphas, j):
        for c in heads:
            _accumulate(acc_ref, c, alphas[c], vt_ref[c, j], p_ref[c])

    def full(j, carry):
        ms, alphas = carry
        flush(alphas, jnp.maximum(j - 1, 0))
        new = [_softmax_probs(scores(c, j), ms[c], 0.0) for c in heads]
        for c in heads:
            p_ref[c] = new[c][2]
        return tuple(n[0] for n in new), tuple(n[1] for n in new)

    minf = jnp.full((1, TQ), -jnp.inf, f32)
    one = jnp.ones((1, TQ), f32)
    ms, alphas = lax.fori_loop(0, n_full, full, ((minf,) * MLA_HEADS_PER_STEP, (one,) * MLA_HEADS_PER_STEP))
    flush(alphas, jnp.maximum(n_full - 1, 0))
    for d in range(TQ // TK):
        j, lo = n_full + d, d * TK
        for c in heads:
            m, alpha, p = _softmax_probs(jnp.where(_diag_visible(lo), scores(c, j, lo), -jnp.inf), ms[c], 0.0, lo)
            _accumulate(acc_ref, c, alpha, vt_ref[c, j], p, lo)
            ms = ms[:c] + (m,) + ms[c + 1:]
    for c in heads:
        o_ref[:, c * MLA_V:(c + 1) * MLA_V] = _attn_out(acc_ref[c]).astype(bf16)


def _mla_attn(qt, k, vt, B, S):
    T = B * S
    nq, nk = S // TQ, S // TK
    hs = MLA_HEADS_PER_STEP
    return pl.pallas_call(
        _mla_attn_kernel, grid=(B, MLA_HEADS // hs, nq),
        in_specs=[pl.BlockSpec((hs * QK_PAD, TQ), lambda b, h, i: (h, b * nq + i)),
                  pl.BlockSpec((S, hs * QK_PAD), lambda b, h, i: (b, h)),
                  pl.BlockSpec((None, hs, nk, VT_ROWS, TK), lambda b, h, i: (b, h, 0, 0, 0))],
        out_specs=pl.BlockSpec((TQ, hs * MLA_V), lambda b, h, i: (b * nq + i, h)),
        out_shape=jax.ShapeDtypeStruct((T, MLA_HEADS * MLA_V), bf16),
        scratch_shapes=[pltpu.VMEM((hs, VT_ROWS, TQ), f32), pltpu.VMEM((hs, TK, TQ), bf16)],
        compiler_params=_cparams(3), name="mla_attn")(qt, k, vt)


def _diff_attn_kernel(pmin_ref, kmax_ref, lam_ref, q_ref, k_ref, v_ref, posq_ref, posk_ref, tab_ref, sub_ref, o_ref,
                      vt_ref, acc_ref, p_ref, *, lam_init, nq, nk):
    b, h, i = pl.program_id(0), pl.program_id(1), pl.program_id(2)

    @pl.when(i == 0)
    def _():
        pad = _ones_row_pad(TK)
        for j in range(nk):
            vt_ref[j, 0:DIFF_V, :] = v_ref[j * TK:(j + 1) * TK, :].astype(f32).T.astype(bf16)
            vt_ref[j, DIFF_V:VT_ROWS, :] = pad

    lv = lam_ref[...]
    lam = (jnp.exp(jnp.sum(lv[0:1] * lv[1:2], axis=1, keepdims=True))
           - jnp.exp(jnp.sum(lv[2:3] * lv[3:4], axis=1, keepdims=True)) + lam_init)
    qt = (q_ref[...].astype(f32) * (DIFF_QK ** -0.5)).T
    frow = lax.broadcasted_iota(i32, qt.shape, 0)
    qm = (jnp.where(frow < DIFF_QK, qt, 0.0).astype(bf16), jnp.where(frow >= DIFF_QK, qt, 0.0).astype(bf16))
    acc_ref[...] = jnp.zeros_like(acc_ref)
    posq = posq_ref[...]
    tab = jnp.broadcast_to(tab_ref[pl.ds(h, 1), :], (TK, LANE))
    far_bias = tab_ref[pl.ds(h, 1), REL_CLIP:REL_CLIP + 1]
    n_full = i * (TQ // TK)

    def scores(j, lo=0):
        off = pl.multiple_of(j * TK, TK)
        q_both = jnp.concatenate([qm[0][:, lo:], qm[1][:, lo:]], axis=1)
        return jnp.dot(k_ref[pl.ds(off, TK), :], q_both, preferred_element_type=f32), off

    def bias_tile(off, lo):
        n = jnp.clip(posq[:, lo:] - posk_ref[pl.ds(off, TK), :], 0, REL_CLIP)
        return jnp.concatenate(
            [jnp.take_along_axis(tab, n[:, c * LANE:(c + 1) * LANE], axis=1) for c in range((TQ - lo) // LANE)],
            axis=1)

    def flush(alphas, j):
        for c in range(2):
            _accumulate(acc_ref, c, alphas[c], vt_ref[j], p_ref[c])

    def softmax_tile(j, ms, far_tile, lo=0, visible=None):
        s2, off = scores(j, lo)
        bias = None if far_tile else bias_tile(off, lo)
        out = []
        for c in range(2):
            s = s2[:, c * (TQ - lo):(c + 1) * (TQ - lo)]
            if not far_tile:
                s = s + bias
            if visible is not None:
                s = jnp.where(visible, s, -jnp.inf)
            out.append(_softmax_probs(s, ms[c], far_bias if far_tile else 0.0, lo))
        return out

    def below_diagonal(far_tile, j, carry):
        ms, alphas = carry
        flush(alphas, jnp.maximum(j - 1, 0))
        new = softmax_tile(j, ms, far_tile)
        for c in range(2):
            p_ref[c] = new[c][2]
        return tuple(n[0] for n in new), tuple(n[1] for n in new)

    def body(j, carry):
        is_far = pmin_ref[b * nq + i] - kmax_ref[b * nk + j] >= REL_FAR
        return lax.cond(is_far, functools.partial(below_diagonal, True), functools.partial(below_diagonal, False),
                        j, carry)

    p_ref[...] = jnp.zeros_like(p_ref)
    minf = jnp.full((1, TQ), -jnp.inf, f32)
    one = jnp.ones((1, TQ), f32)
    ms, alphas = lax.fori_loop(0, n_full, body, ((minf, minf), (one, one)))
    flush(alphas, jnp.maximum(n_full - 1, 0))
    for d in range(TQ // TK):
        j, lo = n_full + d, d * TK
        new = softmax_tile(j, ms, False, lo, _diag_visible(lo))
        for c in range(2):
            _accumulate(acc_ref, c, new[c][1], vt_ref[j], new[c][2], lo)
        ms = tuple(n[0] for n in new)
    o = _attn_out(acc_ref[0]) - lam * _attn_out(acc_ref[1])
    o_ref[...] = (_rms(o, sub_ref[...]) * (1.0 - lam_init)).astype(bf16)


def _diff_attn(proj, lam_vecs, positions, tab, subln, lam_init, B, S):
    T = B * S
    nq, nk = S // TQ, S // TK
    cq, ck, cv = COL_DQ // LANE, COL_DK // LANE, COL_DV // LANE
    pmin = positions.reshape(B * nq, TQ).min(axis=1)
    kmax = positions.reshape(B * nk, TK).max(axis=1)
    grid_spec = pltpu.PrefetchScalarGridSpec(
        num_scalar_prefetch=2, grid=(B, DIFF_HEADS, nq),
        in_specs=[pl.BlockSpec((4, DIFF_QK), lambda b, h, i, *_: (0, 0)),
                  pl.BlockSpec((TQ, LANE), lambda b, h, i, *_: (b * nq + i, cq + h)),
                  pl.BlockSpec((S, LANE), lambda b, h, i, *_: (b, ck + h)),
                  pl.BlockSpec((S, LANE), lambda b, h, i, *_: (b, cv + h)),
                  pl.BlockSpec((1, TQ), lambda b, h, i, *_: (0, b * nq + i)),
                  pl.BlockSpec((S, 1), lambda b, h, i, *_: (b, 0)),
                  pl.BlockSpec((8, LANE), lambda b, h, i, *_: (0, 0)),
                  pl.BlockSpec((1, DIFF_V), lambda b, h, i, *_: (0, 0))],
        out_specs=pl.BlockSpec((TQ, DIFF_V), lambda b, h, i, *_: (b * nq + i, h)),
        scratch_shapes=[pltpu.VMEM((nk, VT_ROWS, TK), bf16), pltpu.VMEM((2, VT_ROWS, TQ), f32),
                        pltpu.VMEM((2, TK, TQ), bf16)])
    return pl.pallas_call(
        functools.partial(_diff_attn_kernel, lam_init=lam_init, nq=nq, nk=nk), grid_spec=grid_spec,
        out_shape=jax.ShapeDtypeStruct((T, DIFF_HEADS * DIFF_V), bf16),
        compiler_params=_cparams(3), name="diff_attn")(
            pmin, kmax, lam_vecs, proj, proj, proj, positions.reshape(1, T), positions.reshape(T, 1), tab, subln)


ROUTER_ROWS = 64


def _outproj_router_kernel(a_ref, b_ref, c_ref, wa_ref, wb_ref, wc_ref, h_ref, g_ref, wr_ref, br_ref,
                           h2_ref, xn_ref, eid_ref, gate_ref):
    mixed = (jnp.dot(a_ref[...], wa_ref[...], preferred_element_type=f32)
             + jnp.dot(b_ref[...], wb_ref[...], preferred_element_type=f32)
             + jnp.dot(c_ref[...], wc_ref[...], preferred_element_type=f32))
    h2 = h_ref[...] + mixed
    h2_ref[...] = h2
    xn = _rms(h2, g_ref[...])
    _to_slab(xn_ref, xn)
    xh = xn.astype(bf16)
    xl = (xn - xh.astype(f32)).astype(bf16)
    wr = wr_ref[...]
    ah, al = _nt_dot(wr, xh), _nt_dot(wr, xl)
    logit = ah[:ROUTER_ROWS] + ah[ROUTER_ROWS:] + al[:ROUTER_ROWS] + br_ref[...]

    iota8 = lax.broadcasted_iota(i32, (8, TM), 0).astype(f32)
    gl = jnp.where(iota8 < N_GROUPS, logit[0:8], -jnp.inf)
    gmax = jnp.max(gl, axis=0, keepdims=True)
    gidx = jnp.min(jnp.where(gl == gmax, iota8, 8.0), axis=0, keepdims=True)
    ggate = 1.0 / jnp.sum(jnp.exp(gl - gmax), axis=0, keepdims=True)

    e = jnp.zeros((EXPERTS_PER_GROUP, TM), f32)
    for g in range(N_GROUPS):
        e = jnp.where(gidx == g, logit[8 + 8 * g:16 + 8 * g], e)
    m1 = jnp.max(e, axis=0, keepdims=True)
    i1 = jnp.min(jnp.where(e == m1, iota8, 8.0), axis=0, keepdims=True)
    e2 = jnp.where(iota8 == i1, -jnp.inf, e)
    m2 = jnp.max(e2, axis=0, keepdims=True)
    i2 = jnp.min(jnp.where(e2 == m2, iota8, 8.0), axis=0, keepdims=True)
    r = jnp.exp(m2 - m1)
    p1 = 1.0 / (1.0 + r)
    p2 = r / (1.0 + r)
    eid_ref[0:1, :] = (gidx * EXPERTS_PER_GROUP + i1).astype(i32)
    eid_ref[1:2, :] = (gidx * EXPERTS_PER_GROUP + i2).astype(i32)
    gate_ref[0:1, :] = ggate * p1
    gate_ref[1:2, :] = ggate * p2


def _outproj_router(a, b, c, wa, wb, wc, h, g, wr, br):
    T = h.shape[0]
    row = lambda i: (i, 0)
    const = lambda i: (0, 0)
    na, nb, nc = a.shape[1], b.shape[1], c.shape[1]
    return pl.pallas_call(
        _outproj_router_kernel, grid=(T // TM,),
        in_specs=[pl.BlockSpec((TM, na), row), pl.BlockSpec((TM, nb), row), pl.BlockSpec((TM, nc), row),
                  pl.BlockSpec((na, D_MODEL), const, pipeline_mode=pl.Buffered(1)),
                  pl.BlockSpec((nb, D_MODEL), const, pipeline_mode=pl.Buffered(1)),
                  pl.BlockSpec((nc, D_MODEL), const, pipeline_mode=pl.Buffered(1)),
                  pl.BlockSpec((TM, D_MODEL), row),
                  pl.BlockSpec((1, D_MODEL), const),
                  pl.BlockSpec((2 * ROUTER_ROWS, D_MODEL), const),
                  pl.BlockSpec((ROUTER_ROWS, 1), const)],
        out_specs=[pl.BlockSpec((TM, D_MODEL), row), pl.BlockSpec((TM * SLAB_ROWS, LANE), row),
                   pl.BlockSpec((TOP_K, TM), lambda i: (0, i)), pl.BlockSpec((TOP_K, TM), lambda i: (0, i))],
        out_shape=[jax.ShapeDtypeStruct((T, D_MODEL), f32), jax.ShapeDtypeStruct((T * SLAB_ROWS, LANE), f32),
                   jax.ShapeDtypeStruct((TOP_K, T), i32), jax.ShapeDtypeStruct((TOP_K, T), f32)],
        compiler_params=_cparams(1), name="outproj_router")(a, b, c, wa, wb, wc, h, g, wr, br)


def _moe_kernel(asg_ref, cstart_ref, nvalid_ref, be_ref, nused_ref, x_hbm, wg_ref, wu_ref, wd_ref, y_hbm,
                xbuf, ybuf, gsem, ssem, *, n_tokens):
    del be_ref
    b = pl.program_id(0)
    slot = b % 2
    n_used = nused_ref[0]

    def gather_row(blk, buf, r):
        tok = asg_ref[cstart_ref[blk] + r] >> 1
        return pltpu.make_async_copy(x_hbm.at[pl.ds(tok * SLAB_ROWS, SLAB_ROWS)],
                                     xbuf.at[buf, pl.ds(r * SLAB_ROWS, SLAB_ROWS)], gsem.at[buf])

    def scatter_row(blk, buf, r):
        a = asg_ref[cstart_ref[blk] + r]
        dst = (a & 1) * n_tokens + (a >> 1)
        return pltpu.make_async_copy(ybuf.at[buf, pl.ds(r * SLAB_ROWS, SLAB_ROWS)],
                                     y_hbm.at[pl.ds(dst * SLAB_ROWS, SLAB_ROWS)], ssem.at[buf])

    def start_gather(blk, buf):
        def body(r2, carry):
            gather_row(blk, buf, 2 * r2).start(priority=0)
            gather_row(blk, buf, 2 * r2 + 1).start(priority=1)
            return carry
        lax.fori_loop(0, MOE_BLK // 2, body, 0, unroll=4)

    def start_scatter(blk, buf):
        nv = nvalid_ref[blk]

        def group(g, carry):
            for u in range(8):
                scatter_row(blk, buf, 8 * g + u).start(priority=u % 2)
            return carry
        lax.fori_loop(0, nv >> 3, group, 0)

        def single(r, carry):
            scatter_row(blk, buf, r).start(priority=0)
            return carry
        lax.fori_loop(nv & ~7, nv, single, 0)

    def wait_gather(buf):
        pltpu.make_async_copy(x_hbm.at[pl.ds(0, MOE_BLK * SLAB_ROWS)], xbuf.at[buf], gsem.at[buf]).wait()

    def wait_scatter(blk, buf):
        nv = nvalid_ref[blk]
        rows = MOE_BLK
        while rows >= 1:
            @pl.when((nv & rows) != 0)
            def _(rows=rows):
                n = rows * SLAB_ROWS
                pltpu.make_async_copy(ybuf.at[buf, pl.ds(0, n)], y_hbm.at[pl.ds(0, n)], ssem.at[buf]).wait()
            rows //= 2

    @pl.when(b == 0)
    def _():
        start_gather(0, 0)

    @pl.when(b + 1 < n_used)
    def _():
        start_gather(b + 1, 1 - slot)

    @pl.when(b < n_used)
    def _():
        wait_gather(slot)

        @pl.when(b >= 2)
        def _():
            wait_scatter(b - 2, slot)

        x = _unslab(xbuf.at[slot], MOE_BLK)
        hg = jnp.dot(x, wg_ref[...], preferred_element_type=f32)
        hu = jnp.dot(x, wu_ref[...], preferred_element_type=f32)
        hid = hg * jax.nn.sigmoid(hg) * hu
        _to_slab(ybuf.at[slot], jnp.dot(hid, wd_ref[...], preferred_element_type=f32))
        start_scatter(b, slot)

    @pl.when(b == n_used - 1)
    def _():
        wait_scatter(b, slot)

        @pl.when(b >= 1)
        def _():
            wait_scatter(b - 1, 1 - slot)


def _moe(xn, tables, w_gate, w_up, w_down, layer):
    T = xn.shape[0] // SLAB_ROWS
    n_blocks = tables[3].shape[0]
    wspec = lambda shape: pl.BlockSpec((None, None) + shape, lambda b, a, c, n, e, u: (layer, e[b], 0, 0))
    grid_spec = pltpu.PrefetchScalarGridSpec(
        num_scalar_prefetch=5, grid=(n_blocks,),
        in_specs=[pl.BlockSpec(memory_space=pl.ANY),
                  wspec((D_MODEL, EXPERT_HIDDEN)), wspec((D_MODEL, EXPERT_HIDDEN)), wspec((EXPERT_HIDDEN, D_MODEL))],
        out_specs=pl.BlockSpec(memory_space=pl.ANY),
        scratch_shapes=[pltpu.VMEM((2, MOE_BLK * SLAB_ROWS, LANE), f32), pltpu.VMEM((2, MOE_BLK * SLAB_ROWS, LANE), f32),
                        pltpu.SemaphoreType.DMA((2,)), pltpu.SemaphoreType.DMA((2,))])
    return pl.pallas_call(
        functools.partial(_moe_kernel, n_tokens=T), grid_spec=grid_spec,
        out_shape=jax.ShapeDtypeStruct((TOP_K * T * SLAB_ROWS, LANE), f32),
        compiler_params=_cparams(1), name="moe")(*tables, xn, w_gate, w_up, w_down)


def _dispatch(eid, T):
    n_assign = TOP_K * T
    n_blocks = n_assign // MOE_BLK + N_EXPERTS
    e_flat = eid.T.reshape(-1)
    keys = jnp.sort(e_flat * n_assign + jnp.arange(n_assign, dtype=i32))
    asg = jnp.concatenate([keys % n_assign, jnp.zeros((MOE_BLK,), i32)])
    bounds = jnp.searchsorted(keys, jnp.arange(N_EXPERTS + 1, dtype=i32) * n_assign).astype(i32)
    start, counts = bounds[:-1], bounds[1:] - bounds[:-1]
    nblk = (counts + MOE_BLK - 1) // MOE_BLK
    bend = jnp.cumsum(nblk)
    n_used = bend[-1]
    b = jnp.arange(n_blocks, dtype=i32)
    be = jnp.minimum(jnp.searchsorted(bend, jnp.minimum(b, n_used - 1), side='right'), N_EXPERTS - 1).astype(i32)
    j = b - (bend - nblk)[be]
    used = b < n_used
    cstart = jnp.where(used, start[be] + j * MOE_BLK, 0).astype(i32)
    nvalid = jnp.where(used, jnp.clip(counts[be] - j * MOE_BLK, 0, MOE_BLK), 0).astype(i32)
    return asg, cstart, nvalid, be, n_used.reshape(1).astype(i32)


def _final_kernel(h_ref, y0_ref, y1_ref, gate_ref, g_ref, o_ref):
    o_ref[...] = _rms(_combine(h_ref, y0_ref, y1_ref, gate_ref), g_ref[...])


def _final(h, moe_out, g):
    T = h.shape[0]
    y, gate = moe_out
    return pl.pallas_call(
        _final_kernel, grid=(T // TM,),
        in_specs=_combine_specs(T) + [pl.BlockSpec((1, D_MODEL), lambda i: (0, 0))],
        out_specs=pl.BlockSpec((TM, D_MODEL), lambda i: (i, 0)),
        out_shape=jax.ShapeDtypeStruct((T, D_MODEL), f32),
        compiler_params=_cparams(1), name="final_norm")(h, y, y, gate, g)


def _t5_bucket(n):
    nf = jnp.maximum(n, 1).astype(f32)
    large = REL_MAX_EXACT + (jnp.log(nf / REL_MAX_EXACT) / math.log(REL_MAX_DISTANCE / REL_MAX_EXACT)
                             * (REL_BUCKETS - REL_MAX_EXACT)).astype(i32)
    return jnp.where(n < REL_MAX_EXACT, n, jnp.minimum(large, REL_BUCKETS - 1))


def _split_hi_lo(w):
    hi = w.astype(bf16)
    return hi, (w - hi.astype(f32)).astype(bf16)


def kernel(x, positions, attn_norm, w_in, pool_w, pool_scale, mla_q_norm, mla_w_uq, mla_kv_norm, mla_w_ukv,
           diff_lambda, diff_subln, rel_bias, w_out, ffn_norm, router_group_w, router_group_b,
           router_expert_w, router_expert_b, expert_w_gate, expert_w_up, expert_w_down, final_norm):
    B, S, D = x.shape
    T = B * S
    depth = w_in.shape[0]
    assert D == D_MODEL and S % TQ == 0 and T % TM == 0 and (TOP_K * T) % MOE_BLK == 0

    inv = 1.0 / (ROPE_THETA ** (jnp.arange(0, MLA_ROPE, 2, dtype=f32) / MLA_ROPE))
    ang = positions.astype(f32).reshape(T, 1) * inv
    zeros = jnp.zeros((T, LANE - MLA_ROPE), f32)
    cos = jnp.concatenate([jnp.cos(ang), jnp.cos(ang), zeros], axis=1)
    sin = jnp.concatenate([jnp.sin(ang), jnp.sin(ang), zeros], axis=1)
    cost, sint = jnp.cos(ang).T, jnp.sin(ang).T
    assert REL_CLIP == LANE - 1
    tab = jnp.pad(rel_bias[_t5_bucket(jnp.arange(LANE, dtype=i32))].T.astype(f32), ((0, 8 - DIFF_HEADS), (0, 0)))

    h = x.reshape(T, D)
    moe_out = None
    for l in range(depth):
        wl = w_in[l]
        w_in_p = jnp.concatenate([wl[:, :COL_KR + MLA_ROPE], jnp.zeros((D, LANE - MLA_ROPE), f32),
                                  wl[:, COL_KR + MLA_ROPE:]], axis=1).astype(bf16)
        wq = mla_w_uq[l].reshape(MLA_Q_RANK, MLA_HEADS, MLA_NOPE + MLA_ROPE)
        wqt = jnp.pad(wq, ((0, 0), (0, 0), (0, QK_PAD - MLA_NOPE - MLA_ROPE))).reshape(MLA_Q_RANK, -1).T.astype(bf16)
        wkv = mla_w_ukv[l].reshape(MLA_KV_RANK, MLA_HEADS, MLA_NOPE + MLA_V)
        wk = wkv[:, :, :MLA_NOPE].reshape(MLA_KV_RANK, -1).astype(bf16)
        wvt = wkv[:, :, MLA_NOPE:].reshape(MLA_KV_RANK, -1).T.astype(bf16)
        wo = w_out[l].astype(bf16)
        wa, wb, wc = wo[:POOL_WIDTH], wo[POOL_WIDTH:POOL_WIDTH + MLA_HEADS * MLA_V], wo[POOL_WIDTH + MLA_HEADS * MLA_V:]
        wr = jnp.zeros((D, ROUTER_ROWS), f32).at[:, :N_GROUPS].set(router_group_w[l]).at[:, 8:8 + N_EXPERTS].set(router_expert_w[l])
        wr_hi, wr_lo = _split_hi_lo(wr)
        wr_t = jnp.concatenate([wr_hi, wr_lo], axis=1).T
        br = jnp.zeros((ROUTER_ROWS,), f32).at[:N_GROUPS].set(router_group_b[l]).at[8:8 + N_EXPERTS].set(router_expert_b[l])
        lam_init = 0.8 - 0.6 * math.exp(-0.3 * l)

        proj, h = _inproj(h, moe_out, attn_norm[l].reshape(1, D), w_in_p)
        a_out = _pool(proj, pool_w[l].astype(bf16), pool_scale[l].reshape(1, POOL_WIDTH), B, S)
        qt, k, vt = _mla_prep(proj, cos, sin, cost, sint, mla_q_norm[l].reshape(1, -1), mla_kv_norm[l].reshape(1, -1),
                              wqt, wk, wvt, B, S)
        b_out = _mla_attn(qt, k, vt, B, S)
        c_out = _diff_attn(proj, diff_lambda[l], positions, tab, diff_subln[l].reshape(1, -1), lam_init, B, S)
        h, xn, eid, gate = _outproj_router(a_out, b_out, c_out, wa, wb, wc, h, ffn_norm[l].reshape(1, D),
                                           wr_t, br.reshape(ROUTER_ROWS, 1))
        y = _moe(xn, _dispatch(eid, T), expert_w_gate, expert_w_up, expert_w_down, l)
        moe_out = (y, gate.T)
    out = _final(h, moe_out, final_norm.reshape(1, D))
    return out.reshape(B, S, D)
```

```python
import functools
import math

import jax
import jax.numpy as jnp
from jax import lax
from jax.experimental import pallas as pl
from jax.experimental.pallas import tpu as pltpu

f32, bf16, i32 = jnp.float32, jnp.bfloat16, jnp.int32

D_MODEL = 2048
POOL_WINDOWS = (2, 4, 8, 16)
POOL_GROUPS = 4
POOL_WIDTH = 512
MLA_NOPE, MLA_ROPE, MLA_V, MLA_HEADS = 128, 64, 128, 6
MLA_Q_RANK, MLA_KV_RANK = 512, 256
ROPE_THETA = 10000.0
DIFF_QK, DIFF_V, DIFF_HEADS = 64, 128, 6
REL_BUCKETS, REL_MAX_EXACT, REL_MAX_DISTANCE = 32, 16, 128
N_GROUPS, EXPERTS_PER_GROUP, N_EXPERTS, TOP_K, EXPERT_HIDDEN = 4, 8, 32, 2, 512
NORM_EPS = 1e-6

LANE = 128
MXU_DIM = 256
VMEM_LIMIT = 56 * 1024 * 1024

N_IN_PAD = 3712
COL_CQ, COL_CKV, COL_KR, COL_DQ, COL_DK, COL_DV = 512, 1024, 1280, 1408, 2176, 2944
QK_PAD = 2 * LANE

TM = 256
TQ = 512
TK = 256
MOE_BLK = 256
SLAB_ROWS = D_MODEL // LANE
REL_CLIP = REL_MAX_DISTANCE - 1
REL_FAR = REL_MAX_DISTANCE
LOG2E = math.log2(math.e)


def _cparams(n_axes):
    return pltpu.CompilerParams(dimension_semantics=("arbitrary",) * n_axes, vmem_limit_bytes=VMEM_LIMIT)


def _rms(x, g):
    return x * lax.rsqrt(jnp.mean(x * x, axis=-1, keepdims=True) + NORM_EPS) * g


def _unslab(ref, rows):
    return jnp.concatenate([ref[pl.ds(s, rows, stride=SLAB_ROWS), :] for s in range(SLAB_ROWS)], axis=1)


def _to_slab(ref, val):
    for s in range(SLAB_ROWS):
        ref[pl.ds(s, val.shape[0], stride=SLAB_ROWS), :] = val[:, s * LANE:(s + 1) * LANE]


def _combine(h_ref, y0_ref, y1_ref, gate_ref):
    gate = gate_ref[...]
    return h_ref[...] + _unslab(y0_ref, TM) * gate[:, 0:1] + _unslab(y1_ref, TM) * gate[:, 1:2]


def _combine_specs(T):
    plane1 = T // TM
    return [pl.BlockSpec((TM, D_MODEL), lambda i: (i, 0)),
            pl.BlockSpec((TM * SLAB_ROWS, LANE), lambda i: (i, 0)),
            pl.BlockSpec((TM * SLAB_ROWS, LANE), lambda i: (plane1 + i, 0)),
            pl.BlockSpec((TM, TOP_K), lambda i: (i, 0))]


def _inproj_kernel(*refs, combine):
    if combine:
        h_ref, y0_ref, y1_ref, gate_ref, g_ref, w_ref, proj_ref, hout_ref = refs
        h = _combine(h_ref, y0_ref, y1_ref, gate_ref)
        hout_ref[...] = h
    else:
        h_ref, g_ref, w_ref, proj_ref = refs
        h = h_ref[...]
    u = _rms(h, g_ref[...]).astype(bf16)
    for n0 in range(0, N_IN_PAD, 512):
        n1 = min(n0 + 512, N_IN_PAD)
        proj_ref[:, n0:n1] = jnp.dot(u, w_ref[:, n0:n1], preferred_element_type=f32).astype(bf16)


def _inproj(h, moe_out, g, w):
    T = h.shape[0]
    combine = moe_out is not None
    row = lambda i: (i, 0)
    const = lambda i: (0, 0)
    if combine:
        y, gate = moe_out
        in_specs = _combine_specs(T)
        args = [h, y, y, gate]
    else:
        in_specs = [pl.BlockSpec((TM, D_MODEL), row)]
        args = [h]
    in_specs += [pl.BlockSpec((1, D_MODEL), const),
                 pl.BlockSpec((D_MODEL, N_IN_PAD), const, pipeline_mode=pl.Buffered(1))]
    args += [g, w]
    out_shape = [jax.ShapeDtypeStruct((T, N_IN_PAD), bf16)]
    out_specs = [pl.BlockSpec((TM, N_IN_PAD), row)]
    if combine:
        out_shape.append(jax.ShapeDtypeStruct((T, D_MODEL), f32))
        out_specs.append(pl.BlockSpec((TM, D_MODEL), row))
    res = pl.pallas_call(
        functools.partial(_inproj_kernel, combine=combine),
        grid=(T // TM,), in_specs=in_specs, out_specs=out_specs, out_shape=out_shape,
        compiler_params=_cparams(1), name="inproj_c" if combine else "inproj")(*args)
    return (res[0], res[1]) if combine else (res[0], h)


def _pool_kernel(x_ref, w_ref, sc_ref, o_ref):
    g = pl.program_id(1)
    x = x_ref[...].astype(f32)
    t = lax.broadcasted_iota(i32, x.shape, 0)

    def lagged(v, k):
        return jnp.where(t >= k, pltpu.roll(v, k, axis=0), 0.0)

    s2 = x + lagged(x, 1)
    s4 = s2 + lagged(s2, 2)
    s8 = s4 + lagged(s4, 4)
    s16 = s8 + lagged(s8, 8)
    wsum = jnp.where(g == 0, s2, jnp.where(g == 1, s4, jnp.where(g == 2, s8, s16)))
    win = jnp.where(g == 0, 2, jnp.where(g == 1, 4, jnp.where(g == 2, 8, 16)))
    cnt = jnp.minimum(t + 1, win).astype(f32)
    d = (wsum / cnt - x).astype(bf16)
    y = jnp.dot(d, w_ref[...], preferred_element_type=f32) * sc_ref[...]
    o_ref[...] = y.astype(bf16)


def _pool(proj, w, scale, B, S):
    T = B * S
    return pl.pallas_call(
        _pool_kernel, grid=(B, POOL_GROUPS),
        in_specs=[pl.BlockSpec((S, LANE), lambda b, g: (b, g)),
                  pl.BlockSpec((None, LANE, LANE), lambda b, g: (g, 0, 0)),
                  pl.BlockSpec((1, LANE), lambda b, g: (0, g))],
        out_specs=pl.BlockSpec((S, LANE), lambda b, g: (b, g)),
        out_shape=jax.ShapeDtypeStruct((T, POOL_WIDTH), bf16),
        compiler_params=_cparams(2), name="pool")(proj, w, scale)


VT_ROWS = MLA_V + 16


def _nt_dot(a, b):
    return lax.dot_general(a, b, (((1,), (1,)), ((), ())), preferred_element_type=f32)


def _rope128(r, cos, sin):
    lane = lax.broadcasted_iota(i32, r.shape, 1)
    half = MLA_ROPE // 2
    partner = jnp.where(lane < half, -pltpu.roll(r, LANE - half, axis=1), pltpu.roll(r, half, axis=1))
    return r * cos + partner * sin


def _ones_row_pad(n_cols):
    row = lax.broadcasted_iota(i32, (VT_ROWS - MLA_V, n_cols), 0)
    return jnp.where(row == 0, 1.0, 0.0).astype(bf16)


def _mla_prep_kernel(cq_ref, ckv_ref, kr_ref, cos_ref, sin_ref, cost_ref, sint_ref, qn_ref, kvn_ref,
                     wqt_ref, wk_ref, wvt_ref, qt_ref, k_ref, vt_ref):
    scale = (MLA_NOPE + MLA_ROPE) ** -0.5 * LOG2E
    half = MLA_ROPE // 2
    cost, sint = cost_ref[...], sint_ref[...]
    cq = _rms(cq_ref[...].astype(f32), qn_ref[...]).astype(bf16)
    ckv = _rms(ckv_ref[...].astype(f32), kvn_ref[...]).astype(bf16)
    kr = _rope128(kr_ref[...].astype(f32), cos_ref[...], sin_ref[...]).astype(bf16)
    pad = _ones_row_pad(TM)
    for h in range(MLA_HEADS):
        r0 = h * QK_PAD
        qt = _nt_dot(wqt_ref[r0:r0 + QK_PAD, :], cq) * scale
        x1, x2 = qt[MLA_NOPE:MLA_NOPE + half], qt[MLA_NOPE + half:MLA_NOPE + MLA_ROPE]
        qt_ref[r0:r0 + MLA_NOPE, :] = qt[:MLA_NOPE].astype(bf16)
        qt_ref[r0 + MLA_NOPE:r0 + MLA_NOPE + half, :] = (x1 * cost - x2 * sint).astype(bf16)
        qt_ref[r0 + MLA_NOPE + half:r0 + MLA_NOPE + MLA_ROPE, :] = (x1 * sint + x2 * cost).astype(bf16)
        qt_ref[r0 + MLA_NOPE + MLA_ROPE:r0 + QK_PAD, :] = qt[MLA_NOPE + MLA_ROPE:].astype(bf16)
        kn = jnp.dot(ckv, wk_ref[:, h * LANE:(h + 1) * LANE], preferred_element_type=f32)
        k_ref[:, r0:r0 + LANE] = kn.astype(bf16)
        k_ref[:, r0 + LANE:r0 + QK_PAD] = kr
        vt = _nt_dot(wvt_ref[h * MLA_V:(h + 1) * MLA_V, :], ckv)
        vt_ref[h, 0:MLA_V, :] = vt.astype(bf16)
        vt_ref[h, MLA_V:VT_ROWS, :] = pad


def _mla_prep(proj, cos, sin, cost, sint, qn, kvn, wqt, wk, wvt, B, S):
    T = B * S
    assert TM == TK
    nk = S // TK
    const = lambda i: (0, 0)
    nq, nkc = MLA_HEADS * QK_PAD, MLA_HEADS * LANE
    return pl.pallas_call(
        _mla_prep_kernel, grid=(T // TM,),
        in_specs=[pl.BlockSpec((TM, MLA_Q_RANK), lambda i: (i, COL_CQ // MLA_Q_RANK)),
                  pl.BlockSpec((TM, MLA_KV_RANK), lambda i: (i, COL_CKV // MLA_KV_RANK)),
                  pl.BlockSpec((TM, LANE), lambda i: (i, COL_KR // LANE)),
                  pl.BlockSpec((TM, LANE), lambda i: (i, 0)),
                  pl.BlockSpec((TM, LANE), lambda i: (i, 0)),
                  pl.BlockSpec((MLA_ROPE // 2, TM), lambda i: (0, i)),
                  pl.BlockSpec((MLA_ROPE // 2, TM), lambda i: (0, i)),
                  pl.BlockSpec((1, MLA_Q_RANK), const),
                  pl.BlockSpec((1, MLA_KV_RANK), const),
                  pl.BlockSpec((nq, MLA_Q_RANK), const),
                  pl.BlockSpec((MLA_KV_RANK, nkc), const),
                  pl.BlockSpec((MLA_HEADS * MLA_V, MLA_KV_RANK), const)],
        out_specs=[pl.BlockSpec((nq, TM), lambda i: (0, i)),
                   pl.BlockSpec((TM, nq), lambda i: (i, 0)),
                   pl.BlockSpec((None, MLA_HEADS, None, VT_ROWS, TK), lambda i: (i // nk, 0, i % nk, 0, 0))],
        out_shape=[jax.ShapeDtypeStruct((nq, T), bf16), jax.ShapeDtypeStruct((T, nq), bf16),
                   jax.ShapeDtypeStruct((B, MLA_HEADS, nk, VT_ROWS, TK), bf16)],
        compiler_params=_cparams(1), name="mla_prep")(proj, proj, proj, cos, sin, cost, sint, qn, kvn, wqt, wk, wvt)


def _softmax_probs(s, m, shift, lo=0):
    m_old = m[:, lo:]
    m_new = jnp.maximum(m_old, jnp.max(s, axis=0, keepdims=True) + shift)
    alpha = jnp.exp2(m_old - m_new)
    p = jnp.exp2(s - (m_new - shift)).astype(bf16)
    return (m_new if lo == 0 else jnp.concatenate([m[:, :lo], m_new], axis=1)), alpha, p


def _accumulate(acc_ref, idx, alpha, vt, p, lo=0):
    acc_ref[idx, :, lo:] = alpha * acc_ref[idx, :, lo:] + jnp.dot(vt, p, preferred_element_type=f32)


def _attn_out_t(acc):
    return acc[0:MLA_V] * (1.0 / acc[MLA_V:MLA_V + 1])


def _diag_visible(lo):
    krow = lax.broadcasted_iota(i32, (TK, TQ - lo), 0)
    qcol = lax.broadcasted_iota(i32, (TK, TQ - lo), 1)
    return krow <= qcol


MLA_HEADS_PER_STEP = 2


def _mla_attn_kernel(qt_ref, k_ref, vt_ref, o_ref, acc_ref, p_ref):
    i = pl.program_id(2)
    acc_ref[...] = jnp.zeros_like(acc_ref)
    p_ref[...] = jnp.zeros_like(p_ref)
    n_full = i * (TQ // TK)
    heads = range(MLA_HEADS_PER_STEP)

    def scores(c, j, lo=0):
        k = k_ref[pl.ds(pl.multiple_of(j * TK, TK), TK), c * QK_PAD:(c + 1) * QK_PAD]
        return jnp.dot(k, qt_ref[c * QK_PAD:(c + 1) * QK_PAD, lo:], preferred_element_type=f32)

    def flush(alphas, j):
        for c in heads:
            _accumulate(acc_ref, c, alphas[c], vt_ref[c, j], p_ref[c])

    def full(j, carry):
        ms, alphas = carry
        flush(alphas, jnp.maximum(j - 1, 0))
        new = [_softmax_probs(scores(c, j), ms[c], 0.0) for c in heads]
        for c in heads:
            p_ref[c] = new[c][2]
        return tuple(n[0] for n in new), tuple(n[1] for n in new)

    minf = jnp.full((1, TQ), -jnp.inf, f32)
    one = jnp.ones((1, TQ), f32)
    ms, alphas = lax.fori_loop(0, n_full, full, ((minf,) * MLA_HEADS_PER_STEP, (one,) * MLA_HEADS_PER_STEP))
    flush(alphas, jnp.maximum(n_full - 1, 0))
    for d in range(TQ // TK):
        j, lo = n_full + d, d * TK
        for c in heads:
            m, alpha, p = _softmax_probs(jnp.where(_diag_visible(lo), scores(c, j, lo), -jnp.inf), ms[c], 0.0, lo)
            _accumulate(acc_ref, c, alpha, vt_ref[c, j], p, lo)
            ms = ms[:c] + (m,) + ms[c + 1:]
    for c in heads:
        o_ref[:, c * MLA_V:(c + 1) * MLA_V] = _attn_out_t(acc_ref[c]).T.astype(bf16)


def _mla_attn(qt, k, vt, B, S):
    T = B * S
    nq, nk = S // TQ, S // TK
    hs = MLA_HEADS_PER_STEP
    return pl.pallas_call(
        _mla_attn_kernel, grid=(B, MLA_HEADS // hs, nq),
        in_specs=[pl.BlockSpec((hs * QK_PAD, TQ), lambda b, h, i: (h, b * nq + i)),
                  pl.BlockSpec((S, hs * QK_PAD), lambda b, h, i: (b, h)),
                  pl.BlockSpec((None, hs, nk, VT_ROWS, TK), lambda b, h, i: (b, h, 0, 0, 0))],
        out_specs=pl.BlockSpec((TQ, hs * MLA_V), lambda b, h, i: (b * nq + i, h)),
        out_shape=jax.ShapeDtypeStruct((T, MLA_HEADS * MLA_V), bf16),
        scratch_shapes=[pltpu.VMEM((hs, VT_ROWS, TQ), f32), pltpu.VMEM((hs, TK, TQ), bf16)],
        compiler_params=_cparams(3), name="mla_attn")(qt, k, vt)


def _diff_attn_kernel(pmin_ref, kmax_ref, lam_ref, q_ref, k_ref, v_ref, posq_ref, posk_ref, tab_ref, sub_ref, o_ref,
                      vt_ref, acc_ref, p_ref, *, lam_init, nq, nk):
    b, h, i = pl.program_id(0), pl.program_id(1), pl.program_id(2)

    @pl.when(i == 0)
    def _():
        pad = _ones_row_pad(TK)
        for j in range(nk):
            vt_ref[j, 0:DIFF_V, :] = v_ref[j * TK:(j + 1) * TK, :].astype(f32).T.astype(bf16)
            vt_ref[j, DIFF_V:VT_ROWS, :] = pad

    lv = lam_ref[...]
    lam = (jnp.exp(jnp.sum(lv[0:1] * lv[1:2], axis=1, keepdims=True))
           - jnp.exp(jnp.sum(lv[2:3] * lv[3:4], axis=1, keepdims=True)) + lam_init)
    qt = q_ref[...].astype(f32).T
    frow = lax.broadcasted_iota(i32, qt.shape, 0)
    qm = (jnp.where(frow < DIFF_QK, qt, 0.0).astype(bf16), jnp.where(frow >= DIFF_QK, qt, 0.0).astype(bf16))
    acc_ref[...] = jnp.zeros_like(acc_ref)
    posq = posq_ref[...]
    tab = jnp.broadcast_to(tab_ref[pl.ds(h, 1), :], (TK, LANE))
    far_bias = tab_ref[pl.ds(h, 1), REL_CLIP:REL_CLIP + 1]
    n_full = i * (TQ // TK)

    def scores(j, lo=0):
        off = pl.multiple_of(j * TK, TK)
        q_both = jnp.concatenate([qm[0][:, lo:], qm[1][:, lo:]], axis=1)
        return jnp.dot(k_ref[pl.ds(off, TK), :], q_both, preferred_element_type=f32), off

    def bias_tile(off, lo):
        n = jnp.clip(posq[:, lo:] - posk_ref[pl.ds(off, TK), :], 0, REL_CLIP)
        return jnp.concatenate(
            [jnp.take_along_axis(tab, n[:, c * LANE:(c + 1) * LANE], axis=1) for c in range((TQ - lo) // LANE)],
            axis=1)

    def flush(alphas, j):
        for c in range(2):
            _accumulate(acc_ref, c, alphas[c], vt_ref[j], p_ref[c])

    def softmax_tile(j, ms, far_tile, lo=0, visible=None):
        s2, off = scores(j, lo)
        bias = None if far_tile else bias_tile(off, lo)
        out = []
        for c in range(2):
            s = s2[:, c * (TQ - lo):(c + 1) * (TQ - lo)]
            if not far_tile:
                s = s + bias
            if visible is not None:
                s = jnp.where(visible, s, -jnp.inf)
            out.append(_softmax_probs(s, ms[c], far_bias if far_tile else 0.0, lo))
        return out

    def below_diagonal(far_tile, j, carry):
        ms, alphas = carry
        flush(alphas, jnp.maximum(j - 1, 0))
        new = softmax_tile(j, ms, far_tile)
        for c in range(2):
            p_ref[c] = new[c][2]
        return tuple(n[0] for n in new), tuple(n[1] for n in new)

    def body(j, carry):
        is_far = pmin_ref[b * nq + i] - kmax_ref[b * nk + j] >= REL_FAR
        return lax.cond(is_far, functools.partial(below_diagonal, True), functools.partial(below_diagonal, False),
                        j, carry)

    p_ref[...] = jnp.zeros_like(p_ref)
    minf = jnp.full((1, TQ), -jnp.inf, f32)
    one = jnp.ones((1, TQ), f32)
    ms, alphas = lax.fori_loop(0, n_full, body, ((minf, minf), (one, one)))
    flush(alphas, jnp.maximum(n_full - 1, 0))
    for d in range(TQ // TK):
        j, lo = n_full + d, d * TK
        new = softmax_tile(j, ms, False, lo, _diag_visible(lo))
        for c in range(2):
            _accumulate(acc_ref, c, new[c][1], vt_ref[j], new[c][2], lo)
        ms = tuple(n[0] for n in new)
    ot = _attn_out_t(acc_ref[0]) - lam * _attn_out_t(acc_ref[1])
    ot = ot * lax.rsqrt(jnp.mean(ot * ot, axis=0, keepdims=True) + NORM_EPS) * (sub_ref[...] * (1.0 - lam_init))
    o_ref[...] = ot.T.astype(bf16)


def _diff_attn(proj, lam_vecs, positions, tab, subln, lam_init, B, S):
    T = B * S
    nq, nk = S // TQ, S // TK
    cq, ck, cv = COL_DQ // LANE, COL_DK // LANE, COL_DV // LANE
    pmin = positions.reshape(B * nq, TQ).min(axis=1)
    kmax = positions.reshape(B * nk, TK).max(axis=1)
    grid_spec = pltpu.PrefetchScalarGridSpec(
        num_scalar_prefetch=2, grid=(B, DIFF_HEADS, nq),
        in_specs=[pl.BlockSpec((4, DIFF_QK), lambda b, h, i, *_: (0, 0)),
                  pl.BlockSpec((TQ, LANE), lambda b, h, i, *_: (b * nq + i, cq + h)),
                  pl.BlockSpec((S, LANE), lambda b, h, i, *_: (b, ck + h)),
                  pl.BlockSpec((S, LANE), lambda b, h, i, *_: (b, cv + h)),
                  pl.BlockSpec((1, TQ), lambda b, h, i, *_: (0, b * nq + i)),
                  pl.BlockSpec((S, 1), lambda b, h, i, *_: (b, 0)),
                  pl.BlockSpec((8, LANE), lambda b, h, i, *_: (0, 0)),
                  pl.BlockSpec((DIFF_V, 1), lambda b, h, i, *_: (0, 0))],
        out_specs=pl.BlockSpec((TQ, DIFF_V), lambda b, h, i, *_: (b * nq + i, h)),
        scratch_shapes=[pltpu.VMEM((nk, VT_ROWS, TK), bf16), pltpu.VMEM((2, VT_ROWS, TQ), f32),
                        pltpu.VMEM((2, TK, TQ), bf16)])
    return pl.pallas_call(
        functools.partial(_diff_attn_kernel, lam_init=lam_init, nq=nq, nk=nk), grid_spec=grid_spec,
        out_shape=jax.ShapeDtypeStruct((T, DIFF_HEADS * DIFF_V), bf16),
        compiler_params=_cparams(3), name="diff_attn")(
            pmin, kmax, lam_vecs, proj, proj, proj, positions.reshape(1, T), positions.reshape(T, 1), tab, subln)


ROUTER_ROWS = 64


def _outproj_router_kernel(a_ref, b_ref, c_ref, wo_ref, h_ref, g_ref, wr_ref, br_ref,
                           h2_ref, xn_ref, eid_ref, gate_ref):
    na, nb = a_ref.shape[1], b_ref.shape[1]
    mixed = (jnp.dot(a_ref[...], wo_ref[0:na, :], preferred_element_type=f32)
             + jnp.dot(b_ref[...], wo_ref[na:na + nb, :], preferred_element_type=f32)
             + jnp.dot(c_ref[...], wo_ref[na + nb:, :], preferred_element_type=f32))
    h2 = h_ref[...] + mixed
    h2_ref[...] = h2
    xn = _rms(h2, g_ref[...])
    _to_slab(xn_ref, xn)
    xh = xn.astype(bf16)
    xl = (xn - xh.astype(f32)).astype(bf16)
    wr = wr_ref[...]
    ah = jnp.dot(xh, wr, preferred_element_type=f32)
    al = jnp.dot(xl, wr, preferred_element_type=f32)
    logit = (ah + pltpu.roll(ah, ROUTER_ROWS, axis=1) + al).T[:ROUTER_ROWS] + br_ref[...]

    iota8 = lax.broadcasted_iota(i32, (8, TM), 0).astype(f32)
    gl = jnp.where(iota8 < N_GROUPS, logit[0:8], -jnp.inf)
    gmax = jnp.max(gl, axis=0, keepdims=True)
    gidx = jnp.min(jnp.where(gl == gmax, iota8, 8.0), axis=0, keepdims=True)
    ggate = 1.0 / jnp.sum(jnp.exp(gl - gmax), axis=0, keepdims=True)

    e = jnp.zeros((EXPERTS_PER_GROUP, TM), f32)
    for g in range(N_GROUPS):
        e = jnp.where(gidx == g, logit[8 + 8 * g:16 + 8 * g], e)
    m1 = jnp.max(e, axis=0, keepdims=True)
    i1 = jnp.min(jnp.where(e == m1, iota8, 8.0), axis=0, keepdims=True)
    e2 = jnp.where(iota8 == i1, -jnp.inf, e)
    m2 = jnp.max(e2, axis=0, keepdims=True)
    i2 = jnp.min(jnp.where(e2 == m2, iota8, 8.0), axis=0, keepdims=True)
    r = jnp.exp(m2 - m1)
    p1 = 1.0 / (1.0 + r)
    p2 = r / (1.0 + r)
    eid_ref[0:1, :] = (gidx * EXPERTS_PER_GROUP + i1).astype(i32)
    eid_ref[1:2, :] = (gidx * EXPERTS_PER_GROUP + i2).astype(i32)
    gate_ref[0:1, :] = ggate * p1
    gate_ref[1:2, :] = ggate * p2


def _outproj_router(a, b, c, wo, h, g, wr, br):
    T = h.shape[0]
    row = lambda i: (i, 0)
    const = lambda i: (0, 0)
    na, nb, nc = a.shape[1], b.shape[1], c.shape[1]
    return pl.pallas_call(
        _outproj_router_kernel, grid=(T // TM,),
        in_specs=[pl.BlockSpec((TM, na), row), pl.BlockSpec((TM, nb), row), pl.BlockSpec((TM, nc), row),
                  pl.BlockSpec((na + nb + nc, D_MODEL), const, pipeline_mode=pl.Buffered(1)),
                  pl.BlockSpec((TM, D_MODEL), row),
                  pl.BlockSpec((1, D_MODEL), const),
                  pl.BlockSpec((D_MODEL, 2 * ROUTER_ROWS), const),
                  pl.BlockSpec((ROUTER_ROWS, 1), const)],
        out_specs=[pl.BlockSpec((TM, D_MODEL), row), pl.BlockSpec((TM * SLAB_ROWS, LANE), row),
                   pl.BlockSpec((TOP_K, TM), lambda i: (0, i)), pl.BlockSpec((TOP_K, TM), lambda i: (0, i))],
        out_shape=[jax.ShapeDtypeStruct((T, D_MODEL), f32), jax.ShapeDtypeStruct((T * SLAB_ROWS, LANE), f32),
                   jax.ShapeDtypeStruct((TOP_K, T), i32), jax.ShapeDtypeStruct((TOP_K, T), f32)],
        compiler_params=_cparams(1), name="outproj_router")(a, b, c, wo, h, g, wr, br)


def _moe_kernel(src_ref, dst_ref, cstart_ref, nvalid_ref, be_ref, nused_ref, x_hbm, wg_ref, wu_ref, wd_ref, y_hbm,
                xbuf, ybuf, gsem, ssem):
    del be_ref
    b = pl.program_id(0)
    slot = b % 2
    n_used = nused_ref[0]

    def gather_row(base, buf, r):
        return pltpu.make_async_copy(x_hbm.at[pl.ds(src_ref[base + r] * SLAB_ROWS, SLAB_ROWS)],
                                     xbuf.at[buf, pl.ds(r * SLAB_ROWS, SLAB_ROWS)], gsem.at[buf])

    def scatter_row(base, buf, r):
        return pltpu.make_async_copy(ybuf.at[buf, pl.ds(r * SLAB_ROWS, SLAB_ROWS)],
                                     y_hbm.at[pl.ds(dst_ref[base + r] * SLAB_ROWS, SLAB_ROWS)], ssem.at[buf])

    def start_group(make_row, base, buf, g):
        for u in range(8):
            make_row(base, buf, 8 * g + u).start(priority=u % 2)

    def start_gather(blk, buf):
        base = cstart_ref[blk]

        def group(g, carry):
            start_group(gather_row, base, buf, g)
            return carry
        lax.fori_loop(0, MOE_BLK // 8, group, 0)

    def start_scatter(blk, buf):
        base, nv = cstart_ref[blk], nvalid_ref[blk]

        def group(g, carry):
            start_group(scatter_row, base, buf, g)
            return carry
        lax.fori_loop(0, nv >> 3, group, 0)

        def single(r, carry):
            scatter_row(base, buf, r).start(priority=0)
            return carry
        lax.fori_loop(nv & ~7, nv, single, 0)

    def wait_gather(buf):
        pltpu.make_async_copy(x_hbm.at[pl.ds(0, MOE_BLK * SLAB_ROWS)], xbuf.at[buf], gsem.at[buf]).wait()

    def wait_scatter(blk, buf):
        nv = nvalid_ref[blk]
        rows = MOE_BLK
        while rows >= 1:
            @pl.when((nv & rows) != 0)
            def _(rows=rows):
                n = rows * SLAB_ROWS
                pltpu.make_async_copy(ybuf.at[buf, pl.ds(0, n)], y_hbm.at[pl.ds(0, n)], ssem.at[buf]).wait()
            rows //= 2

    def per_slot(fn):
        for buf in range(2):
            pl.when(slot == buf)(functools.partial(fn, buf))

    @pl.when(b == 0)
    def _():
        start_gather(0, 0)

    @pl.when(b + 1 < n_used)
    def _():
        per_slot(lambda buf: start_gather(b + 1, 1 - buf))

    @pl.when(b < n_used)
    def _():
        wait_gather(slot)

        @pl.when(b >= 2)
        def _():
            wait_scatter(b - 2, slot)

        x = _unslab(xbuf.at[slot], MOE_BLK)
        hg = jnp.dot(x, wg_ref[...], preferred_element_type=f32)
        hu = jnp.dot(x, wu_ref[...], preferred_element_type=f32)
        hid = hg * jax.nn.sigmoid(hg) * hu
        _to_slab(ybuf.at[slot], jnp.dot(hid, wd_ref[...], preferred_element_type=f32))
        per_slot(lambda buf: start_scatter(b, buf))

    @pl.when(b == n_used - 1)
    def _():
        wait_scatter(b, slot)

        @pl.when(b >= 1)
        def _():
            wait_scatter(b - 1, 1 - slot)


def _moe(xn, tables, w_gate, w_up, w_down, layer):
    T = xn.shape[0] // SLAB_ROWS
    n_blocks = tables[4].shape[0]
    wspec = lambda shape: pl.BlockSpec((None, None) + shape, lambda b, s, d, c, n, e, u: (layer, e[b], 0, 0))
    grid_spec = pltpu.PrefetchScalarGridSpec(
        num_scalar_prefetch=6, grid=(n_blocks,),
        in_specs=[pl.BlockSpec(memory_space=pl.ANY),
                  wspec((D_MODEL, EXPERT_HIDDEN)), wspec((D_MODEL, EXPERT_HIDDEN)), wspec((EXPERT_HIDDEN, D_MODEL))],
        out_specs=pl.BlockSpec(memory_space=pl.ANY),
        scratch_shapes=[pltpu.VMEM((2, MOE_BLK * SLAB_ROWS, LANE), f32), pltpu.VMEM((2, MOE_BLK * SLAB_ROWS, LANE), f32),
                        pltpu.SemaphoreType.DMA((2,)), pltpu.SemaphoreType.DMA((2,))])
    return pl.pallas_call(
        _moe_kernel, grid_spec=grid_spec,
        out_shape=jax.ShapeDtypeStruct((TOP_K * T * SLAB_ROWS, LANE), f32),
        compiler_params=_cparams(1), name="moe")(*tables, xn, w_gate, w_up, w_down)


def _dispatch(eid, T):
    n_assign = TOP_K * T
    n_blocks = n_assign // MOE_BLK + N_EXPERTS
    e_flat = eid.T.reshape(-1)
    keys = jnp.sort(e_flat * n_assign + jnp.arange(n_assign, dtype=i32))
    asg = jnp.concatenate([keys % n_assign, jnp.zeros((MOE_BLK,), i32)])
    src, dst = asg // TOP_K, (asg % TOP_K) * T + asg // TOP_K
    edges = jnp.arange(N_EXPERTS + 1, dtype=i32) * n_assign
    bounds = jnp.sum(keys[None, :] < edges[:, None], axis=1, dtype=i32)
    start, counts = bounds[:-1], bounds[1:] - bounds[:-1]
    nblk = (counts + MOE_BLK - 1) // MOE_BLK
    bend = jnp.cumsum(nblk)
    n_used = bend[-1]
    b = jnp.arange(n_blocks, dtype=i32)
    be = jnp.minimum(jnp.sum(bend[None, :] <= jnp.minimum(b, n_used - 1)[:, None], axis=1, dtype=i32), N_EXPERTS - 1)
    j = b - (bend - nblk)[be]
    used = b < n_used
    cstart = jnp.where(used, start[be] + j * MOE_BLK, 0).astype(i32)
    nvalid = jnp.where(used, jnp.clip(counts[be] - j * MOE_BLK, 0, MOE_BLK), 0).astype(i32)
    return src, dst, cstart, nvalid, be, n_used.reshape(1).astype(i32)


def _final_kernel(h_ref, y0_ref, y1_ref, gate_ref, g_ref, o_ref):
    o_ref[...] = _rms(_combine(h_ref, y0_ref, y1_ref, gate_ref), g_ref[...])


def _final(h, moe_out, g):
    T = h.shape[0]
    y, gate = moe_out
    return pl.pallas_call(
        _final_kernel, grid=(T // TM,),
        in_specs=_combine_specs(T) + [pl.BlockSpec((1, D_MODEL), lambda i: (0, 0))],
        out_specs=pl.BlockSpec((TM, D_MODEL), lambda i: (i, 0)),
        out_shape=jax.ShapeDtypeStruct((T, D_MODEL), f32),
        compiler_params=_cparams(1), name="final_norm")(h, y, y, gate, g)


def _t5_bucket(n):
    nf = jnp.maximum(n, 1).astype(f32)
    large = REL_MAX_EXACT + (jnp.log(nf / REL_MAX_EXACT) / math.log(REL_MAX_DISTANCE / REL_MAX_EXACT)
                             * (REL_BUCKETS - REL_MAX_EXACT)).astype(i32)
    return jnp.where(n < REL_MAX_EXACT, n, jnp.minimum(large, REL_BUCKETS - 1))


def _split_hi_lo(w):
    hi = w.astype(bf16)
    return hi, (w - hi.astype(f32)).astype(bf16)


def kernel(x, positions, attn_norm, w_in, pool_w, pool_scale, mla_q_norm, mla_w_uq, mla_kv_norm, mla_w_ukv,
           diff_lambda, diff_subln, rel_bias, w_out, ffn_norm, router_group_w, router_group_b,
           router_expert_w, router_expert_b, expert_w_gate, expert_w_up, expert_w_down, final_norm):
    B, S, D = x.shape
    T = B * S
    depth = w_in.shape[0]
    assert D == D_MODEL and S % TQ == 0 and T % TM == 0 and (TOP_K * T) % MOE_BLK == 0

    inv = 1.0 / (ROPE_THETA ** (jnp.arange(0, MLA_ROPE, 2, dtype=f32) / MLA_ROPE))
    ang = positions.astype(f32).reshape(T, 1) * inv
    zeros = jnp.zeros((T, LANE - MLA_ROPE), f32)
    cos = jnp.concatenate([jnp.cos(ang), jnp.cos(ang), zeros], axis=1)
    sin = jnp.concatenate([jnp.sin(ang), jnp.sin(ang), zeros], axis=1)
    cost, sint = jnp.cos(ang).T, jnp.sin(ang).T
    assert REL_CLIP == LANE - 1
    tab = jnp.pad(rel_bias[_t5_bucket(jnp.arange(LANE, dtype=i32))].T.astype(f32) * LOG2E,
                  ((0, 8 - DIFF_HEADS), (0, 0)))

    h = x.reshape(T, D)
    moe_out = None
    for l in range(depth):
        wl = w_in[l]
        n_dq = COL_DK - COL_DQ
        w_in_p = jnp.concatenate([wl[:, :COL_KR + MLA_ROPE], jnp.zeros((D, LANE - MLA_ROPE), f32),
                                  wl[:, COL_KR + MLA_ROPE:COL_KR + MLA_ROPE + n_dq] * (DIFF_QK ** -0.5 * LOG2E),
                                  wl[:, COL_KR + MLA_ROPE + n_dq:]], axis=1).astype(bf16)
        wq = mla_w_uq[l].reshape(MLA_Q_RANK, MLA_HEADS, MLA_NOPE + MLA_ROPE)
        wqt = jnp.pad(wq, ((0, 0), (0, 0), (0, QK_PAD - MLA_NOPE - MLA_ROPE))).reshape(MLA_Q_RANK, -1).T.astype(bf16)
        wkv = mla_w_ukv[l].reshape(MLA_KV_RANK, MLA_HEADS, MLA_NOPE + MLA_V)
        wk = wkv[:, :, :MLA_NOPE].reshape(MLA_KV_RANK, -1).astype(bf16)
        wvt = wkv[:, :, MLA_NOPE:].reshape(MLA_KV_RANK, -1).T.astype(bf16)
        wo = w_out[l].astype(bf16)
        wr = jnp.zeros((D, ROUTER_ROWS), f32).at[:, :N_GROUPS].set(router_group_w[l]).at[:, 8:8 + N_EXPERTS].set(router_expert_w[l])
        wr_hi, wr_lo = _split_hi_lo(wr)
        wr_cat = jnp.concatenate([wr_hi, wr_lo], axis=1)
        br = jnp.zeros((ROUTER_ROWS,), f32).at[:N_GROUPS].set(router_group_b[l]).at[8:8 + N_EXPERTS].set(router_expert_b[l])
        lam_init = 0.8 - 0.6 * math.exp(-0.3 * l)

        proj, h = _inproj(h, moe_out, attn_norm[l].reshape(1, D), w_in_p)
        a_out = _pool(proj, pool_w[l].astype(bf16), pool_scale[l].reshape(1, POOL_WIDTH), B, S)
        qt, k, vt = _mla_prep(proj, cos, sin, cost, sint, mla_q_norm[l].reshape(1, -1), mla_kv_norm[l].reshape(1, -1),
                              wqt, wk, wvt, B, S)
        b_out = _mla_attn(qt, k, vt, B, S)
        c_out = _diff_attn(proj, diff_lambda[l], positions, tab, diff_subln[l].reshape(-1, 1), lam_init, B, S)
        h, xn, eid, gate = _outproj_router(a_out, b_out, c_out, wo, h, ffn_norm[l].reshape(1, D),
                                           wr_cat, br.reshape(ROUTER_ROWS, 1))
        y = _moe(xn, _dispatch(eid, T), expert_w_gate, expert_w_up, expert_w_down, l)
        moe_out = (y, gate.T)
    out = _final(h, moe_out, final_norm.reshape(1, D))
    return out.reshape(B, S, D)
```

```python
import functools
import math

import jax
import jax.numpy as jnp
from jax import lax
from jax.experimental import pallas as pl
from jax.experimental.pallas import tpu as pltpu

f32, bf16, i32 = jnp.float32, jnp.bfloat16, jnp.int32

D_MODEL = 2048
POOL_WINDOWS = (2, 4, 8, 16)
POOL_GROUPS = 4
POOL_WIDTH = 512
MLA_NOPE, MLA_ROPE, MLA_V, MLA_HEADS = 128, 64, 128, 6
MLA_Q_RANK, MLA_KV_RANK = 512, 256
ROPE_THETA = 10000.0
DIFF_QK, DIFF_V, DIFF_HEADS = 64, 128, 6
REL_BUCKETS, REL_MAX_EXACT, REL_MAX_DISTANCE = 32, 16, 128
N_GROUPS, EXPERTS_PER_GROUP, N_EXPERTS, TOP_K, EXPERT_HIDDEN = 4, 8, 32, 2, 512
NORM_EPS = 1e-6

LANE = 128
MXU_DIM = 256
VMEM_LIMIT = 56 * 1024 * 1024

N_IN_PAD = 3712
COL_CQ, COL_CKV, COL_KR, COL_DQ, COL_DK, COL_DV = 512, 1024, 1280, 1408, 2176, 2944
QK_PAD = 2 * LANE

TM = 256
TQ = 512
TK = 256
MOE_BLK = 256
SLAB_ROWS = D_MODEL // LANE
REL_CLIP = REL_MAX_DISTANCE - 1
REL_FAR = REL_MAX_DISTANCE
LOG2E = math.log2(math.e)


def _cparams(n_axes):
    return pltpu.CompilerParams(dimension_semantics=("arbitrary",) * n_axes, vmem_limit_bytes=VMEM_LIMIT)


def _rms(x, g):
    return x * lax.rsqrt(jnp.mean(x * x, axis=-1, keepdims=True) + NORM_EPS) * g


def _unslab(ref, rows):
    return jnp.concatenate([ref[pl.ds(s, rows, stride=SLAB_ROWS), :] for s in range(SLAB_ROWS)], axis=1)


def _to_slab(ref, val):
    for s in range(SLAB_ROWS):
        ref[pl.ds(s, val.shape[0], stride=SLAB_ROWS), :] = val[:, s * LANE:(s + 1) * LANE]


def _combine(h_ref, y0_ref, y1_ref, gate_ref):
    gate = gate_ref[...]
    return h_ref[...] + _unslab(y0_ref, TM) * gate[:, 0:1] + _unslab(y1_ref, TM) * gate[:, 1:2]


def _combine_specs(T):
    plane1 = T // TM
    return [pl.BlockSpec((TM, D_MODEL), lambda i: (i, 0)),
            pl.BlockSpec((TM * SLAB_ROWS, LANE), lambda i: (i, 0)),
            pl.BlockSpec((TM * SLAB_ROWS, LANE), lambda i: (plane1 + i, 0)),
            pl.BlockSpec((TM, TOP_K), lambda i: (i, 0))]


def _inproj_kernel(*refs, combine):
    if combine:
        h_ref, y0_ref, y1_ref, gate_ref, g_ref, w_ref, proj_ref, hout_ref = refs
        h = _combine(h_ref, y0_ref, y1_ref, gate_ref)
        hout_ref[...] = h
    else:
        h_ref, g_ref, w_ref, proj_ref = refs
        h = h_ref[...]
    u = _rms(h, g_ref[...]).astype(bf16)
    for n0 in range(0, N_IN_PAD, 512):
        n1 = min(n0 + 512, N_IN_PAD)
        proj_ref[:, n0:n1] = jnp.dot(u, w_ref[:, n0:n1], preferred_element_type=f32).astype(bf16)


def _inproj(h, moe_out, g, w, layer):
    T = h.shape[0]
    combine = moe_out is not None
    row = lambda i: (i, 0)
    const = lambda i: (0, 0)
    if combine:
        y, gate = moe_out
        in_specs = _combine_specs(T)
        args = [h, y, y, gate]
    else:
        in_specs = [pl.BlockSpec((TM, D_MODEL), row)]
        args = [h]
    in_specs += [pl.BlockSpec((1, D_MODEL), const),
                 pl.BlockSpec((None, D_MODEL, N_IN_PAD), lambda i: (layer, 0, 0), pipeline_mode=pl.Buffered(1))]
    args += [g, w]
    out_shape = [jax.ShapeDtypeStruct((T, N_IN_PAD), bf16)]
    out_specs = [pl.BlockSpec((TM, N_IN_PAD), row)]
    if combine:
        out_shape.append(jax.ShapeDtypeStruct((T, D_MODEL), f32))
        out_specs.append(pl.BlockSpec((TM, D_MODEL), row))
    res = pl.pallas_call(
        functools.partial(_inproj_kernel, combine=combine),
        grid=(T // TM,), in_specs=in_specs, out_specs=out_specs, out_shape=out_shape,
        compiler_params=_cparams(1), name="inproj_c" if combine else "inproj")(*args)
    return (res[0], res[1]) if combine else (res[0], h)


def _pool_kernel(x_ref, w_ref, sc_ref, o_ref):
    g = pl.program_id(1)
    x = x_ref[...].astype(f32)
    t = lax.broadcasted_iota(i32, x.shape, 0)

    def lagged(v, k):
        return jnp.where(t >= k, pltpu.roll(v, k, axis=0), 0.0)

    s2 = x + lagged(x, 1)
    s4 = s2 + lagged(s2, 2)
    s8 = s4 + lagged(s4, 4)
    s16 = s8 + lagged(s8, 8)
    wsum = jnp.where(g == 0, s2, jnp.where(g == 1, s4, jnp.where(g == 2, s8, s16)))
    win = jnp.where(g == 0, 2, jnp.where(g == 1, 4, jnp.where(g == 2, 8, 16)))
    cnt = jnp.minimum(t + 1, win).astype(f32)
    d = (wsum / cnt - x).astype(bf16)
    y = jnp.dot(d, w_ref[...], preferred_element_type=f32) * sc_ref[...]
    o_ref[...] = y.astype(bf16)


def _pool(proj, w, scale, layer, B, S):
    T = B * S
    return pl.pallas_call(
        _pool_kernel, grid=(B, POOL_GROUPS),
        in_specs=[pl.BlockSpec((S, LANE), lambda b, g: (b, g)),
                  pl.BlockSpec((None, None, LANE, LANE), lambda b, g: (layer, g, 0, 0)),
                  pl.BlockSpec((1, LANE), lambda b, g: (0, g))],
        out_specs=pl.BlockSpec((S, LANE), lambda b, g: (b, g)),
        out_shape=jax.ShapeDtypeStruct((T, POOL_WIDTH), bf16),
        compiler_params=_cparams(2), name="pool")(proj, w, scale)


VT_ROWS = MLA_V + 16


def _nt_dot(a, b):
    return lax.dot_general(a, b, (((1,), (1,)), ((), ())), preferred_element_type=f32)


def _rope128(r, cos, sin):
    lane = lax.broadcasted_iota(i32, r.shape, 1)
    half = MLA_ROPE // 2
    partner = jnp.where(lane < half, -pltpu.roll(r, LANE - half, axis=1), pltpu.roll(r, half, axis=1))
    return r * cos + partner * sin


def _ones_row_pad(n_cols):
    row = lax.broadcasted_iota(i32, (VT_ROWS - MLA_V, n_cols), 0)
    return jnp.where(row == 0, 1.0, 0.0).astype(bf16)


def _mla_prep_kernel(cq_ref, ckv_ref, kr_ref, cos_ref, sin_ref, cost_ref, sint_ref, qn_ref, kvn_ref,
                     wqt_ref, wk_ref, wvt_ref, qt_ref, k_ref, vt_ref):
    scale = (MLA_NOPE + MLA_ROPE) ** -0.5 * LOG2E
    half = MLA_ROPE // 2
    cost, sint = cost_ref[...], sint_ref[...]
    cq = _rms(cq_ref[...].astype(f32), qn_ref[...]).astype(bf16)
    ckv = _rms(ckv_ref[...].astype(f32), kvn_ref[...]).astype(bf16)
    kr = _rope128(kr_ref[...].astype(f32), cos_ref[...], sin_ref[...]).astype(bf16)
    pad = _ones_row_pad(TM)
    for h in range(MLA_HEADS):
        r0 = h * QK_PAD
        qt = _nt_dot(wqt_ref[r0:r0 + QK_PAD, :], cq) * scale
        x1, x2 = qt[MLA_NOPE:MLA_NOPE + half], qt[MLA_NOPE + half:MLA_NOPE + MLA_ROPE]
        qt_ref[r0:r0 + MLA_NOPE, :] = qt[:MLA_NOPE].astype(bf16)
        qt_ref[r0 + MLA_NOPE:r0 + MLA_NOPE + half, :] = (x1 * cost - x2 * sint).astype(bf16)
        qt_ref[r0 + MLA_NOPE + half:r0 + MLA_NOPE + MLA_ROPE, :] = (x1 * sint + x2 * cost).astype(bf16)
        qt_ref[r0 + MLA_NOPE + MLA_ROPE:r0 + QK_PAD, :] = qt[MLA_NOPE + MLA_ROPE:].astype(bf16)
        kn = jnp.dot(ckv, wk_ref[:, h * LANE:(h + 1) * LANE], preferred_element_type=f32)
        k_ref[:, r0:r0 + LANE] = kn.astype(bf16)
        k_ref[:, r0 + LANE:r0 + QK_PAD] = kr
        vt = _nt_dot(wvt_ref[h * MLA_V:(h + 1) * MLA_V, :], ckv)
        vt_ref[h, 0:MLA_V, :] = vt.astype(bf16)
        vt_ref[h, MLA_V:VT_ROWS, :] = pad


def _mla_prep(proj, cos, sin, cost, sint, qn, kvn, wqt, wk, wvt, layer, B, S):
    T = B * S
    assert TM == TK
    nk = S // TK
    const = lambda i: (0, 0)
    per_layer = lambda i: (layer, 0, 0)
    nq, nkc = MLA_HEADS * QK_PAD, MLA_HEADS * LANE
    return pl.pallas_call(
        _mla_prep_kernel, grid=(T // TM,),
        in_specs=[pl.BlockSpec((TM, MLA_Q_RANK), lambda i: (i, COL_CQ // MLA_Q_RANK)),
                  pl.BlockSpec((TM, MLA_KV_RANK), lambda i: (i, COL_CKV // MLA_KV_RANK)),
                  pl.BlockSpec((TM, LANE), lambda i: (i, COL_KR // LANE)),
                  pl.BlockSpec((TM, LANE), lambda i: (i, 0)),
                  pl.BlockSpec((TM, LANE), lambda i: (i, 0)),
                  pl.BlockSpec((MLA_ROPE // 2, TM), lambda i: (0, i)),
                  pl.BlockSpec((MLA_ROPE // 2, TM), lambda i: (0, i)),
                  pl.BlockSpec((1, MLA_Q_RANK), const),
                  pl.BlockSpec((1, MLA_KV_RANK), const),
                  pl.BlockSpec((None, nq, MLA_Q_RANK), per_layer),
                  pl.BlockSpec((None, MLA_KV_RANK, nkc), per_layer),
                  pl.BlockSpec((None, MLA_HEADS * MLA_V, MLA_KV_RANK), per_layer)],
        out_specs=[pl.BlockSpec((nq, TM), lambda i: (0, i)),
                   pl.BlockSpec((TM, nq), lambda i: (i, 0)),
                   pl.BlockSpec((None, MLA_HEADS, None, VT_ROWS, TK), lambda i: (i // nk, 0, i % nk, 0, 0))],
        out_shape=[jax.ShapeDtypeStruct((nq, T), bf16), jax.ShapeDtypeStruct((T, nq), bf16),
                   jax.ShapeDtypeStruct((B, MLA_HEADS, nk, VT_ROWS, TK), bf16)],
        compiler_params=_cparams(1), name="mla_prep")(proj, proj, proj, cos, sin, cost, sint, qn, kvn, wqt, wk, wvt)


def _softmax_probs(s, m, shift, lo=0):
    m_old = m[:, lo:]
    m_new = jnp.maximum(m_old, jnp.max(s, axis=0, keepdims=True) + shift)
    alpha = jnp.exp2(m_old - m_new)
    p = jnp.exp2(s - (m_new - shift)).astype(bf16)
    return (m_new if lo == 0 else jnp.concatenate([m[:, :lo], m_new], axis=1)), alpha, p


def _accumulate(acc_ref, idx, alpha, vt, p, lo=0):
    acc_ref[idx, :, lo:] = alpha * acc_ref[idx, :, lo:] + jnp.dot(vt, p, preferred_element_type=f32)


def _attn_out_t(acc):
    return acc[0:MLA_V] * (1.0 / acc[MLA_V:MLA_V + 1])


def _diag_visible(lo):
    krow = lax.broadcasted_iota(i32, (TK, TQ - lo), 0)
    qcol = lax.broadcasted_iota(i32, (TK, TQ - lo), 1)
    return krow <= qcol


MLA_HEADS_PER_STEP = 2


def _mla_attn_kernel(qt_ref, k_ref, vt_ref, o_ref, acc_ref, p_ref):
    i = pl.program_id(2)
    acc_ref[...] = jnp.zeros_like(acc_ref)
    p_ref[...] = jnp.zeros_like(p_ref)
    n_full = i * (TQ // TK)
    heads = range(MLA_HEADS_PER_STEP)

    def scores(c, j, lo=0):
        k = k_ref[pl.ds(pl.multiple_of(j * TK, TK), TK), c * QK_PAD:(c + 1) * QK_PAD]
        return jnp.dot(k, qt_ref[c * QK_PAD:(c + 1) * QK_PAD, lo:], preferred_element_type=f32)

    def flush(alphas, j):
        for c in heads:
            _accumulate(acc_ref, c, alphas[c], vt_ref[c, j], p_ref[c])

    def full(j, carry):
        ms, alphas = carry
        flush(alphas, jnp.maximum(j - 1, 0))
        new = [_softmax_probs(scores(c, j), ms[c], 0.0) for c in heads]
        for c in heads:
            p_ref[c] = new[c][2]
        return tuple(n[0] for n in new), tuple(n[1] for n in new)

    minf = jnp.full((1, TQ), -jnp.inf, f32)
    one = jnp.ones((1, TQ), f32)
    ms, alphas = lax.fori_loop(0, n_full, full, ((minf,) * MLA_HEADS_PER_STEP, (one,) * MLA_HEADS_PER_STEP))
    flush(alphas, jnp.maximum(n_full - 1, 0))
    for d in range(TQ // TK):
        j, lo = n_full + d, d * TK
        for c in heads:
            m, alpha, p = _softmax_probs(jnp.where(_diag_visible(lo), scores(c, j, lo), -jnp.inf), ms[c], 0.0, lo)
            _accumulate(acc_ref, c, alpha, vt_ref[c, j], p, lo)
            ms = ms[:c] + (m,) + ms[c + 1:]
    for c in heads:
        o_ref[:, c * MLA_V:(c + 1) * MLA_V] = _attn_out_t(acc_ref[c]).T.astype(bf16)


def _mla_attn(qt, k, vt, B, S):
    T = B * S
    nq, nk = S // TQ, S // TK
    hs = MLA_HEADS_PER_STEP
    return pl.pallas_call(
        _mla_attn_kernel, grid=(B, MLA_HEADS // hs, nq),
        in_specs=[pl.BlockSpec((hs * QK_PAD, TQ), lambda b, h, i: (h, b * nq + i)),
                  pl.BlockSpec((S, hs * QK_PAD), lambda b, h, i: (b, h)),
                  pl.BlockSpec((None, hs, nk, VT_ROWS, TK), lambda b, h, i: (b, h, 0, 0, 0))],
        out_specs=pl.BlockSpec((TQ, hs * MLA_V), lambda b, h, i: (b * nq + i, h)),
        out_shape=jax.ShapeDtypeStruct((T, MLA_HEADS * MLA_V), bf16),
        scratch_shapes=[pltpu.VMEM((hs, VT_ROWS, TQ), f32), pltpu.VMEM((hs, TK, TQ), bf16)],
        compiler_params=_cparams(3), name="mla_attn")(qt, k, vt)


def _diff_attn_kernel(pmin_ref, kmax_ref, lam_ref, q_ref, k_ref, v_ref, posq_ref, posk_ref, tab_ref, sub_ref, o_ref,
                      vt_ref, acc_ref, p_ref, *, lam_init, nq, nk):
    b, h, i = pl.program_id(0), pl.program_id(1), pl.program_id(2)

    @pl.when(i == 0)
    def _():
        pad = _ones_row_pad(TK)
        for j in range(nk):
            vt_ref[j, 0:DIFF_V, :] = v_ref[j * TK:(j + 1) * TK, :].astype(f32).T.astype(bf16)
            vt_ref[j, DIFF_V:VT_ROWS, :] = pad

    lv = lam_ref[...]
    lam = (jnp.exp(jnp.sum(lv[0:1] * lv[1:2], axis=1, keepdims=True))
           - jnp.exp(jnp.sum(lv[2:3] * lv[3:4], axis=1, keepdims=True)) + lam_init)
    qt = q_ref[...].astype(f32).T
    frow = lax.broadcasted_iota(i32, qt.shape, 0)
    qm = (jnp.where(frow < DIFF_QK, qt, 0.0).astype(bf16), jnp.where(frow >= DIFF_QK, qt, 0.0).astype(bf16))
    acc_ref[...] = jnp.zeros_like(acc_ref)
    posq = posq_ref[...]
    tab = jnp.broadcast_to(tab_ref[pl.ds(h, 1), :], (TK, LANE))
    far_bias = tab_ref[pl.ds(h, 1), REL_CLIP:REL_CLIP + 1]
    n_full = i * (TQ // TK)

    def scores(j, lo=0):
        off = pl.multiple_of(j * TK, TK)
        q_both = jnp.concatenate([qm[0][:, lo:], qm[1][:, lo:]], axis=1)
        return jnp.dot(k_ref[pl.ds(off, TK), :], q_both, preferred_element_type=f32), off

    def bias_tile(off, lo):
        n = jnp.clip(posq[:, lo:] - posk_ref[pl.ds(off, TK), :], 0, REL_CLIP)
        return jnp.concatenate(
            [jnp.take_along_axis(tab, n[:, c * LANE:(c + 1) * LANE], axis=1) for c in range((TQ - lo) // LANE)],
            axis=1)

    def flush(alphas, j):
        for c in range(2):
            _accumulate(acc_ref, c, alphas[c], vt_ref[j], p_ref[c])

    def softmax_tile(j, ms, far_tile, lo=0, visible=None):
        s2, off = scores(j, lo)
        bias = None if far_tile else bias_tile(off, lo)
        out = []
        for c in range(2):
            s = s2[:, c * (TQ - lo):(c + 1) * (TQ - lo)]
            if not far_tile:
                s = s + bias
            if visible is not None:
                s = jnp.where(visible, s, -jnp.inf)
            out.append(_softmax_probs(s, ms[c], far_bias if far_tile else 0.0, lo))
        return out

    def below_diagonal(far_tile, j, carry):
        ms, alphas = carry
        flush(alphas, jnp.maximum(j - 1, 0))
        new = softmax_tile(j, ms, far_tile)
        for c in range(2):
            p_ref[c] = new[c][2]
        return tuple(n[0] for n in new), tuple(n[1] for n in new)

    def body(j, carry):
        is_far = pmin_ref[b * nq + i] - kmax_ref[b * nk + j] >= REL_FAR
        return lax.cond(is_far, functools.partial(below_diagonal, True), functools.partial(below_diagonal, False),
                        j, carry)

    p_ref[...] = jnp.zeros_like(p_ref)
    minf = jnp.full((1, TQ), -jnp.inf, f32)
    one = jnp.ones((1, TQ), f32)
    ms, alphas = lax.fori_loop(0, n_full, body, ((minf, minf), (one, one)))
    flush(alphas, jnp.maximum(n_full - 1, 0))
    for d in range(TQ // TK):
        j, lo = n_full + d, d * TK
        new = softmax_tile(j, ms, False, lo, _diag_visible(lo))
        for c in range(2):
            _accumulate(acc_ref, c, new[c][1], vt_ref[j], new[c][2], lo)
        ms = tuple(n[0] for n in new)
    ot = _attn_out_t(acc_ref[0]) - lam * _attn_out_t(acc_ref[1])
    ot = ot * lax.rsqrt(jnp.mean(ot * ot, axis=0, keepdims=True) + NORM_EPS) * (sub_ref[...] * (1.0 - lam_init))
    o_ref[...] = ot.T.astype(bf16)


def _diff_attn(proj, lam_vecs, positions, tab, subln, lam_init, B, S):
    T = B * S
    nq, nk = S // TQ, S // TK
    cq, ck, cv = COL_DQ // LANE, COL_DK // LANE, COL_DV // LANE
    pmin = positions.reshape(B * nq, TQ).min(axis=1)
    kmax = positions.reshape(B * nk, TK).max(axis=1)
    grid_spec = pltpu.PrefetchScalarGridSpec(
        num_scalar_prefetch=2, grid=(B, DIFF_HEADS, nq),
        in_specs=[pl.BlockSpec((4, DIFF_QK), lambda b, h, i, *_: (0, 0)),
                  pl.BlockSpec((TQ, LANE), lambda b, h, i, *_: (b * nq + i, cq + h)),
                  pl.BlockSpec((S, LANE), lambda b, h, i, *_: (b, ck + h)),
                  pl.BlockSpec((S, LANE), lambda b, h, i, *_: (b, cv + h)),
                  pl.BlockSpec((1, TQ), lambda b, h, i, *_: (0, b * nq + i)),
                  pl.BlockSpec((S, 1), lambda b, h, i, *_: (b, 0)),
                  pl.BlockSpec((8, LANE), lambda b, h, i, *_: (0, 0)),
                  pl.BlockSpec((DIFF_V, 1), lambda b, h, i, *_: (0, 0))],
        out_specs=pl.BlockSpec((TQ, DIFF_V), lambda b, h, i, *_: (b * nq + i, h)),
        scratch_shapes=[pltpu.VMEM((nk, VT_ROWS, TK), bf16), pltpu.VMEM((2, VT_ROWS, TQ), f32),
                        pltpu.VMEM((2, TK, TQ), bf16)])
    return pl.pallas_call(
        functools.partial(_diff_attn_kernel, lam_init=lam_init, nq=nq, nk=nk), grid_spec=grid_spec,
        out_shape=jax.ShapeDtypeStruct((T, DIFF_HEADS * DIFF_V), bf16),
        compiler_params=_cparams(3), name="diff_attn")(
            pmin, kmax, lam_vecs, proj, proj, proj, positions.reshape(1, T), positions.reshape(T, 1), tab, subln)


ROUTER_ROWS = 64


def _outproj_router_kernel(a_ref, b_ref, c_ref, wo_ref, h_ref, g_ref, wr_ref, br_ref,
                           h2_ref, xn_ref, eid_ref, gate_ref):
    na, nb = a_ref.shape[1], b_ref.shape[1]
    mixed = (jnp.dot(a_ref[...], wo_ref[0:na, :], preferred_element_type=f32)
             + jnp.dot(b_ref[...], wo_ref[na:na + nb, :], preferred_element_type=f32)
             + jnp.dot(c_ref[...], wo_ref[na + nb:, :], preferred_element_type=f32))
    h2 = h_ref[...] + mixed
    h2_ref[...] = h2
    xn = _rms(h2, g_ref[...])
    _to_slab(xn_ref, xn)
    xh = xn.astype(bf16)
    xl = (xn - xh.astype(f32)).astype(bf16)
    wr = wr_ref[...]
    ah = jnp.dot(xh, wr, preferred_element_type=f32)
    al = jnp.dot(xl, wr, preferred_element_type=f32)
    logit = (ah + pltpu.roll(ah, ROUTER_ROWS, axis=1) + al).T[:ROUTER_ROWS] + br_ref[...]

    iota8 = lax.broadcasted_iota(i32, (8, TM), 0).astype(f32)
    gl = jnp.where(iota8 < N_GROUPS, logit[0:8], -jnp.inf)
    gmax = jnp.max(gl, axis=0, keepdims=True)
    gidx = jnp.min(jnp.where(gl == gmax, iota8, 8.0), axis=0, keepdims=True)
    ggate = 1.0 / jnp.sum(jnp.exp(gl - gmax), axis=0, keepdims=True)

    e = jnp.zeros((EXPERTS_PER_GROUP, TM), f32)
    for g in range(N_GROUPS):
        e = jnp.where(gidx == g, logit[8 + 8 * g:16 + 8 * g], e)
    m1 = jnp.max(e, axis=0, keepdims=True)
    i1 = jnp.min(jnp.where(e == m1, iota8, 8.0), axis=0, keepdims=True)
    e2 = jnp.where(iota8 == i1, -jnp.inf, e)
    m2 = jnp.max(e2, axis=0, keepdims=True)
    i2 = jnp.min(jnp.where(e2 == m2, iota8, 8.0), axis=0, keepdims=True)
    r = jnp.exp(m2 - m1)
    p1 = 1.0 / (1.0 + r)
    p2 = r / (1.0 + r)
    eid_ref[0:1, :] = (gidx * EXPERTS_PER_GROUP + i1).astype(i32)
    eid_ref[1:2, :] = (gidx * EXPERTS_PER_GROUP + i2).astype(i32)
    gate_ref[0:1, :] = ggate * p1
    gate_ref[1:2, :] = ggate * p2


def _outproj_router(a, b, c, wo, h, g, wr, br, layer):
    T = h.shape[0]
    row = lambda i: (i, 0)
    const = lambda i: (0, 0)
    per_layer = lambda i: (layer, 0, 0)
    na, nb, nc = a.shape[1], b.shape[1], c.shape[1]
    return pl.pallas_call(
        _outproj_router_kernel, grid=(T // TM,),
        in_specs=[pl.BlockSpec((TM, na), row), pl.BlockSpec((TM, nb), row), pl.BlockSpec((TM, nc), row),
                  pl.BlockSpec((None, na + nb + nc, D_MODEL), per_layer, pipeline_mode=pl.Buffered(1)),
                  pl.BlockSpec((TM, D_MODEL), row),
                  pl.BlockSpec((1, D_MODEL), const),
                  pl.BlockSpec((None, D_MODEL, 2 * ROUTER_ROWS), per_layer),
                  pl.BlockSpec((None, ROUTER_ROWS, 1), per_layer)],
        out_specs=[pl.BlockSpec((TM, D_MODEL), row), pl.BlockSpec((TM * SLAB_ROWS, LANE), row),
                   pl.BlockSpec((TOP_K, TM), lambda i: (0, i)), pl.BlockSpec((TOP_K, TM), lambda i: (0, i))],
        out_shape=[jax.ShapeDtypeStruct((T, D_MODEL), f32), jax.ShapeDtypeStruct((T * SLAB_ROWS, LANE), f32),
                   jax.ShapeDtypeStruct((TOP_K, T), i32), jax.ShapeDtypeStruct((TOP_K, T), f32)],
        compiler_params=_cparams(1), name="outproj_router")(a, b, c, wo, h, g, wr, br)


def _moe_kernel(src_ref, dst_ref, cstart_ref, nvalid_ref, be_ref, wslot_ref, enext_ref, nused_ref,
                x_hbm, wg_hbm, wu_hbm, wd_hbm, y_hbm, xbuf, ybuf, wgbuf, wubuf, wdbuf, gsem, ssem, wsem, *, layer):
    b = pl.program_id(0)
    slot = b % 2
    n_used = nused_ref[0]

    def weight_copies(e, buf):
        return [pltpu.make_async_copy(hbm.at[layer, e], vbuf.at[buf], wsem.at[buf])
                for hbm, vbuf in ((wg_hbm, wgbuf), (wu_hbm, wubuf), (wd_hbm, wdbuf))]

    def gather_row(base, buf, r):
        return pltpu.make_async_copy(x_hbm.at[pl.ds(src_ref[base + r] * SLAB_ROWS, SLAB_ROWS)],
                                     xbuf.at[buf, pl.ds(r * SLAB_ROWS, SLAB_ROWS)], gsem.at[buf])

    def scatter_row(base, buf, r):
        return pltpu.make_async_copy(ybuf.at[buf, pl.ds(r * SLAB_ROWS, SLAB_ROWS)],
                                     y_hbm.at[pl.ds(dst_ref[base + r] * SLAB_ROWS, SLAB_ROWS)], ssem.at[buf])

    def start_group(make_row, base, buf, g):
        for u in range(8):
            make_row(base, buf, 8 * g + u).start(priority=u % 2)

    def start_gather(blk, buf):
        base = cstart_ref[blk]

        def group(g, carry):
            start_group(gather_row, base, buf, g)
            return carry
        lax.fori_loop(0, MOE_BLK // 8, group, 0)

    def start_scatter(blk, buf):
        base, nv = cstart_ref[blk], nvalid_ref[blk]

        def group(g, carry):
            start_group(scatter_row, base, buf, g)
            return carry
        lax.fori_loop(0, nv >> 3, group, 0)

        def single(r, carry):
            scatter_row(base, buf, r).start(priority=0)
            return carry
        lax.fori_loop(nv & ~7, nv, single, 0)

    def wait_gather(buf):
        pltpu.make_async_copy(x_hbm.at[pl.ds(0, MOE_BLK * SLAB_ROWS)], xbuf.at[buf], gsem.at[buf]).wait()

    def wait_scatter(blk, buf):
        nv = nvalid_ref[blk]
        rows = MOE_BLK
        while rows >= 1:
            @pl.when((nv & rows) != 0)
            def _(rows=rows):
                n = rows * SLAB_ROWS
                pltpu.make_async_copy(ybuf.at[buf, pl.ds(0, n)], y_hbm.at[pl.ds(0, n)], ssem.at[buf]).wait()
            rows //= 2

    def per_slot(fn):
        for buf in range(2):
            pl.when(slot == buf)(functools.partial(fn, buf))

    @pl.when(b == 0)
    def _():
        start_gather(0, 0)
        for cp in weight_copies(be_ref[0], wslot_ref[0]):
            cp.start()

    ws = wslot_ref[b]
    first_of_expert = jnp.logical_or(b == 0, be_ref[jnp.maximum(b - 1, 0)] != be_ref[b])

    @pl.when(jnp.logical_and(b < n_used, first_of_expert))
    def _():
        @pl.when(enext_ref[b] >= 0)
        def _():
            for cp in weight_copies(enext_ref[b], 1 - ws):
                cp.start()
        for cp in weight_copies(be_ref[b], ws):
            cp.wait()

    @pl.when(b + 1 < n_used)
    def _():
        per_slot(lambda buf: start_gather(b + 1, 1 - buf))

    @pl.when(b < n_used)
    def _():
        wait_gather(slot)

        @pl.when(b >= 2)
        def _():
            wait_scatter(b - 2, slot)

        x = _unslab(xbuf.at[slot], MOE_BLK)
        hg = jnp.dot(x, wgbuf[ws], preferred_element_type=f32)
        hu = jnp.dot(x, wubuf[ws], preferred_element_type=f32)
        hid = hg * jax.nn.sigmoid(hg) * hu
        _to_slab(ybuf.at[slot], jnp.dot(hid, wdbuf[ws], preferred_element_type=f32))
        per_slot(lambda buf: start_scatter(b, buf))

    @pl.when(b == n_used - 1)
    def _():
        wait_scatter(b, slot)

        @pl.when(b >= 1)
        def _():
            wait_scatter(b - 1, 1 - slot)


def _moe(xn, tables, w_gate, w_up, w_down, layer):
    T = xn.shape[0] // SLAB_ROWS
    n_blocks = tables[4].shape[0]
    anywhere = pl.BlockSpec(memory_space=pl.ANY)
    grid_spec = pltpu.PrefetchScalarGridSpec(
        num_scalar_prefetch=len(tables), grid=(n_blocks,),
        in_specs=[anywhere, anywhere, anywhere, anywhere],
        out_specs=anywhere,
        scratch_shapes=[pltpu.VMEM((2, MOE_BLK * SLAB_ROWS, LANE), f32), pltpu.VMEM((2, MOE_BLK * SLAB_ROWS, LANE), f32),
                        pltpu.VMEM((2, D_MODEL, EXPERT_HIDDEN), f32), pltpu.VMEM((2, D_MODEL, EXPERT_HIDDEN), f32),
                        pltpu.VMEM((2, EXPERT_HIDDEN, D_MODEL), f32),
                        pltpu.SemaphoreType.DMA((2,)), pltpu.SemaphoreType.DMA((2,)), pltpu.SemaphoreType.DMA((2,))])
    return pl.pallas_call(
        functools.partial(_moe_kernel, layer=layer), grid_spec=grid_spec,
        out_shape=jax.ShapeDtypeStruct((TOP_K * T * SLAB_ROWS, LANE), f32),
        compiler_params=_cparams(1), name="moe")(*tables, xn, w_gate, w_up, w_down)


def _dispatch(eid, T):
    n_assign = TOP_K * T
    n_blocks = n_assign // MOE_BLK + N_EXPERTS
    e_flat = eid.T.reshape(-1)
    keys = jnp.sort(e_flat * n_assign + jnp.arange(n_assign, dtype=i32))
    asg = jnp.concatenate([keys % n_assign, jnp.zeros((MOE_BLK,), i32)])
    src, dst = asg // TOP_K, (asg % TOP_K) * T + asg // TOP_K
    edges = jnp.arange(N_EXPERTS + 1, dtype=i32) * n_assign
    bounds = jnp.sum(keys[None, :] < edges[:, None], axis=1, dtype=i32)
    start, counts = bounds[:-1], bounds[1:] - bounds[:-1]
    nblk = (counts + MOE_BLK - 1) // MOE_BLK
    bend = jnp.cumsum(nblk)
    n_used = bend[-1]
    b = jnp.arange(n_blocks, dtype=i32)
    be = jnp.minimum(jnp.sum(bend[None, :] <= jnp.minimum(b, n_used - 1)[:, None], axis=1, dtype=i32), N_EXPERTS - 1)
    j = b - (bend - nblk)[be]
    used = b < n_used
    cstart = jnp.where(used, start[be] + j * MOE_BLK, 0).astype(i32)
    nvalid = jnp.where(used, jnp.clip(counts[be] - j * MOE_BLK, 0, MOE_BLK), 0).astype(i32)
    used_e = counts > 0
    rank = jnp.cumsum(used_e.astype(i32)) - 1
    ids = jnp.arange(N_EXPERTS, dtype=i32)
    later = jnp.where(used_e[None, :] & (ids[None, :] > ids[:, None]), ids[None, :], N_EXPERTS)
    nxt = jnp.min(later, axis=1)
    enext = jnp.where(nxt < N_EXPERTS, nxt, -1).astype(i32)
    return src, dst, cstart, nvalid, be, (rank[be] % 2).astype(i32), enext[be], n_used.reshape(1).astype(i32)


def _final_kernel(h_ref, y0_ref, y1_ref, gate_ref, g_ref, o_ref):
    o_ref[...] = _rms(_combine(h_ref, y0_ref, y1_ref, gate_ref), g_ref[...])


def _final(h, moe_out, g):
    T = h.shape[0]
    y, gate = moe_out
    return pl.pallas_call(
        _final_kernel, grid=(T // TM,),
        in_specs=_combine_specs(T) + [pl.BlockSpec((1, D_MODEL), lambda i: (0, 0))],
        out_specs=pl.BlockSpec((TM, D_MODEL), lambda i: (i, 0)),
        out_shape=jax.ShapeDtypeStruct((T, D_MODEL), f32),
        compiler_params=_cparams(1), name="final_norm")(h, y, y, gate, g)


def _t5_bucket(n):
    nf = jnp.maximum(n, 1).astype(f32)
    large = REL_MAX_EXACT + (jnp.log(nf / REL_MAX_EXACT) / math.log(REL_MAX_DISTANCE / REL_MAX_EXACT)
                             * (REL_BUCKETS - REL_MAX_EXACT)).astype(i32)
    return jnp.where(n < REL_MAX_EXACT, n, jnp.minimum(large, REL_BUCKETS - 1))


def _split_hi_lo(w):
    hi = w.astype(bf16)
    return hi, (w - hi.astype(f32)).astype(bf16)


def kernel(x, positions, attn_norm, w_in, pool_w, pool_scale, mla_q_norm, mla_w_uq, mla_kv_norm, mla_w_ukv,
           diff_lambda, diff_subln, rel_bias, w_out, ffn_norm, router_group_w, router_group_b,
           router_expert_w, router_expert_b, expert_w_gate, expert_w_up, expert_w_down, final_norm):
    B, S, D = x.shape
    T = B * S
    depth = w_in.shape[0]
    assert D == D_MODEL and S % TQ == 0 and T % TM == 0 and (TOP_K * T) % MOE_BLK == 0

    inv = 1.0 / (ROPE_THETA ** (jnp.arange(0, MLA_ROPE, 2, dtype=f32) / MLA_ROPE))
    ang = positions.astype(f32).reshape(T, 1) * inv
    zeros = jnp.zeros((T, LANE - MLA_ROPE), f32)
    cos = jnp.concatenate([jnp.cos(ang), jnp.cos(ang), zeros], axis=1)
    sin = jnp.concatenate([jnp.sin(ang), jnp.sin(ang), zeros], axis=1)
    cost, sint = jnp.cos(ang).T, jnp.sin(ang).T
    assert REL_CLIP == LANE - 1
    tab = jnp.pad(rel_bias[_t5_bucket(jnp.arange(LANE, dtype=i32))].T.astype(f32) * LOG2E,
                  ((0, 8 - DIFF_HEADS), (0, 0)))

    c_kr, c_dq = COL_KR + MLA_ROPE, COL_KR + MLA_ROPE + COL_DK - COL_DQ
    w_in_p = jnp.concatenate([w_in[:, :, :c_kr], jnp.zeros((depth, D, LANE - MLA_ROPE), f32),
                              w_in[:, :, c_kr:c_dq] * (DIFF_QK ** -0.5 * LOG2E), w_in[:, :, c_dq:]],
                             axis=2).astype(bf16)
    wq = mla_w_uq.reshape(depth, MLA_Q_RANK, MLA_HEADS, MLA_NOPE + MLA_ROPE)
    wq = jnp.pad(wq, ((0, 0), (0, 0), (0, 0), (0, QK_PAD - MLA_NOPE - MLA_ROPE)))
    wqt = wq.reshape(depth, MLA_Q_RANK, -1).transpose(0, 2, 1).astype(bf16)
    wkv = mla_w_ukv.reshape(depth, MLA_KV_RANK, MLA_HEADS, MLA_NOPE + MLA_V)
    wk = wkv[..., :MLA_NOPE].reshape(depth, MLA_KV_RANK, -1).astype(bf16)
    wvt = wkv[..., MLA_NOPE:].reshape(depth, MLA_KV_RANK, -1).transpose(0, 2, 1).astype(bf16)
    wo = w_out.astype(bf16)
    pool_wb = pool_w.astype(bf16)
    pad_g = jnp.zeros((depth, D, 8 - N_GROUPS), f32)
    pad_e = jnp.zeros((depth, D, ROUTER_ROWS - 8 - N_EXPERTS), f32)
    wr_hi, wr_lo = _split_hi_lo(jnp.concatenate([router_group_w, pad_g, router_expert_w, pad_e], axis=2))
    wr_cat = jnp.concatenate([wr_hi, wr_lo], axis=2)
    br = jnp.concatenate([router_group_b, pad_g[:, 0], router_expert_b, pad_e[:, 0]], axis=1)[:, :, None]

    h = x.reshape(T, D)
    moe_out = None
    for l in range(depth):
        lam_init = 0.8 - 0.6 * math.exp(-0.3 * l)
        proj, h = _inproj(h, moe_out, attn_norm[l].reshape(1, D), w_in_p, l)
        a_out = _pool(proj, pool_wb, pool_scale[l].reshape(1, POOL_WIDTH), l, B, S)
        qt, k, vt = _mla_prep(proj, cos, sin, cost, sint, mla_q_norm[l].reshape(1, -1), mla_kv_norm[l].reshape(1, -1),
                              wqt, wk, wvt, l, B, S)
        b_out = _mla_attn(qt, k, vt, B, S)
        c_out = _diff_attn(proj, diff_lambda[l], positions, tab, diff_subln[l].reshape(-1, 1), lam_init, B, S)
        h, xn, eid, gate = _outproj_router(a_out, b_out, c_out, wo, h, ffn_norm[l].reshape(1, D), wr_cat, br, l)
        y = _moe(xn, _dispatch(eid, T), expert_w_gate, expert_w_up, expert_w_down, l)
        moe_out = (y, gate.T)
    out = _final(h, moe_out, final_norm.reshape(1, D))
    return out.reshape(B, S, D)
```

```python
import functools
import math

import jax
import jax.numpy as jnp
from jax import lax
from jax.experimental import pallas as pl
from jax.experimental.pallas import tpu as pltpu

f32, bf16, i32 = jnp.float32, jnp.bfloat16, jnp.int32

D_MODEL = 2048
POOL_WINDOWS = (2, 4, 8, 16)
POOL_GROUPS = 4
POOL_WIDTH = 512
MLA_NOPE, MLA_ROPE, MLA_V, MLA_HEADS = 128, 64, 128, 6
MLA_Q_RANK, MLA_KV_RANK = 512, 256
ROPE_THETA = 10000.0
DIFF_QK, DIFF_V, DIFF_HEADS = 64, 128, 6
REL_BUCKETS, REL_MAX_EXACT, REL_MAX_DISTANCE = 32, 16, 128
N_GROUPS, EXPERTS_PER_GROUP, N_EXPERTS, TOP_K, EXPERT_HIDDEN = 4, 8, 32, 2, 512
NORM_EPS = 1e-6

LANE = 128
MXU_DIM = 256
VMEM_LIMIT = 56 * 1024 * 1024

N_IN_PAD = 3712
COL_CQ, COL_CKV, COL_KR, COL_DQ, COL_DK, COL_DV = 512, 1024, 1280, 1408, 2176, 2944
QK_PAD = 2 * LANE

TM = 256
TQ = 512
TK = 256
MOE_BLK = 256
SLAB_ROWS = D_MODEL // LANE
REL_CLIP = REL_MAX_DISTANCE - 1
REL_FAR = REL_MAX_DISTANCE
LOG2E = math.log2(math.e)


def _cparams(n_axes):
    return pltpu.CompilerParams(dimension_semantics=("arbitrary",) * n_axes, vmem_limit_bytes=VMEM_LIMIT)


def _rms(x, g):
    return x * lax.rsqrt(jnp.mean(x * x, axis=-1, keepdims=True) + NORM_EPS) * g


def _unslab(ref, rows, per_token=SLAB_ROWS):
    return jnp.concatenate([ref[pl.ds(s, rows, stride=per_token), :] for s in range(per_token)], axis=1)


def _to_slab(ref, val):
    per_token = val.shape[1] // LANE
    for s in range(per_token):
        ref[pl.ds(s, val.shape[0], stride=per_token), :] = val[:, s * LANE:(s + 1) * LANE]


XPACK_ROWS = SLAB_ROWS // 2


def _pack_bf16_pairs(x):
    bits = lax.bitcast_convert_type(x.astype(bf16).astype(f32), jnp.uint32)
    half = x.shape[1] // 2
    return bits[:, half:] | (bits[:, :half] >> 16)


def _unpack_bf16_pairs(w):
    lo = lax.bitcast_convert_type(w << 16, f32)
    hi = lax.bitcast_convert_type(w & jnp.uint32(0xFFFF0000), f32)
    return jnp.concatenate([lo, hi], axis=1)


def _combine(h_ref, y0_ref, y1_ref, gate_ref):
    gate = gate_ref[...]
    return h_ref[...] + _unslab(y0_ref, TM) * gate[:, 0:1] + _unslab(y1_ref, TM) * gate[:, 1:2]


def _combine_specs(T):
    plane1 = T // TM
    return [pl.BlockSpec((TM, D_MODEL), lambda i: (i, 0)),
            pl.BlockSpec((TM * SLAB_ROWS, LANE), lambda i: (i, 0)),
            pl.BlockSpec((TM * SLAB_ROWS, LANE), lambda i: (plane1 + i, 0)),
            pl.BlockSpec((TM, TOP_K), lambda i: (i, 0))]


def _inproj_kernel(*refs, combine):
    if combine:
        h_ref, y0_ref, y1_ref, gate_ref, g_ref, w_ref, proj_ref, hout_ref = refs
        h = _combine(h_ref, y0_ref, y1_ref, gate_ref)
        hout_ref[...] = h
    else:
        h_ref, g_ref, w_ref, proj_ref = refs
        h = h_ref[...]
    u = _rms(h, g_ref[...]).astype(bf16)
    for n0 in range(0, N_IN_PAD, 512):
        n1 = min(n0 + 512, N_IN_PAD)
        proj_ref[:, n0:n1] = jnp.dot(u, w_ref[:, n0:n1], preferred_element_type=f32).astype(bf16)


def _inproj(h, moe_out, g, w, layer):
    T = h.shape[0]
    combine = moe_out is not None
    row = lambda i: (i, 0)
    const = lambda i: (0, 0)
    if combine:
        y, gate = moe_out
        in_specs = _combine_specs(T)
        args = [h, y, y, gate]
    else:
        in_specs = [pl.BlockSpec((TM, D_MODEL), row)]
        args = [h]
    in_specs += [pl.BlockSpec((1, D_MODEL), const),
                 pl.BlockSpec((None, D_MODEL, N_IN_PAD), lambda i: (layer, 0, 0), pipeline_mode=pl.Buffered(1))]
    args += [g, w]
    out_shape = [jax.ShapeDtypeStruct((T, N_IN_PAD), bf16)]
    out_specs = [pl.BlockSpec((TM, N_IN_PAD), row)]
    if combine:
        out_shape.append(jax.ShapeDtypeStruct((T, D_MODEL), f32))
        out_specs.append(pl.BlockSpec((TM, D_MODEL), row))
    res = pl.pallas_call(
        functools.partial(_inproj_kernel, combine=combine),
        grid=(T // TM,), in_specs=in_specs, out_specs=out_specs, out_shape=out_shape,
        compiler_params=_cparams(1), name="inproj_c" if combine else "inproj")(*args)
    return (res[0], res[1]) if combine else (res[0], h)


def _pool_kernel(x_ref, w_ref, sc_ref, o_ref):
    g = pl.program_id(1)
    x = x_ref[...].astype(f32)
    t = lax.broadcasted_iota(i32, x.shape, 0)

    def lagged(v, k):
        return jnp.where(t >= k, pltpu.roll(v, k, axis=0), 0.0)

    s2 = x + lagged(x, 1)
    s4 = s2 + lagged(s2, 2)
    s8 = s4 + lagged(s4, 4)
    s16 = s8 + lagged(s8, 8)
    wsum = jnp.where(g == 0, s2, jnp.where(g == 1, s4, jnp.where(g == 2, s8, s16)))
    win = jnp.where(g == 0, 2, jnp.where(g == 1, 4, jnp.where(g == 2, 8, 16)))
    cnt = jnp.minimum(t + 1, win).astype(f32)
    d = (wsum / cnt - x).astype(bf16)
    y = jnp.dot(d, w_ref[...], preferred_element_type=f32) * sc_ref[...]
    o_ref[...] = y.astype(bf16)


def _pool(proj, w, scale, layer, B, S):
    T = B * S
    return pl.pallas_call(
        _pool_kernel, grid=(B, POOL_GROUPS),
        in_specs=[pl.BlockSpec((S, LANE), lambda b, g: (b, g)),
                  pl.BlockSpec((None, None, LANE, LANE), lambda b, g: (layer, g, 0, 0)),
                  pl.BlockSpec((1, LANE), lambda b, g: (0, g))],
        out_specs=pl.BlockSpec((S, LANE), lambda b, g: (b, g)),
        out_shape=jax.ShapeDtypeStruct((T, POOL_WIDTH), bf16),
        compiler_params=_cparams(2), name="pool")(proj, w, scale)


VT_ROWS = MLA_V + 16


def _nt_dot(a, b):
    return lax.dot_general(a, b, (((1,), (1,)), ((), ())), preferred_element_type=f32)


def _rope128(r, cos, sin):
    lane = lax.broadcasted_iota(i32, r.shape, 1)
    half = MLA_ROPE // 2
    partner = jnp.where(lane < half, -pltpu.roll(r, LANE - half, axis=1), pltpu.roll(r, half, axis=1))
    return r * cos + partner * sin


def _ones_row_pad(n_cols):
    row = lax.broadcasted_iota(i32, (VT_ROWS - MLA_V, n_cols), 0)
    return jnp.where(row == 0, 1.0, 0.0).astype(bf16)


def _mla_prep_kernel(cq_ref, ckv_ref, kr_ref, cos_ref, sin_ref, cost_ref, sint_ref, qn_ref, kvn_ref,
                     wqt_ref, wk_ref, wvt_ref, qt_ref, k_ref, vt_ref):
    scale = (MLA_NOPE + MLA_ROPE) ** -0.5 * LOG2E
    half = MLA_ROPE // 2
    cost, sint = cost_ref[...], sint_ref[...]
    cq = _rms(cq_ref[...].astype(f32), qn_ref[...]).astype(bf16)
    ckv = _rms(ckv_ref[...].astype(f32), kvn_ref[...]).astype(bf16)
    kr = _rope128(kr_ref[...].astype(f32), cos_ref[...], sin_ref[...]).astype(bf16)
    pad = _ones_row_pad(TM)
    for h in range(MLA_HEADS):
        r0 = h * QK_PAD
        qt = _nt_dot(wqt_ref[r0:r0 + QK_PAD, :], cq) * scale
        x1, x2 = qt[MLA_NOPE:MLA_NOPE + half], qt[MLA_NOPE + half:MLA_NOPE + MLA_ROPE]
        qt_ref[r0:r0 + MLA_NOPE, :] = qt[:MLA_NOPE].astype(bf16)
        qt_ref[r0 + MLA_NOPE:r0 + MLA_NOPE + half, :] = (x1 * cost - x2 * sint).astype(bf16)
        qt_ref[r0 + MLA_NOPE + half:r0 + MLA_NOPE + MLA_ROPE, :] = (x1 * sint + x2 * cost).astype(bf16)
        qt_ref[r0 + MLA_NOPE + MLA_ROPE:r0 + QK_PAD, :] = qt[MLA_NOPE + MLA_ROPE:].astype(bf16)
        kn = jnp.dot(ckv, wk_ref[:, h * LANE:(h + 1) * LANE], preferred_element_type=f32)
        k_ref[:, r0:r0 + LANE] = kn.astype(bf16)
        k_ref[:, r0 + LANE:r0 + QK_PAD] = kr
        vt = _nt_dot(wvt_ref[h * MLA_V:(h + 1) * MLA_V, :], ckv)
        vt_ref[h, 0:MLA_V, :] = vt.astype(bf16)
        vt_ref[h, MLA_V:VT_ROWS, :] = pad


def _mla_prep(proj, cos, sin, cost, sint, qn, kvn, wqt, wk, wvt, layer, B, S):
    T = B * S
    assert TM == TK
    nk = S // TK
    const = lambda i: (0, 0)
    per_layer = lambda i: (layer, 0, 0)
    nq, nkc = MLA_HEADS * QK_PAD, MLA_HEADS * LANE
    return pl.pallas_call(
        _mla_prep_kernel, grid=(T // TM,),
        in_specs=[pl.BlockSpec((TM, MLA_Q_RANK), lambda i: (i, COL_CQ // MLA_Q_RANK)),
                  pl.BlockSpec((TM, MLA_KV_RANK), lambda i: (i, COL_CKV // MLA_KV_RANK)),
                  pl.BlockSpec((TM, LANE), lambda i: (i, COL_KR // LANE)),
                  pl.BlockSpec((TM, LANE), lambda i: (i, 0)),
                  pl.BlockSpec((TM, LANE), lambda i: (i, 0)),
                  pl.BlockSpec((MLA_ROPE // 2, TM), lambda i: (0, i)),
                  pl.BlockSpec((MLA_ROPE // 2, TM), lambda i: (0, i)),
                  pl.BlockSpec((1, MLA_Q_RANK), const),
                  pl.BlockSpec((1, MLA_KV_RANK), const),
                  pl.BlockSpec((None, nq, MLA_Q_RANK), per_layer),
                  pl.BlockSpec((None, MLA_KV_RANK, nkc), per_layer),
                  pl.BlockSpec((None, MLA_HEADS * MLA_V, MLA_KV_RANK), per_layer)],
        out_specs=[pl.BlockSpec((nq, TM), lambda i: (0, i)),
                   pl.BlockSpec((TM, nq), lambda i: (i, 0)),
                   pl.BlockSpec((None, MLA_HEADS, None, VT_ROWS, TK), lambda i: (i // nk, 0, i % nk, 0, 0))],
        out_shape=[jax.ShapeDtypeStruct((nq, T), bf16), jax.ShapeDtypeStruct((T, nq), bf16),
                   jax.ShapeDtypeStruct((B, MLA_HEADS, nk, VT_ROWS, TK), bf16)],
        compiler_params=_cparams(1), name="mla_prep")(proj, proj, proj, cos, sin, cost, sint, qn, kvn, wqt, wk, wvt)


def _softmax_probs(s, m, shift, lo=0):
    m_old = m[:, lo:]
    m_new = jnp.maximum(m_old, jnp.max(s, axis=0, keepdims=True) + shift)
    alpha = jnp.exp2(m_old - m_new)
    p = jnp.exp2(s - (m_new - shift)).astype(bf16)
    return (m_new if lo == 0 else jnp.concatenate([m[:, :lo], m_new], axis=1)), alpha, p


def _accumulate(acc_ref, idx, alpha, vt, p, lo=0):
    acc_ref[idx, :, lo:] = alpha * acc_ref[idx, :, lo:] + jnp.dot(vt, p, preferred_element_type=f32)


def _attn_out_t(acc):
    return acc[0:MLA_V] * (1.0 / acc[MLA_V:MLA_V + 1])


def _diag_visible(lo):
    krow = lax.broadcasted_iota(i32, (TK, TQ - lo), 0)
    qcol = lax.broadcasted_iota(i32, (TK, TQ - lo), 1)
    return krow <= qcol


MLA_HEADS_PER_STEP = 2


def _mla_attn_kernel(qt_ref, k_ref, vt_ref, o_ref, acc_ref, p_ref):
    i = pl.program_id(2)
    acc_ref[...] = jnp.zeros_like(acc_ref)
    p_ref[...] = jnp.zeros_like(p_ref)
    n_full = i * (TQ // TK)
    heads = range(MLA_HEADS_PER_STEP)

    def scores(c, j, lo=0):
        k = k_ref[pl.ds(pl.multiple_of(j * TK, TK), TK), c * QK_PAD:(c + 1) * QK_PAD]
        return jnp.dot(k, qt_ref[c * QK_PAD:(c + 1) * QK_PAD, lo:], preferred_element_type=f32)

    def flush(alphas, j):
        for c in heads:
            _accumulate(acc_ref, c, alphas[c], vt_ref[c, j], p_ref[c])

    def full(j, carry):
        ms, alphas = carry
        flush(alphas, jnp.maximum(j - 1, 0))
        new = [_softmax_probs(scores(c, j), ms[c], 0.0) for c in heads]
        for c in heads:
            p_ref[c] = new[c][2]
        return tuple(n[0] for n in new), tuple(n[1] for n in new)

    minf = jnp.full((1, TQ), -jnp.inf, f32)
    one = jnp.ones((1, TQ), f32)
    ms, alphas = lax.fori_loop(0, n_full, full, ((minf,) * MLA_HEADS_PER_STEP, (one,) * MLA_HEADS_PER_STEP))
    flush(alphas, jnp.maximum(n_full - 1, 0))
    for d in range(TQ // TK):
        j, lo = n_full + d, d * TK
        for c in heads:
            m, alpha, p = _softmax_probs(jnp.where(_diag_visible(lo), scores(c, j, lo), -jnp.inf), ms[c], 0.0, lo)
            _accumulate(acc_ref, c, alpha, vt_ref[c, j], p, lo)
            ms = ms[:c] + (m,) + ms[c + 1:]
    for c in heads:
        o_ref[:, c * MLA_V:(c + 1) * MLA_V] = _attn_out_t(acc_ref[c]).T.astype(bf16)


def _mla_attn(qt, k, vt, B, S):
    T = B * S
    nq, nk = S // TQ, S // TK
    hs = MLA_HEADS_PER_STEP
    return pl.pallas_call(
        _mla_attn_kernel, grid=(B, MLA_HEADS // hs, nq),
        in_specs=[pl.BlockSpec((hs * QK_PAD, TQ), lambda b, h, i: (h, b * nq + i)),
                  pl.BlockSpec((S, hs * QK_PAD), lambda b, h, i: (b, h)),
                  pl.BlockSpec((None, hs, nk, VT_ROWS, TK), lambda b, h, i: (b, h, 0, 0, 0))],
        out_specs=pl.BlockSpec((TQ, hs * MLA_V), lambda b, h, i: (b * nq + i, h)),
        out_shape=jax.ShapeDtypeStruct((T, MLA_HEADS * MLA_V), bf16),
        scratch_shapes=[pltpu.VMEM((hs, VT_ROWS, TQ), f32), pltpu.VMEM((hs, TK, TQ), bf16)],
        compiler_params=_cparams(3), name="mla_attn")(qt, k, vt)


def _diff_attn_kernel(pmin_ref, kmax_ref, lam_ref, q_ref, k_ref, v_ref, posq_ref, posk_ref, tab_ref, sub_ref, o_ref,
                      vt_ref, acc_ref, p_ref, *, lam_init, nq, nk):
    b, h, i = pl.program_id(0), pl.program_id(1), pl.program_id(2)

    @pl.when(i == 0)
    def _():
        pad = _ones_row_pad(TK)
        for j in range(nk):
            vt_ref[j, 0:DIFF_V, :] = v_ref[j * TK:(j + 1) * TK, :].astype(f32).T.astype(bf16)
            vt_ref[j, DIFF_V:VT_ROWS, :] = pad

    lv = lam_ref[...]
    lam = (jnp.exp(jnp.sum(lv[0:1] * lv[1:2], axis=1, keepdims=True))
           - jnp.exp(jnp.sum(lv[2:3] * lv[3:4], axis=1, keepdims=True)) + lam_init)
    qt = q_ref[...].astype(f32).T
    frow = lax.broadcasted_iota(i32, qt.shape, 0)
    qm = (jnp.where(frow < DIFF_QK, qt, 0.0).astype(bf16), jnp.where(frow >= DIFF_QK, qt, 0.0).astype(bf16))
    acc_ref[...] = jnp.zeros_like(acc_ref)
    posq = posq_ref[...]
    tab = jnp.broadcast_to(tab_ref[pl.ds(h, 1), :], (TK, LANE))
    far_bias = tab_ref[pl.ds(h, 1), REL_CLIP:REL_CLIP + 1]
    n_full = i * (TQ // TK)

    def scores(j, lo=0):
        off = pl.multiple_of(j * TK, TK)
        q_both = jnp.concatenate([qm[0][:, lo:], qm[1][:, lo:]], axis=1)
        return jnp.dot(k_ref[pl.ds(off, TK), :], q_both, preferred_element_type=f32), off

    def bias_tile(off, lo):
        n = jnp.clip(posq[:, lo:] - posk_ref[pl.ds(off, TK), :], 0, REL_CLIP)
        return jnp.concatenate(
            [jnp.take_along_axis(tab, n[:, c * LANE:(c + 1) * LANE], axis=1) for c in range((TQ - lo) // LANE)],
            axis=1)

    def flush(alphas, j):
        for c in range(2):
            _accumulate(acc_ref, c, alphas[c], vt_ref[j], p_ref[c])

    def softmax_tile(j, ms, far_tile, lo=0, visible=None):
        s2, off = scores(j, lo)
        bias = None if far_tile else bias_tile(off, lo)
        out = []
        for c in range(2):
            s = s2[:, c * (TQ - lo):(c + 1) * (TQ - lo)]
            if not far_tile:
                s = s + bias
            if visible is not None:
                s = jnp.where(visible, s, -jnp.inf)
            out.append(_softmax_probs(s, ms[c], far_bias if far_tile else 0.0, lo))
        return out

    def below_diagonal(far_tile, j, carry):
        ms, alphas = carry
        flush(alphas, jnp.maximum(j - 1, 0))
        new = softmax_tile(j, ms, far_tile)
        for c in range(2):
            p_ref[c] = new[c][2]
        return tuple(n[0] for n in new), tuple(n[1] for n in new)

    def body(j, carry):
        is_far = pmin_ref[b * nq + i] - kmax_ref[b * nk + j] >= REL_FAR
        return lax.cond(is_far, functools.partial(below_diagonal, True), functools.partial(below_diagonal, False),
                        j, carry)

    p_ref[...] = jnp.zeros_like(p_ref)
    minf = jnp.full((1, TQ), -jnp.inf, f32)
    one = jnp.ones((1, TQ), f32)
    ms, alphas = lax.fori_loop(0, n_full, body, ((minf, minf), (one, one)))
    flush(alphas, jnp.maximum(n_full - 1, 0))
    for d in range(TQ // TK):
        j, lo = n_full + d, d * TK
        new = softmax_tile(j, ms, False, lo, _diag_visible(lo))
        for c in range(2):
            _accumulate(acc_ref, c, new[c][1], vt_ref[j], new[c][2], lo)
        ms = tuple(n[0] for n in new)
    ot = _attn_out_t(acc_ref[0]) - lam * _attn_out_t(acc_ref[1])
    ot = ot * lax.rsqrt(jnp.mean(ot * ot, axis=0, keepdims=True) + NORM_EPS) * (sub_ref[...] * (1.0 - lam_init))
    o_ref[...] = ot.T.astype(bf16)


def _diff_attn(proj, lam_vecs, positions, tab, subln, lam_init, B, S):
    T = B * S
    nq, nk = S // TQ, S // TK
    cq, ck, cv = COL_DQ // LANE, COL_DK // LANE, COL_DV // LANE
    pmin = positions.reshape(B * nq, TQ).min(axis=1)
    kmax = positions.reshape(B * nk, TK).max(axis=1)
    grid_spec = pltpu.PrefetchScalarGridSpec(
        num_scalar_prefetch=2, grid=(B, DIFF_HEADS, nq),
        in_specs=[pl.BlockSpec((4, DIFF_QK), lambda b, h, i, *_: (0, 0)),
                  pl.BlockSpec((TQ, LANE), lambda b, h, i, *_: (b * nq + i, cq + h)),
                  pl.BlockSpec((S, LANE), lambda b, h, i, *_: (b, ck + h)),
                  pl.BlockSpec((S, LANE), lambda b, h, i, *_: (b, cv + h)),
                  pl.BlockSpec((1, TQ), lambda b, h, i, *_: (0, b * nq + i)),
                  pl.BlockSpec((S, 1), lambda b, h, i, *_: (b, 0)),
                  pl.BlockSpec((8, LANE), lambda b, h, i, *_: (0, 0)),
                  pl.BlockSpec((DIFF_V, 1), lambda b, h, i, *_: (0, 0))],
        out_specs=pl.BlockSpec((TQ, DIFF_V), lambda b, h, i, *_: (b * nq + i, h)),
        scratch_shapes=[pltpu.VMEM((nk, VT_ROWS, TK), bf16), pltpu.VMEM((2, VT_ROWS, TQ), f32),
                        pltpu.VMEM((2, TK, TQ), bf16)])
    return pl.pallas_call(
        functools.partial(_diff_attn_kernel, lam_init=lam_init, nq=nq, nk=nk), grid_spec=grid_spec,
        out_shape=jax.ShapeDtypeStruct((T, DIFF_HEADS * DIFF_V), bf16),
        compiler_params=_cparams(3), name="diff_attn")(
            pmin, kmax, lam_vecs, proj, proj, proj, positions.reshape(1, T), positions.reshape(T, 1), tab, subln)


ROUTER_ROWS = 64


def _outproj_router_kernel(a_ref, b_ref, c_ref, wo_ref, h_ref, g_ref, wr_ref, br_ref,
                           h2_ref, xn_ref, eid_ref, gate_ref):
    na, nb = a_ref.shape[1], b_ref.shape[1]
    mixed = (jnp.dot(a_ref[...], wo_ref[0:na, :], preferred_element_type=f32)
             + jnp.dot(b_ref[...], wo_ref[na:na + nb, :], preferred_element_type=f32)
             + jnp.dot(c_ref[...], wo_ref[na + nb:, :], preferred_element_type=f32))
    h2 = h_ref[...] + mixed
    h2_ref[...] = h2
    xn = _rms(h2, g_ref[...])
    xh = xn.astype(bf16)
    xl = (xn - xh.astype(f32)).astype(bf16)
    _to_slab(xn_ref, _pack_bf16_pairs(xn))
    wr = wr_ref[...]
    ah = jnp.dot(xh, wr, preferred_element_type=f32)
    al = jnp.dot(xl, wr, preferred_element_type=f32)
    logit = (ah + pltpu.roll(ah, ROUTER_ROWS, axis=1) + al).T[:ROUTER_ROWS] + br_ref[...]

    iota8 = lax.broadcasted_iota(i32, (8, TM), 0).astype(f32)
    gl = jnp.where(iota8 < N_GROUPS, logit[0:8], -jnp.inf)
    gmax = jnp.max(gl, axis=0, keepdims=True)
    gidx = jnp.min(jnp.where(gl == gmax, iota8, 8.0), axis=0, keepdims=True)
    ggate = 1.0 / jnp.sum(jnp.exp(gl - gmax), axis=0, keepdims=True)

    e = jnp.zeros((EXPERTS_PER_GROUP, TM), f32)
    for g in range(N_GROUPS):
        e = jnp.where(gidx == g, logit[8 + 8 * g:16 + 8 * g], e)
    m1 = jnp.max(e, axis=0, keepdims=True)
    i1 = jnp.min(jnp.where(e == m1, iota8, 8.0), axis=0, keepdims=True)
    e2 = jnp.where(iota8 == i1, -jnp.inf, e)
    m2 = jnp.max(e2, axis=0, keepdims=True)
    i2 = jnp.min(jnp.where(e2 == m2, iota8, 8.0), axis=0, keepdims=True)
    r = jnp.exp(m2 - m1)
    p1 = 1.0 / (1.0 + r)
    p2 = r / (1.0 + r)
    eid_ref[0:1, :] = (gidx * EXPERTS_PER_GROUP + i1).astype(i32)
    eid_ref[1:2, :] = (gidx * EXPERTS_PER_GROUP + i2).astype(i32)
    gate_ref[0:1, :] = ggate * p1
    gate_ref[1:2, :] = ggate * p2


def _outproj_router(a, b, c, wo, h, g, wr, br, layer):
    T = h.shape[0]
    row = lambda i: (i, 0)
    const = lambda i: (0, 0)
    per_layer = lambda i: (layer, 0, 0)
    na, nb, nc = a.shape[1], b.shape[1], c.shape[1]
    return pl.pallas_call(
        _outproj_router_kernel, grid=(T // TM,),
        in_specs=[pl.BlockSpec((TM, na), row), pl.BlockSpec((TM, nb), row), pl.BlockSpec((TM, nc), row),
                  pl.BlockSpec((None, na + nb + nc, D_MODEL), per_layer, pipeline_mode=pl.Buffered(1)),
                  pl.BlockSpec((TM, D_MODEL), row),
                  pl.BlockSpec((1, D_MODEL), const),
                  pl.BlockSpec((None, D_MODEL, 2 * ROUTER_ROWS), per_layer),
                  pl.BlockSpec((None, ROUTER_ROWS, 1), per_layer)],
        out_specs=[pl.BlockSpec((TM, D_MODEL), row), pl.BlockSpec((TM * XPACK_ROWS, LANE), row),
                   pl.BlockSpec((TOP_K, TM), lambda i: (0, i)), pl.BlockSpec((TOP_K, TM), lambda i: (0, i))],
        out_shape=[jax.ShapeDtypeStruct((T, D_MODEL), f32), jax.ShapeDtypeStruct((T * XPACK_ROWS, LANE), jnp.uint32),
                   jax.ShapeDtypeStruct((TOP_K, T), i32), jax.ShapeDtypeStruct((TOP_K, T), f32)],
        compiler_params=_cparams(1), name="outproj_router")(a, b, c, wo, h, g, wr, br)


def _moe_kernel(src_ref, dst_ref, cstart_ref, nvalid_ref, be_ref, wslot_ref, enext_ref, nused_ref,
                x_hbm, wg_hbm, wu_hbm, wd_hbm, y_hbm, xbuf, ybuf, wgbuf, wubuf, wdbuf, gsem, ssem, wsem, *, layer):
    b = pl.program_id(0)
    slot = b % 2
    n_used = nused_ref[0]

    def weight_copies(e, buf):
        return [pltpu.make_async_copy(hbm.at[layer, e], vbuf.at[buf], wsem.at[buf])
                for hbm, vbuf in ((wg_hbm, wgbuf), (wu_hbm, wubuf), (wd_hbm, wdbuf))]

    def gather_row(base, buf, r):
        return pltpu.make_async_copy(x_hbm.at[pl.ds(src_ref[base + r] * XPACK_ROWS, XPACK_ROWS)],
                                     xbuf.at[buf, pl.ds(r * XPACK_ROWS, XPACK_ROWS)], gsem.at[buf])

    def scatter_row(base, buf, r):
        return pltpu.make_async_copy(ybuf.at[buf, pl.ds(r * SLAB_ROWS, SLAB_ROWS)],
                                     y_hbm.at[pl.ds(dst_ref[base + r] * SLAB_ROWS, SLAB_ROWS)], ssem.at[buf])

    def start_group(make_row, base, buf, g):
        for u in range(8):
            make_row(base, buf, 8 * g + u).start(priority=u % 2)

    def start_gather(blk, buf):
        base = cstart_ref[blk]

        def group(g, carry):
            start_group(gather_row, base, buf, g)
            return carry
        lax.fori_loop(0, MOE_BLK // 8, group, 0)

    def start_scatter(blk, buf):
        base, nv = cstart_ref[blk], nvalid_ref[blk]

        def group(g, carry):
            start_group(scatter_row, base, buf, g)
            return carry
        lax.fori_loop(0, nv >> 3, group, 0)

        def single(r, carry):
            scatter_row(base, buf, r).start(priority=0)
            return carry
        lax.fori_loop(nv & ~7, nv, single, 0)

    def wait_gather(buf):
        pltpu.make_async_copy(x_hbm.at[pl.ds(0, MOE_BLK * XPACK_ROWS)], xbuf.at[buf], gsem.at[buf]).wait()

    def wait_scatter(blk, buf):
        nv = nvalid_ref[blk]
        rows = MOE_BLK
        while rows >= 1:
            @pl.when((nv & rows) != 0)
            def _(rows=rows):
                n = rows * SLAB_ROWS
                pltpu.make_async_copy(ybuf.at[buf, pl.ds(0, n)], y_hbm.at[pl.ds(0, n)], ssem.at[buf]).wait()
            rows //= 2

    def per_slot(fn):
        for buf in range(2):
            pl.when(slot == buf)(functools.partial(fn, buf))

    @pl.when(b == 0)
    def _():
        start_gather(0, 0)
        for cp in weight_copies(be_ref[0], wslot_ref[0]):
            cp.start()

    ws = wslot_ref[b]
    first_of_expert = jnp.logical_or(b == 0, be_ref[jnp.maximum(b - 1, 0)] != be_ref[b])

    @pl.when(jnp.logical_and(b < n_used, first_of_expert))
    def _():
        @pl.when(enext_ref[b] >= 0)
        def _():
            for cp in weight_copies(enext_ref[b], 1 - ws):
                cp.start()
        for cp in weight_copies(be_ref[b], ws):
            cp.wait()

    @pl.when(b + 1 < n_used)
    def _():
        per_slot(lambda buf: start_gather(b + 1, 1 - buf))

    @pl.when(b < n_used)
    def _():
        wait_gather(slot)

        @pl.when(b >= 2)
        def _():
            wait_scatter(b - 2, slot)

        x = _unpack_bf16_pairs(_unslab(xbuf.at[slot], MOE_BLK, XPACK_ROWS))
        hg = jnp.dot(x, wgbuf[ws], preferred_element_type=f32)
        hu = jnp.dot(x, wubuf[ws], preferred_element_type=f32)
        hid = hg * jax.nn.sigmoid(hg) * hu
        _to_slab(ybuf.at[slot], jnp.dot(hid, wdbuf[ws], preferred_element_type=f32))
        per_slot(lambda buf: start_scatter(b, buf))

    @pl.when(b == n_used - 1)
    def _():
        wait_scatter(b, slot)

        @pl.when(b >= 1)
        def _():
            wait_scatter(b - 1, 1 - slot)


def _moe(xn, tables, w_gate, w_up, w_down, layer):
    T = xn.shape[0] // XPACK_ROWS
    n_blocks = tables[4].shape[0]
    anywhere = pl.BlockSpec(memory_space=pl.ANY)
    grid_spec = pltpu.PrefetchScalarGridSpec(
        num_scalar_prefetch=len(tables), grid=(n_blocks,),
        in_specs=[anywhere, anywhere, anywhere, anywhere],
        out_specs=anywhere,
        scratch_shapes=[pltpu.VMEM((2, MOE_BLK * XPACK_ROWS, LANE), jnp.uint32),
                        pltpu.VMEM((2, MOE_BLK * SLAB_ROWS, LANE), f32),
                        pltpu.VMEM((2, D_MODEL, EXPERT_HIDDEN), f32), pltpu.VMEM((2, D_MODEL, EXPERT_HIDDEN), f32),
                        pltpu.VMEM((2, EXPERT_HIDDEN, D_MODEL), f32),
                        pltpu.SemaphoreType.DMA((2,)), pltpu.SemaphoreType.DMA((2,)), pltpu.SemaphoreType.DMA((2,))])
    return pl.pallas_call(
        functools.partial(_moe_kernel, layer=layer), grid_spec=grid_spec,
        out_shape=jax.ShapeDtypeStruct((TOP_K * T * SLAB_ROWS, LANE), f32),
        compiler_params=_cparams(1), name="moe")(*tables, xn, w_gate, w_up, w_down)


def _dispatch(eid, T):
    n_assign = TOP_K * T
    n_blocks = n_assign // MOE_BLK + N_EXPERTS
    e_flat = eid.T.reshape(-1)
    keys = jnp.sort(e_flat * n_assign + jnp.arange(n_assign, dtype=i32))
    asg = jnp.concatenate([keys % n_assign, jnp.zeros((MOE_BLK,), i32)])
    src, dst = asg // TOP_K, (asg % TOP_K) * T + asg // TOP_K
    edges = jnp.arange(N_EXPERTS + 1, dtype=i32) * n_assign
    bounds = jnp.sum(keys[None, :] < edges[:, None], axis=1, dtype=i32)
    start, counts = bounds[:-1], bounds[1:] - bounds[:-1]
    nblk = (counts + MOE_BLK - 1) // MOE_BLK
    bend = jnp.cumsum(nblk)
    n_used = bend[-1]
    b = jnp.arange(n_blocks, dtype=i32)
    be = jnp.minimum(jnp.sum(bend[None, :] <= jnp.minimum(b, n_used - 1)[:, None], axis=1, dtype=i32), N_EXPERTS - 1)
    j = b - (bend - nblk)[be]
    used = b < n_used
    cstart = jnp.where(used, start[be] + j * MOE_BLK, 0).astype(i32)
    nvalid = jnp.where(used, jnp.clip(counts[be] - j * MOE_BLK, 0, MOE_BLK), 0).astype(i32)
    used_e = counts > 0
    rank = jnp.cumsum(used_e.astype(i32)) - 1
    ids = jnp.arange(N_EXPERTS, dtype=i32)
    later = jnp.where(used_e[None, :] & (ids[None, :] > ids[:, None]), ids[None, :], N_EXPERTS)
    nxt = jnp.min(later, axis=1)
    enext = jnp.where(nxt < N_EXPERTS, nxt, -1).astype(i32)
    return src, dst, cstart, nvalid, be, (rank[be] % 2).astype(i32), enext[be], n_used.reshape(1).astype(i32)


def _final_kernel(h_ref, y0_ref, y1_ref, gate_ref, g_ref, o_ref):
    o_ref[...] = _rms(_combine(h_ref, y0_ref, y1_ref, gate_ref), g_ref[...])


def _final(h, moe_out, g):
    T = h.shape[0]
    y, gate = moe_out
    return pl.pallas_call(
        _final_kernel, grid=(T // TM,),
        in_specs=_combine_specs(T) + [pl.BlockSpec((1, D_MODEL), lambda i: (0, 0))],
        out_specs=pl.BlockSpec((TM, D_MODEL), lambda i: (i, 0)),
        out_shape=jax.ShapeDtypeStruct((T, D_MODEL), f32),
        compiler_params=_cparams(1), name="final_norm")(h, y, y, gate, g)


def _t5_bucket(n):
    nf = jnp.maximum(n, 1).astype(f32)
    large = REL_MAX_EXACT + (jnp.log(nf / REL_MAX_EXACT) / math.log(REL_MAX_DISTANCE / REL_MAX_EXACT)
                             * (REL_BUCKETS - REL_MAX_EXACT)).astype(i32)
    return jnp.where(n < REL_MAX_EXACT, n, jnp.minimum(large, REL_BUCKETS - 1))


def _split_hi_lo(w):
    hi = w.astype(bf16)
    return hi, (w - hi.astype(f32)).astype(bf16)


def kernel(x, positions, attn_norm, w_in, pool_w, pool_scale, mla_q_norm, mla_w_uq, mla_kv_norm, mla_w_ukv,
           diff_lambda, diff_subln, rel_bias, w_out, ffn_norm, router_group_w, router_group_b,
           router_expert_w, router_expert_b, expert_w_gate, expert_w_up, expert_w_down, final_norm):
    B, S, D = x.shape
    T = B * S
    depth = w_in.shape[0]
    assert D == D_MODEL and S % TQ == 0 and T % TM == 0 and (TOP_K * T) % MOE_BLK == 0

    inv = 1.0 / (ROPE_THETA ** (jnp.arange(0, MLA_ROPE, 2, dtype=f32) / MLA_ROPE))
    ang = positions.astype(f32).reshape(T, 1) * inv
    zeros = jnp.zeros((T, LANE - MLA_ROPE), f32)
    cos = jnp.concatenate([jnp.cos(ang), jnp.cos(ang), zeros], axis=1)
    sin = jnp.concatenate([jnp.sin(ang), jnp.sin(ang), zeros], axis=1)
    cost, sint = jnp.cos(ang).T, jnp.sin(ang).T
    assert REL_CLIP == LANE - 1
    tab = jnp.pad(rel_bias[_t5_bucket(jnp.arange(LANE, dtype=i32))].T.astype(f32) * LOG2E,
                  ((0, 8 - DIFF_HEADS), (0, 0)))

    c_kr, c_dq = COL_KR + MLA_ROPE, COL_KR + MLA_ROPE + COL_DK - COL_DQ
    w_in_p = jnp.concatenate([w_in[:, :, :c_kr], jnp.zeros((depth, D, LANE - MLA_ROPE), f32),
                              w_in[:, :, c_kr:c_dq] * (DIFF_QK ** -0.5 * LOG2E), w_in[:, :, c_dq:]],
                             axis=2).astype(bf16)
    wq = mla_w_uq.reshape(depth, MLA_Q_RANK, MLA_HEADS, MLA_NOPE + MLA_ROPE)
    wq = jnp.pad(wq, ((0, 0), (0, 0), (0, 0), (0, QK_PAD - MLA_NOPE - MLA_ROPE)))
    wqt = wq.reshape(depth, MLA_Q_RANK, -1).transpose(0, 2, 1).astype(bf16)
    wkv = mla_w_ukv.reshape(depth, MLA_KV_RANK, MLA_HEADS, MLA_NOPE + MLA_V)
    wk = wkv[..., :MLA_NOPE].reshape(depth, MLA_KV_RANK, -1).astype(bf16)
    wvt = wkv[..., MLA_NOPE:].reshape(depth, MLA_KV_RANK, -1).transpose(0, 2, 1).astype(bf16)
    wo = w_out.astype(bf16)
    pool_wb = pool_w.astype(bf16)
    pad_g = jnp.zeros((depth, D, 8 - N_GROUPS), f32)
    pad_e = jnp.zeros((depth, D, ROUTER_ROWS - 8 - N_EXPERTS), f32)
    wr_hi, wr_lo = _split_hi_lo(jnp.concatenate([router_group_w, pad_g, router_expert_w, pad_e], axis=2))
    wr_cat = jnp.concatenate([wr_hi, wr_lo], axis=2)
    br = jnp.concatenate([router_group_b, pad_g[:, 0], router_expert_b, pad_e[:, 0]], axis=1)[:, :, None]

    h = x.reshape(T, D)
    moe_out = None
    for l in range(depth):
        lam_init = 0.8 - 0.6 * math.exp(-0.3 * l)
        proj, h = _inproj(h, moe_out, attn_norm[l].reshape(1, D), w_in_p, l)
        a_out = _pool(proj, pool_wb, pool_scale[l].reshape(1, POOL_WIDTH), l, B, S)
        qt, k, vt = _mla_prep(proj, cos, sin, cost, sint, mla_q_norm[l].reshape(1, -1), mla_kv_norm[l].reshape(1, -1),
                              wqt, wk, wvt, l, B, S)
        b_out = _mla_attn(qt, k, vt, B, S)
        c_out = _diff_attn(proj, diff_lambda[l], positions, tab, diff_subln[l].reshape(-1, 1), lam_init, B, S)
        h, xn, eid, gate = _outproj_router(a_out, b_out, c_out, wo, h, ffn_norm[l].reshape(1, D), wr_cat, br, l)
        y = _moe(xn, _dispatch(eid, T), expert_w_gate, expert_w_up, expert_w_down, l)
        moe_out = (y, gate.T)
    out = _final(h, moe_out, final_norm.reshape(1, D))
    return out.reshape(B, S, D)
```

```python
import functools
import math

import jax
import jax.numpy as jnp
from jax import lax
from jax.experimental import pallas as pl
from jax.experimental.pallas import tpu as pltpu

f32, bf16, i32 = jnp.float32, jnp.bfloat16, jnp.int32

D_MODEL = 2048
POOL_WINDOWS = (2, 4, 8, 16)
POOL_GROUPS = 4
POOL_WIDTH = 512
MLA_NOPE, MLA_ROPE, MLA_V, MLA_HEADS = 128, 64, 128, 6
MLA_Q_RANK, MLA_KV_RANK = 512, 256
ROPE_THETA = 10000.0
DIFF_QK, DIFF_V, DIFF_HEADS = 64, 128, 6
REL_BUCKETS, REL_MAX_EXACT, REL_MAX_DISTANCE = 32, 16, 128
N_GROUPS, EXPERTS_PER_GROUP, N_EXPERTS, TOP_K, EXPERT_HIDDEN = 4, 8, 32, 2, 512
NORM_EPS = 1e-6

LANE = 128
MXU_DIM = 256
VMEM_LIMIT = 56 * 1024 * 1024

N_IN_PAD = 3712
COL_CQ, COL_CKV, COL_KR, COL_DQ, COL_DK, COL_DV = 512, 1024, 1280, 1408, 2176, 2944
QK_PAD = 2 * LANE

TM = 256
TQ = 512
TK = 256
MOE_BLK = 256
SLAB_ROWS = D_MODEL // LANE
REL_CLIP = REL_MAX_DISTANCE - 1
REL_FAR = REL_MAX_DISTANCE
LOG2E = math.log2(math.e)


def _cparams(n_axes):
    return pltpu.CompilerParams(dimension_semantics=("arbitrary",) * n_axes, vmem_limit_bytes=VMEM_LIMIT)


def _rms(x, g):
    return x * lax.rsqrt(jnp.mean(x * x, axis=-1, keepdims=True) + NORM_EPS) * g


def _unslab(ref, rows, per_token=SLAB_ROWS):
    return jnp.concatenate([ref[pl.ds(s, rows, stride=per_token), :] for s in range(per_token)], axis=1)


def _to_slab(ref, val):
    per_token = val.shape[1] // LANE
    for s in range(per_token):
        ref[pl.ds(s, val.shape[0], stride=per_token), :] = val[:, s * LANE:(s + 1) * LANE]


XPACK_ROWS = SLAB_ROWS // 2


def _pack_bf16_pairs(x):
    bits = lax.bitcast_convert_type(x.astype(bf16).astype(f32), jnp.uint32)
    half = x.shape[1] // 2
    return bits[:, half:] | (bits[:, :half] >> 16)


def _unpack_bf16_pairs(w):
    lo = lax.bitcast_convert_type(w << 16, f32)
    hi = lax.bitcast_convert_type(w & jnp.uint32(0xFFFF0000), f32)
    return jnp.concatenate([lo, hi], axis=1)


def _combine(h_ref, y0_ref, y1_ref, gate_ref):
    gate = gate_ref[...]
    return h_ref[...] + _unslab(y0_ref, TM) * gate[:, 0:1] + _unslab(y1_ref, TM) * gate[:, 1:2]


def _combine_specs(T):
    plane1 = T // TM
    return [pl.BlockSpec((TM, D_MODEL), lambda i: (i, 0)),
            pl.BlockSpec((TM * SLAB_ROWS, LANE), lambda i: (i, 0)),
            pl.BlockSpec((TM * SLAB_ROWS, LANE), lambda i: (plane1 + i, 0)),
            pl.BlockSpec((TM, TOP_K), lambda i: (i, 0))]


def _inproj_kernel(*refs, combine):
    if combine:
        h_ref, y0_ref, y1_ref, gate_ref, g_ref, w_ref, proj_ref, hout_ref = refs
        h = _combine(h_ref, y0_ref, y1_ref, gate_ref)
        hout_ref[...] = h
    else:
        h_ref, g_ref, w_ref, proj_ref = refs
        h = h_ref[...]
    u = _rms(h, g_ref[...]).astype(bf16)
    for n0 in range(0, N_IN_PAD, 512):
        n1 = min(n0 + 512, N_IN_PAD)
        proj_ref[:, n0:n1] = jnp.dot(u, w_ref[:, n0:n1], preferred_element_type=f32).astype(bf16)


def _inproj(h, moe_out, g, w, layer):
    T = h.shape[0]
    combine = moe_out is not None
    row = lambda i: (i, 0)
    const = lambda i: (0, 0)
    if combine:
        y, gate = moe_out
        in_specs = _combine_specs(T)
        args = [h, y, y, gate]
    else:
        in_specs = [pl.BlockSpec((TM, D_MODEL), row)]
        args = [h]
    in_specs += [pl.BlockSpec((1, D_MODEL), const),
                 pl.BlockSpec((None, D_MODEL, N_IN_PAD), lambda i: (layer, 0, 0), pipeline_mode=pl.Buffered(1))]
    args += [g, w]
    out_shape = [jax.ShapeDtypeStruct((T, N_IN_PAD), bf16)]
    out_specs = [pl.BlockSpec((TM, N_IN_PAD), row)]
    if combine:
        out_shape.append(jax.ShapeDtypeStruct((T, D_MODEL), f32))
        out_specs.append(pl.BlockSpec((TM, D_MODEL), row))
    res = pl.pallas_call(
        functools.partial(_inproj_kernel, combine=combine),
        grid=(T // TM,), in_specs=in_specs, out_specs=out_specs, out_shape=out_shape,
        compiler_params=_cparams(1), name="inproj_c" if combine else "inproj")(*args)
    return (res[0], res[1]) if combine else (res[0], h)


def _pool_kernel(x_ref, w_ref, sc_ref, o_ref):
    g = pl.program_id(1)
    x = x_ref[...].astype(f32)
    t = lax.broadcasted_iota(i32, x.shape, 0)

    def lagged(v, k):
        return jnp.where(t >= k, pltpu.roll(v, k, axis=0), 0.0)

    s2 = x + lagged(x, 1)
    s4 = s2 + lagged(s2, 2)
    s8 = s4 + lagged(s4, 4)
    s16 = s8 + lagged(s8, 8)
    wsum = jnp.where(g == 0, s2, jnp.where(g == 1, s4, jnp.where(g == 2, s8, s16)))
    win = jnp.where(g == 0, 2, jnp.where(g == 1, 4, jnp.where(g == 2, 8, 16)))
    cnt = jnp.minimum(t + 1, win).astype(f32)
    d = (wsum / cnt - x).astype(bf16)
    y = jnp.dot(d, w_ref[...], preferred_element_type=f32) * sc_ref[...]
    o_ref[...] = y.astype(bf16)


def _pool(proj, w, scale, layer, B, S):
    T = B * S
    return pl.pallas_call(
        _pool_kernel, grid=(B, POOL_GROUPS),
        in_specs=[pl.BlockSpec((S, LANE), lambda b, g: (b, g)),
                  pl.BlockSpec((None, None, LANE, LANE), lambda b, g: (layer, g, 0, 0)),
                  pl.BlockSpec((1, LANE), lambda b, g: (0, g))],
        out_specs=pl.BlockSpec((S, LANE), lambda b, g: (b, g)),
        out_shape=jax.ShapeDtypeStruct((T, POOL_WIDTH), bf16),
        compiler_params=_cparams(2), name="pool")(proj, w, scale)


VT_ROWS = MLA_V + 16


def _nt_dot(a, b):
    return lax.dot_general(a, b, (((1,), (1,)), ((), ())), preferred_element_type=f32)


def _rope128(r, cos, sin):
    lane = lax.broadcasted_iota(i32, r.shape, 1)
    half = MLA_ROPE // 2
    partner = jnp.where(lane < half, -pltpu.roll(r, LANE - half, axis=1), pltpu.roll(r, half, axis=1))
    return r * cos + partner * sin


def _ones_row_pad(n_cols):
    row = lax.broadcasted_iota(i32, (VT_ROWS - MLA_V, n_cols), 0)
    return jnp.where(row == 0, 1.0, 0.0).astype(bf16)


def _mla_prep_kernel(cq_ref, ckv_ref, kr_ref, cos_ref, sin_ref, cost_ref, sint_ref, qn_ref, kvn_ref,
                     wqt_ref, wk_ref, wvt_ref, qt_ref, k_ref, vt_ref):
    scale = (MLA_NOPE + MLA_ROPE) ** -0.5 * LOG2E
    half = MLA_ROPE // 2
    cost, sint = cost_ref[...], sint_ref[...]
    cq = _rms(cq_ref[...].astype(f32), qn_ref[...]).astype(bf16)
    ckv = _rms(ckv_ref[...].astype(f32), kvn_ref[...]).astype(bf16)
    kr = _rope128(kr_ref[...].astype(f32), cos_ref[...], sin_ref[...]).astype(bf16)
    pad = _ones_row_pad(TM)
    for h in range(MLA_HEADS):
        r0 = h * QK_PAD
        qt = _nt_dot(wqt_ref[r0:r0 + QK_PAD, :], cq) * scale
        x1, x2 = qt[MLA_NOPE:MLA_NOPE + half], qt[MLA_NOPE + half:MLA_NOPE + MLA_ROPE]
        qt_ref[r0:r0 + MLA_NOPE, :] = qt[:MLA_NOPE].astype(bf16)
        qt_ref[r0 + MLA_NOPE:r0 + MLA_NOPE + half, :] = (x1 * cost - x2 * sint).astype(bf16)
        qt_ref[r0 + MLA_NOPE + half:r0 + MLA_NOPE + MLA_ROPE, :] = (x1 * sint + x2 * cost).astype(bf16)
        qt_ref[r0 + MLA_NOPE + MLA_ROPE:r0 + QK_PAD, :] = qt[MLA_NOPE + MLA_ROPE:].astype(bf16)
        kn = jnp.dot(ckv, wk_ref[:, h * LANE:(h + 1) * LANE], preferred_element_type=f32)
        k_ref[:, r0:r0 + LANE] = kn.astype(bf16)
        k_ref[:, r0 + LANE:r0 + QK_PAD] = kr
        vt = _nt_dot(wvt_ref[h * MLA_V:(h + 1) * MLA_V, :], ckv)
        vt_ref[h, 0:MLA_V, :] = vt.astype(bf16)
        vt_ref[h, MLA_V:VT_ROWS, :] = pad


def _mla_prep(proj, cos, sin, cost, sint, qn, kvn, wqt, wk, wvt, layer, B, S):
    T = B * S
    assert TM == TK
    nk = S // TK
    const = lambda i: (0, 0)
    per_layer = lambda i: (layer, 0, 0)
    nq, nkc = MLA_HEADS * QK_PAD, MLA_HEADS * LANE
    return pl.pallas_call(
        _mla_prep_kernel, grid=(T // TM,),
        in_specs=[pl.BlockSpec((TM, MLA_Q_RANK), lambda i: (i, COL_CQ // MLA_Q_RANK)),
                  pl.BlockSpec((TM, MLA_KV_RANK), lambda i: (i, COL_CKV // MLA_KV_RANK)),
                  pl.BlockSpec((TM, LANE), lambda i: (i, COL_KR // LANE)),
                  pl.BlockSpec((TM, LANE), lambda i: (i, 0)),
                  pl.BlockSpec((TM, LANE), lambda i: (i, 0)),
                  pl.BlockSpec((MLA_ROPE // 2, TM), lambda i: (0, i)),
                  pl.BlockSpec((MLA_ROPE // 2, TM), lambda i: (0, i)),
                  pl.BlockSpec((1, MLA_Q_RANK), const),
                  pl.BlockSpec((1, MLA_KV_RANK), const),
                  pl.BlockSpec((None, nq, MLA_Q_RANK), per_layer),
                  pl.BlockSpec((None, MLA_KV_RANK, nkc), per_layer),
                  pl.BlockSpec((None, MLA_HEADS * MLA_V, MLA_KV_RANK), per_layer)],
        out_specs=[pl.BlockSpec((nq, TM), lambda i: (0, i)),
                   pl.BlockSpec((TM, nq), lambda i: (i, 0)),
                   pl.BlockSpec((None, MLA_HEADS, None, VT_ROWS, TK), lambda i: (i // nk, 0, i % nk, 0, 0))],
        out_shape=[jax.ShapeDtypeStruct((nq, T), bf16), jax.ShapeDtypeStruct((T, nq), bf16),
                   jax.ShapeDtypeStruct((B, MLA_HEADS, nk, VT_ROWS, TK), bf16)],
        compiler_params=_cparams(1), name="mla_prep")(proj, proj, proj, cos, sin, cost, sint, qn, kvn, wqt, wk, wvt)


def _softmax_probs(s, m, shift, lo=0):
    m_old = m[:, lo:]
    m_new = jnp.maximum(m_old, jnp.max(s, axis=0, keepdims=True) + shift)
    alpha = jnp.exp2(m_old - m_new)
    p = jnp.exp2(s - (m_new - shift)).astype(bf16)
    return (m_new if lo == 0 else jnp.concatenate([m[:, :lo], m_new], axis=1)), alpha, p


def _accumulate(acc_ref, idx, alpha, vt, p, lo=0):
    acc_ref[idx, :, lo:] = alpha * acc_ref[idx, :, lo:] + jnp.dot(vt, p, preferred_element_type=f32)


def _attn_out_t(acc):
    return acc[0:MLA_V] * (1.0 / acc[MLA_V:MLA_V + 1])


def _diag_visible(lo):
    krow = lax.broadcasted_iota(i32, (TK, TQ - lo), 0)
    qcol = lax.broadcasted_iota(i32, (TK, TQ - lo), 1)
    return krow <= qcol


MLA_HEADS_PER_STEP = 6


def _mla_attn_kernel(qt_ref, k_ref, vt_ref, o_ref, acc_ref, p_ref):
    i = pl.program_id(2)
    acc_ref[...] = jnp.zeros_like(acc_ref)
    p_ref[...] = jnp.zeros_like(p_ref)
    n_full = i * (TQ // TK)
    heads = range(MLA_HEADS_PER_STEP)

    def scores(c, j, lo=0):
        k = k_ref[pl.ds(pl.multiple_of(j * TK, TK), TK), c * QK_PAD:(c + 1) * QK_PAD]
        return jnp.dot(k, qt_ref[c * QK_PAD:(c + 1) * QK_PAD, lo:], preferred_element_type=f32)

    def flush(alphas, j):
        for c in heads:
            _accumulate(acc_ref, c, alphas[c], vt_ref[c, j], p_ref[c])

    def full(j, carry):
        ms, alphas = carry
        flush(alphas, jnp.maximum(j - 1, 0))
        new = [_softmax_probs(scores(c, j), ms[c], 0.0) for c in heads]
        for c in heads:
            p_ref[c] = new[c][2]
        return tuple(n[0] for n in new), tuple(n[1] for n in new)

    minf = jnp.full((1, TQ), -jnp.inf, f32)
    one = jnp.ones((1, TQ), f32)
    ms, alphas = lax.fori_loop(0, n_full, full, ((minf,) * MLA_HEADS_PER_STEP, (one,) * MLA_HEADS_PER_STEP))
    flush(alphas, jnp.maximum(n_full - 1, 0))
    for d in range(TQ // TK):
        j, lo = n_full + d, d * TK
        for c in heads:
            m, alpha, p = _softmax_probs(jnp.where(_diag_visible(lo), scores(c, j, lo), -jnp.inf), ms[c], 0.0, lo)
            _accumulate(acc_ref, c, alpha, vt_ref[c, j], p, lo)
            ms = ms[:c] + (m,) + ms[c + 1:]
    for c in heads:
        o_ref[:, c * MLA_V:(c + 1) * MLA_V] = _attn_out_t(acc_ref[c]).T.astype(bf16)


def _mla_attn(qt, k, vt, B, S):
    T = B * S
    nq, nk = S // TQ, S // TK
    hs = MLA_HEADS_PER_STEP
    return pl.pallas_call(
        _mla_attn_kernel, grid=(B, MLA_HEADS // hs, nq),
        in_specs=[pl.BlockSpec((hs * QK_PAD, TQ), lambda b, h, i: (h, b * nq + i)),
                  pl.BlockSpec((S, hs * QK_PAD), lambda b, h, i: (b, h)),
                  pl.BlockSpec((None, hs, nk, VT_ROWS, TK), lambda b, h, i: (b, h, 0, 0, 0))],
        out_specs=pl.BlockSpec((TQ, hs * MLA_V), lambda b, h, i: (b * nq + i, h)),
        out_shape=jax.ShapeDtypeStruct((T, MLA_HEADS * MLA_V), bf16),
        scratch_shapes=[pltpu.VMEM((hs, VT_ROWS, TQ), f32), pltpu.VMEM((hs, TK, TQ), bf16)],
        compiler_params=_cparams(3), name="mla_attn")(qt, k, vt)


DIFF_HEADS_PER_STEP = 6


def _diff_attn_kernel(pmin_ref, kmax_ref, lam_ref, *refs, lam_init, nq, nk):
    hs = DIFF_HEADS_PER_STEP
    q_refs, k_refs, v_refs = refs[:hs], refs[hs:2 * hs], refs[2 * hs:3 * hs]
    posq_ref, posk_ref, tab_ref, sub_ref, o_ref, vt_ref, acc_ref, p_ref = refs[3 * hs:]
    b, hp, i = pl.program_id(0), pl.program_id(1), pl.program_id(2)
    heads = range(hs)
    chains = [(c, m) for c in heads for m in range(2)]

    @pl.when(i == 0)
    def _():
        pad = _ones_row_pad(TK)
        for c in heads:
            for j in range(nk):
                vt_ref[c, j, 0:DIFF_V, :] = v_refs[c][j * TK:(j + 1) * TK, :].astype(f32).T.astype(bf16)
                vt_ref[c, j, DIFF_V:VT_ROWS, :] = pad

    lv = lam_ref[...]
    lam = (jnp.exp(jnp.sum(lv[0:1] * lv[1:2], axis=1, keepdims=True))
           - jnp.exp(jnp.sum(lv[2:3] * lv[3:4], axis=1, keepdims=True)) + lam_init)
    frow = lax.broadcasted_iota(i32, (2 * DIFF_QK, TQ), 0)
    qm = []
    for c in heads:
        qt = q_refs[c][...].astype(f32).T
        qm.append((jnp.where(frow < DIFF_QK, qt, 0.0).astype(bf16), jnp.where(frow >= DIFF_QK, qt, 0.0).astype(bf16)))
    acc_ref[...] = jnp.zeros_like(acc_ref)
    posq = posq_ref[...]
    tabs = [jnp.broadcast_to(tab_ref[pl.ds(hp * hs + c, 1), :], (TK, LANE)) for c in heads]
    far_bias = [tab_ref[pl.ds(hp * hs + c, 1), REL_CLIP:REL_CLIP + 1] for c in heads]
    n_full = i * (TQ // TK)

    def scores(c, j, lo):
        off = pl.multiple_of(j * TK, TK)
        q_both = jnp.concatenate([qm[c][0][:, lo:], qm[c][1][:, lo:]], axis=1)
        return jnp.dot(k_refs[c][pl.ds(off, TK), :], q_both, preferred_element_type=f32)

    def distances(j, lo):
        off = pl.multiple_of(j * TK, TK)
        return jnp.clip(posq[:, lo:] - posk_ref[pl.ds(off, TK), :], 0, REL_CLIP)

    def bias_tile(c, n):
        return jnp.concatenate(
            [jnp.take_along_axis(tabs[c], n[:, t * LANE:(t + 1) * LANE], axis=1) for t in range(n.shape[1] // LANE)],
            axis=1)

    def flush(alphas, j):
        for c, m in chains:
            _accumulate(acc_ref, 2 * c + m, alphas[2 * c + m], vt_ref[c, j], p_ref[2 * c + m])

    def softmax_tile(j, ms, far_tile, lo=0, visible=None):
        n = None if far_tile else distances(j, lo)
        out = []
        for c in heads:
            s2 = scores(c, j, lo)
            bias = None if far_tile else bias_tile(c, n)
            for m in range(2):
                s = s2[:, m * (TQ - lo):(m + 1) * (TQ - lo)]
                if not far_tile:
                    s = s + bias
                if visible is not None:
                    s = jnp.where(visible, s, -jnp.inf)
                out.append(_softmax_probs(s, ms[2 * c + m], far_bias[c] if far_tile else 0.0, lo))
        return out

    def below_diagonal(far_tile, j, carry):
        ms, alphas = carry
        flush(alphas, jnp.maximum(j - 1, 0))
        new = softmax_tile(j, ms, far_tile)
        for idx in range(len(chains)):
            p_ref[idx] = new[idx][2]
        return tuple(n[0] for n in new), tuple(n[1] for n in new)

    def body(j, carry):
        is_far = pmin_ref[b * nq + i] - kmax_ref[b * nk + j] >= REL_FAR
        return lax.cond(is_far, functools.partial(below_diagonal, True), functools.partial(below_diagonal, False),
                        j, carry)

    p_ref[...] = jnp.zeros_like(p_ref)
    minf = jnp.full((1, TQ), -jnp.inf, f32)
    one = jnp.ones((1, TQ), f32)
    ms, alphas = lax.fori_loop(0, n_full, body, ((minf,) * len(chains), (one,) * len(chains)))
    flush(alphas, jnp.maximum(n_full - 1, 0))
    for d in range(TQ // TK):
        j, lo = n_full + d, d * TK
        new = softmax_tile(j, ms, False, lo, _diag_visible(lo))
        for c, m in chains:
            _accumulate(acc_ref, 2 * c + m, new[2 * c + m][1], vt_ref[c, j], new[2 * c + m][2], lo)
        ms = tuple(n[0] for n in new)
    gain = sub_ref[...] * (1.0 - lam_init)
    for c in heads:
        ot = _attn_out_t(acc_ref[2 * c]) - lam * _attn_out_t(acc_ref[2 * c + 1])
        ot = ot * lax.rsqrt(jnp.mean(ot * ot, axis=0, keepdims=True) + NORM_EPS) * gain
        o_ref[:, c * DIFF_V:(c + 1) * DIFF_V] = ot.T.astype(bf16)


def _diff_attn(proj, lam_vecs, positions, tab, subln, lam_init, B, S):
    T = B * S
    nq, nk = S // TQ, S // TK
    hs = DIFF_HEADS_PER_STEP
    pmin = positions.reshape(B * nq, TQ).min(axis=1)
    kmax = positions.reshape(B * nk, TK).max(axis=1)

    def head_specs(rows, col0, by_tile):
        def spec(c):
            if by_tile:
                return pl.BlockSpec((rows, LANE), lambda b, hp, i, *_: (b * nq + i, col0 // LANE + hp * hs + c))
            return pl.BlockSpec((rows, LANE), lambda b, hp, i, *_: (b, col0 // LANE + hp * hs + c))
        return [spec(c) for c in range(hs)]

    grid_spec = pltpu.PrefetchScalarGridSpec(
        num_scalar_prefetch=2, grid=(B, DIFF_HEADS // hs, nq),
        in_specs=([pl.BlockSpec((4, DIFF_QK), lambda b, hp, i, *_: (0, 0))]
                  + head_specs(TQ, COL_DQ, True) + head_specs(S, COL_DK, False) + head_specs(S, COL_DV, False)
                  + [pl.BlockSpec((1, TQ), lambda b, hp, i, *_: (0, b * nq + i)),
                     pl.BlockSpec((S, 1), lambda b, hp, i, *_: (b, 0)),
                     pl.BlockSpec((8, LANE), lambda b, hp, i, *_: (0, 0)),
                     pl.BlockSpec((DIFF_V, 1), lambda b, hp, i, *_: (0, 0))]),
        out_specs=pl.BlockSpec((TQ, hs * DIFF_V), lambda b, hp, i, *_: (b * nq + i, hp)),
        scratch_shapes=[pltpu.VMEM((hs, nk, VT_ROWS, TK), bf16), pltpu.VMEM((2 * hs, VT_ROWS, TQ), f32),
                        pltpu.VMEM((2 * hs, TK, TQ), bf16)])
    return pl.pallas_call(
        functools.partial(_diff_attn_kernel, lam_init=lam_init, nq=nq, nk=nk), grid_spec=grid_spec,
        out_shape=jax.ShapeDtypeStruct((T, DIFF_HEADS * DIFF_V), bf16),
        compiler_params=_cparams(3), name="diff_attn")(
            pmin, kmax, lam_vecs, *([proj] * (3 * hs)), positions.reshape(1, T), positions.reshape(T, 1), tab, subln)


ROUTER_ROWS = 64


def _outproj_router_kernel(a_ref, b_ref, c_ref, wo_ref, h_ref, g_ref, wr_ref, br_ref,
                           h2_ref, xn_ref, eid_ref, gate_ref):
    na, nb = a_ref.shape[1], b_ref.shape[1]
    mixed = (jnp.dot(a_ref[...], wo_ref[0:na, :], preferred_element_type=f32)
             + jnp.dot(b_ref[...], wo_ref[na:na + nb, :], preferred_element_type=f32)
             + jnp.dot(c_ref[...], wo_ref[na + nb:, :], preferred_element_type=f32))
    h2 = h_ref[...] + mixed
    h2_ref[...] = h2
    xn = _rms(h2, g_ref[...])
    xh = xn.astype(bf16)
    xl = (xn - xh.astype(f32)).astype(bf16)
    _to_slab(xn_ref, _pack_bf16_pairs(xn))
    wr = wr_ref[...]
    ah = jnp.dot(xh, wr, preferred_element_type=f32)
    al = jnp.dot(xl, wr, preferred_element_type=f32)
    logit = (ah + pltpu.roll(ah, ROUTER_ROWS, axis=1) + al).T[:ROUTER_ROWS] + br_ref[...]

    iota8 = lax.broadcasted_iota(i32, (8, TM), 0).astype(f32)
    gl = jnp.where(iota8 < N_GROUPS, logit[0:8], -jnp.inf)
    gmax = jnp.max(gl, axis=0, keepdims=True)
    gidx = jnp.min(jnp.where(gl == gmax, iota8, 8.0), axis=0, keepdims=True)
    ggate = 1.0 / jnp.sum(jnp.exp(gl - gmax), axis=0, keepdims=True)

    e = jnp.zeros((EXPERTS_PER_GROUP, TM), f32)
    for g in range(N_GROUPS):
        e = jnp.where(gidx == g, logit[8 + 8 * g:16 + 8 * g], e)
    m1 = jnp.max(e, axis=0, keepdims=True)
    i1 = jnp.min(jnp.where(e == m1, iota8, 8.0), axis=0, keepdims=True)
    e2 = jnp.where(iota8 == i1, -jnp.inf, e)
    m2 = jnp.max(e2, axis=0, keepdims=True)
    i2 = jnp.min(jnp.where(e2 == m2, iota8, 8.0), axis=0, keepdims=True)
    r = jnp.exp(m2 - m1)
    p1 = 1.0 / (1.0 + r)
    p2 = r / (1.0 + r)
    eid_ref[0:1, :] = (gidx * EXPERTS_PER_GROUP + i1).astype(i32)
    eid_ref[1:2, :] = (gidx * EXPERTS_PER_GROUP + i2).astype(i32)
    gate_ref[0:1, :] = ggate * p1
    gate_ref[1:2, :] = ggate * p2


def _outproj_router(a, b, c, wo, h, g, wr, br, layer):
    T = h.shape[0]
    row = lambda i: (i, 0)
    const = lambda i: (0, 0)
    per_layer = lambda i: (layer, 0, 0)
    na, nb, nc = a.shape[1], b.shape[1], c.shape[1]
    return pl.pallas_call(
        _outproj_router_kernel, grid=(T // TM,),
        in_specs=[pl.BlockSpec((TM, na), row), pl.BlockSpec((TM, nb), row), pl.BlockSpec((TM, nc), row),
                  pl.BlockSpec((None, na + nb + nc, D_MODEL), per_layer, pipeline_mode=pl.Buffered(1)),
                  pl.BlockSpec((TM, D_MODEL), row),
                  pl.BlockSpec((1, D_MODEL), const),
                  pl.BlockSpec((None, D_MODEL, 2 * ROUTER_ROWS), per_layer),
                  pl.BlockSpec((None, ROUTER_ROWS, 1), per_layer)],
        out_specs=[pl.BlockSpec((TM, D_MODEL), row), pl.BlockSpec((TM * XPACK_ROWS, LANE), row),
                   pl.BlockSpec((TOP_K, TM), lambda i: (0, i)), pl.BlockSpec((TOP_K, TM), lambda i: (0, i))],
        out_shape=[jax.ShapeDtypeStruct((T, D_MODEL), f32), jax.ShapeDtypeStruct((T * XPACK_ROWS, LANE), jnp.uint32),
                   jax.ShapeDtypeStruct((TOP_K, T), i32), jax.ShapeDtypeStruct((TOP_K, T), f32)],
        compiler_params=_cparams(1), name="outproj_router")(a, b, c, wo, h, g, wr, br)


def _moe_kernel(src_ref, dst_ref, cstart_ref, nvalid_ref, be_ref, wslot_ref, enext_ref, nused_ref,
                x_hbm, wg_hbm, wu_hbm, wd_hbm, y_hbm, xbuf, ybuf, wgbuf, wubuf, wdbuf, gsem, ssem, wsem, *, layer):
    b = pl.program_id(0)
    slot = b % 2
    n_used = nused_ref[0]

    def weight_copies(e, buf):
        return [pltpu.make_async_copy(hbm.at[layer, e], vbuf.at[buf], wsem.at[buf])
                for hbm, vbuf in ((wg_hbm, wgbuf), (wu_hbm, wubuf), (wd_hbm, wdbuf))]

    def gather_row(base, buf, r):
        return pltpu.make_async_copy(x_hbm.at[pl.ds(src_ref[base + r] * XPACK_ROWS, XPACK_ROWS)],
                                     xbuf.at[buf, pl.ds(r * XPACK_ROWS, XPACK_ROWS)], gsem.at[buf])

    def scatter_row(base, buf, r):
        return pltpu.make_async_copy(ybuf.at[buf, pl.ds(r * SLAB_ROWS, SLAB_ROWS)],
                                     y_hbm.at[pl.ds(dst_ref[base + r] * SLAB_ROWS, SLAB_ROWS)], ssem.at[buf])

    def start_group(make_row, base, buf, g, queues):
        for u in range(8):
            make_row(base, buf, 8 * g + u).start(priority=u % queues)

    def start_gather(blk, buf):
        base = cstart_ref[blk]

        def group(g, carry):
            start_group(gather_row, base, buf, g, 1)
            return carry
        lax.fori_loop(0, MOE_BLK // 8, group, 0)

    def start_scatter(blk, buf):
        base, nv = cstart_ref[blk], nvalid_ref[blk]

        def group(g, carry):
            start_group(scatter_row, base, buf, g, 2)
            return carry
        lax.fori_loop(0, nv >> 3, group, 0)

        def single(r, carry):
            scatter_row(base, buf, r).start(priority=0)
            return carry
        lax.fori_loop(nv & ~7, nv, single, 0)

    def wait_gather(buf):
        pltpu.make_async_copy(x_hbm.at[pl.ds(0, MOE_BLK * XPACK_ROWS)], xbuf.at[buf], gsem.at[buf]).wait()

    def wait_scatter(blk, buf):
        nv = nvalid_ref[blk]
        rows = MOE_BLK
        while rows >= 1:
            @pl.when((nv & rows) != 0)
            def _(rows=rows):
                n = rows * SLAB_ROWS
                pltpu.make_async_copy(ybuf.at[buf, pl.ds(0, n)], y_hbm.at[pl.ds(0, n)], ssem.at[buf]).wait()
            rows //= 2

    def per_slot(fn):
        for buf in range(2):
            pl.when(slot == buf)(functools.partial(fn, buf))

    @pl.when(b == 0)
    def _():
        start_gather(0, 0)
        for cp in weight_copies(be_ref[0], wslot_ref[0]):
            cp.start(priority=1)

    ws = wslot_ref[b]
    first_of_expert = jnp.logical_or(b == 0, be_ref[jnp.maximum(b - 1, 0)] != be_ref[b])

    @pl.when(jnp.logical_and(b < n_used, first_of_expert))
    def _():
        @pl.when(enext_ref[b] >= 0)
        def _():
            for cp in weight_copies(enext_ref[b], 1 - ws):
                cp.start(priority=1)
        for cp in weight_copies(be_ref[b], ws):
            cp.wait()

    @pl.when(b + 1 < n_used)
    def _():
        per_slot(lambda buf: start_gather(b + 1, 1 - buf))

    @pl.when(b < n_used)
    def _():
        wait_gather(slot)

        @pl.when(b >= 2)
        def _():
            wait_scatter(b - 2, slot)

        x = _unpack_bf16_pairs(_unslab(xbuf.at[slot], MOE_BLK, XPACK_ROWS))
        hg = jnp.dot(x, wgbuf[ws], preferred_element_type=f32)
        hu = jnp.dot(x, wubuf[ws], preferred_element_type=f32)
        hid = hg * jax.nn.sigmoid(hg) * hu
        _to_slab(ybuf.at[slot], jnp.dot(hid, wdbuf[ws], preferred_element_type=f32))
        per_slot(lambda buf: start_scatter(b, buf))

    @pl.when(b == n_used - 1)
    def _():
        wait_scatter(b, slot)

        @pl.when(b >= 1)
        def _():
            wait_scatter(b - 1, 1 - slot)


def _moe(xn, tables, w_gate, w_up, w_down, layer):
    T = xn.shape[0] // XPACK_ROWS
    n_blocks = tables[4].shape[0]
    anywhere = pl.BlockSpec(memory_space=pl.ANY)
    grid_spec = pltpu.PrefetchScalarGridSpec(
        num_scalar_prefetch=len(tables), grid=(n_blocks,),
        in_specs=[anywhere, anywhere, anywhere, anywhere],
        out_specs=anywhere,
        scratch_shapes=[pltpu.VMEM((2, MOE_BLK * XPACK_ROWS, LANE), jnp.uint32),
                        pltpu.VMEM((2, MOE_BLK * SLAB_ROWS, LANE), f32),
                        pltpu.VMEM((2, D_MODEL, EXPERT_HIDDEN), f32), pltpu.VMEM((2, D_MODEL, EXPERT_HIDDEN), f32),
                        pltpu.VMEM((2, EXPERT_HIDDEN, D_MODEL), f32),
                        pltpu.SemaphoreType.DMA((2,)), pltpu.SemaphoreType.DMA((2,)), pltpu.SemaphoreType.DMA((2,))])
    return pl.pallas_call(
        functools.partial(_moe_kernel, layer=layer), grid_spec=grid_spec,
        out_shape=jax.ShapeDtypeStruct((TOP_K * T * SLAB_ROWS, LANE), f32),
        compiler_params=_cparams(1), name="moe")(*tables, xn, w_gate, w_up, w_down)


def _dispatch(eid, T):
    n_assign = TOP_K * T
    n_blocks = n_assign // MOE_BLK + N_EXPERTS
    e_flat = eid.T.reshape(-1)
    keys = jnp.sort(e_flat * n_assign + jnp.arange(n_assign, dtype=i32))
    asg = jnp.concatenate([keys % n_assign, jnp.zeros((MOE_BLK,), i32)])
    src, dst = asg // TOP_K, (asg % TOP_K) * T + asg // TOP_K
    edges = jnp.arange(N_EXPERTS + 1, dtype=i32) * n_assign
    bounds = jnp.sum(keys[None, :] < edges[:, None], axis=1, dtype=i32)
    start, counts = bounds[:-1], bounds[1:] - bounds[:-1]
    nblk = (counts + MOE_BLK - 1) // MOE_BLK
    bend = jnp.cumsum(nblk)
    n_used = bend[-1]
    b = jnp.arange(n_blocks, dtype=i32)
    be = jnp.minimum(jnp.sum(bend[None, :] <= jnp.minimum(b, n_used - 1)[:, None], axis=1, dtype=i32), N_EXPERTS - 1)
    j = b - (bend - nblk)[be]
    used = b < n_used
    cstart = jnp.where(used, start[be] + j * MOE_BLK, 0).astype(i32)
    nvalid = jnp.where(used, jnp.clip(counts[be] - j * MOE_BLK, 0, MOE_BLK), 0).astype(i32)
    used_e = counts > 0
    rank = jnp.cumsum(used_e.astype(i32)) - 1
    ids = jnp.arange(N_EXPERTS, dtype=i32)
    later = jnp.where(used_e[None, :] & (ids[None, :] > ids[:, None]), ids[None, :], N_EXPERTS)
    nxt = jnp.min(later, axis=1)
    enext = jnp.where(nxt < N_EXPERTS, nxt, -1).astype(i32)
    return src, dst, cstart, nvalid, be, (rank[be] % 2).astype(i32), enext[be], n_used.reshape(1).astype(i32)


def _final_kernel(h_ref, y0_ref, y1_ref, gate_ref, g_ref, o_ref):
    o_ref[...] = _rms(_combine(h_ref, y0_ref, y1_ref, gate_ref), g_ref[...])


def _final(h, moe_out, g):
    T = h.shape[0]
    y, gate = moe_out
    return pl.pallas_call(
        _final_kernel, grid=(T // TM,),
        in_specs=_combine_specs(T) + [pl.BlockSpec((1, D_MODEL), lambda i: (0, 0))],
        out_specs=pl.BlockSpec((TM, D_MODEL), lambda i: (i, 0)),
        out_shape=jax.ShapeDtypeStruct((T, D_MODEL), f32),
        compiler_params=_cparams(1), name="final_norm")(h, y, y, gate, g)


def _t5_bucket(n):
    nf = jnp.maximum(n, 1).astype(f32)
    large = REL_MAX_EXACT + (jnp.log(nf / REL_MAX_EXACT) / math.log(REL_MAX_DISTANCE / REL_MAX_EXACT)
                             * (REL_BUCKETS - REL_MAX_EXACT)).astype(i32)
    return jnp.where(n < REL_MAX_EXACT, n, jnp.minimum(large, REL_BUCKETS - 1))


def _split_hi_lo(w):
    hi = w.astype(bf16)
    return hi, (w - hi.astype(f32)).astype(bf16)


def kernel(x, positions, attn_norm, w_in, pool_w, pool_scale, mla_q_norm, mla_w_uq, mla_kv_norm, mla_w_ukv,
           diff_lambda, diff_subln, rel_bias, w_out, ffn_norm, router_group_w, router_group_b,
           router_expert_w, router_expert_b, expert_w_gate, expert_w_up, expert_w_down, final_norm):
    B, S, D = x.shape
    T = B * S
    depth = w_in.shape[0]
    assert D == D_MODEL and S % TQ == 0 and T % TM == 0 and (TOP_K * T) % MOE_BLK == 0

    inv = 1.0 / (ROPE_THETA ** (jnp.arange(0, MLA_ROPE, 2, dtype=f32) / MLA_ROPE))
    ang = positions.astype(f32).reshape(T, 1) * inv
    zeros = jnp.zeros((T, LANE - MLA_ROPE), f32)
    cos = jnp.concatenate([jnp.cos(ang), jnp.cos(ang), zeros], axis=1)
    sin = jnp.concatenate([jnp.sin(ang), jnp.sin(ang), zeros], axis=1)
    cost, sint = jnp.cos(ang).T, jnp.sin(ang).T
    assert REL_CLIP == LANE - 1
    tab = jnp.pad(rel_bias[_t5_bucket(jnp.arange(LANE, dtype=i32))].T.astype(f32) * LOG2E,
                  ((0, 8 - DIFF_HEADS), (0, 0)))

    c_kr, c_dq = COL_KR + MLA_ROPE, COL_KR + MLA_ROPE + COL_DK - COL_DQ
    w_in_p = jnp.concatenate([w_in[:, :, :c_kr], jnp.zeros((depth, D, LANE - MLA_ROPE), f32),
                              w_in[:, :, c_kr:c_dq] * (DIFF_QK ** -0.5 * LOG2E), w_in[:, :, c_dq:]],
                             axis=2).astype(bf16)
    wq = mla_w_uq.reshape(depth, MLA_Q_RANK, MLA_HEADS, MLA_NOPE + MLA_ROPE)
    wq = jnp.pad(wq, ((0, 0), (0, 0), (0, 0), (0, QK_PAD - MLA_NOPE - MLA_ROPE)))
    wqt = wq.reshape(depth, MLA_Q_RANK, -1).transpose(0, 2, 1).astype(bf16)
    wkv = mla_w_ukv.reshape(depth, MLA_KV_RANK, MLA_HEADS, MLA_NOPE + MLA_V)
    wk = wkv[..., :MLA_NOPE].reshape(depth, MLA_KV_RANK, -1).astype(bf16)
    wvt = wkv[..., MLA_NOPE:].reshape(depth, MLA_KV_RANK, -1).transpose(0, 2, 1).astype(bf16)
    wo = w_out.astype(bf16)
    pool_wb = pool_w.astype(bf16)
    pad_g = jnp.zeros((depth, D, 8 - N_GROUPS), f32)
    pad_e = jnp.zeros((depth, D, ROUTER_ROWS - 8 - N_EXPERTS), f32)
    wr_hi, wr_lo = _split_hi_lo(jnp.concatenate([router_group_w, pad_g, router_expert_w, pad_e], axis=2))
    wr_cat = jnp.concatenate([wr_hi, wr_lo], axis=2)
    br = jnp.concatenate([router_group_b, pad_g[:, 0], router_expert_b, pad_e[:, 0]], axis=1)[:, :, None]

    h = x.reshape(T, D)
    moe_out = None
    for l in range(depth):
        lam_init = 0.8 - 0.6 * math.exp(-0.3 * l)
        proj, h = _inproj(h, moe_out, attn_norm[l].reshape(1, D), w_in_p, l)
        a_out = _pool(proj, pool_wb, pool_scale[l].reshape(1, POOL_WIDTH), l, B, S)
        qt, k, vt = _mla_prep(proj, cos, sin, cost, sint, mla_q_norm[l].reshape(1, -1), mla_kv_norm[l].reshape(1, -1),
                              wqt, wk, wvt, l, B, S)
        b_out = _mla_attn(qt, k, vt, B, S)
        c_out = _diff_attn(proj, diff_lambda[l], positions, tab, diff_subln[l].reshape(-1, 1), lam_init, B, S)
        h, xn, eid, gate = _outproj_router(a_out, b_out, c_out, wo, h, ffn_norm[l].reshape(1, D), wr_cat, br, l)
        y = _moe(xn, _dispatch(eid, T), expert_w_gate, expert_w_up, expert_w_down, l)
        moe_out = (y, gate.T)
    out = _final(h, moe_out, final_norm.reshape(1, D))
    return out.reshape(B, S, D)
```

```python
import functools
import math

import jax
import jax.numpy as jnp
from jax import lax
from jax.experimental import pallas as pl
from jax.experimental.pallas import tpu as pltpu

f32, bf16, i32 = jnp.float32, jnp.bfloat16, jnp.int32

D_MODEL = 2048
POOL_WINDOWS = (2, 4, 8, 16)
POOL_GROUPS = 4
POOL_WIDTH = 512
MLA_NOPE, MLA_ROPE, MLA_V, MLA_HEADS = 128, 64, 128, 6
MLA_Q_RANK, MLA_KV_RANK = 512, 256
ROPE_THETA = 10000.0
DIFF_QK, DIFF_V, DIFF_HEADS = 64, 128, 6
REL_BUCKETS, REL_MAX_EXACT, REL_MAX_DISTANCE = 32, 16, 128
N_GROUPS, EXPERTS_PER_GROUP, N_EXPERTS, TOP_K, EXPERT_HIDDEN = 4, 8, 32, 2, 512
NORM_EPS = 1e-6

LANE = 128
MXU_DIM = 256
VMEM_LIMIT = 56 * 1024 * 1024

N_IN_PAD = 3712
COL_CQ, COL_CKV, COL_KR, COL_DQ, COL_DK, COL_DV = 512, 1024, 1280, 1408, 2176, 2944
QK_PAD = 2 * LANE

TM = 256
TM_OUT = 512
TQ = 512
TK = 256
MOE_BLK = 256
SLAB_ROWS = D_MODEL // LANE
REL_CLIP = REL_MAX_DISTANCE - 1
REL_FAR = REL_MAX_DISTANCE
LOG2E = math.log2(math.e)


def _cparams(n_axes):
    return pltpu.CompilerParams(dimension_semantics=("arbitrary",) * n_axes, vmem_limit_bytes=VMEM_LIMIT)


def _rms(x, g):
    return x * lax.rsqrt(jnp.mean(x * x, axis=-1, keepdims=True) + NORM_EPS) * g


def _unslab(ref, rows, per_token=SLAB_ROWS):
    return jnp.concatenate([ref[pl.ds(s, rows, stride=per_token), :] for s in range(per_token)], axis=1)


def _to_slab(ref, val):
    per_token = val.shape[1] // LANE
    for s in range(per_token):
        ref[pl.ds(s, val.shape[0], stride=per_token), :] = val[:, s * LANE:(s + 1) * LANE]


XPACK_ROWS = SLAB_ROWS // 2


def _pack_bf16_pairs(x):
    bits = lax.bitcast_convert_type(x.astype(bf16).astype(f32), jnp.uint32)
    half = x.shape[1] // 2
    return bits[:, half:] | (bits[:, :half] >> 16)


def _unpack_bf16_pairs(w):
    lo = lax.bitcast_convert_type(w << 16, f32)
    hi = lax.bitcast_convert_type(w & jnp.uint32(0xFFFF0000), f32)
    return jnp.concatenate([lo, hi], axis=1)


def _combine(h_ref, y0_ref, y1_ref, gate_ref):
    gate = gate_ref[...]
    return h_ref[...] + _unslab(y0_ref, TM) * gate[:, 0:1] + _unslab(y1_ref, TM) * gate[:, 1:2]


def _combine_specs(T):
    plane1 = T // TM
    return [pl.BlockSpec((TM, D_MODEL), lambda i: (i, 0)),
            pl.BlockSpec((TM * SLAB_ROWS, LANE), lambda i: (i, 0)),
            pl.BlockSpec((TM * SLAB_ROWS, LANE), lambda i: (plane1 + i, 0)),
            pl.BlockSpec((TM, TOP_K), lambda i: (i, 0))]


def _inproj_kernel(*refs, combine):
    if combine:
        h_ref, y0_ref, y1_ref, gate_ref, g_ref, w_ref, proj_ref, hout_ref = refs
        h = _combine(h_ref, y0_ref, y1_ref, gate_ref)
        hout_ref[...] = h
    else:
        h_ref, g_ref, w_ref, proj_ref = refs
        h = h_ref[...]
    u = _rms(h, g_ref[...]).astype(bf16)
    for n0 in range(0, N_IN_PAD, 512):
        n1 = min(n0 + 512, N_IN_PAD)
        proj_ref[:, n0:n1] = jnp.dot(u, w_ref[:, n0:n1], preferred_element_type=f32).astype(bf16)


def _inproj(h, moe_out, g, w, layer):
    T = h.shape[0]
    combine = moe_out is not None
    row = lambda i: (i, 0)
    const = lambda i: (0, 0)
    if combine:
        y, gate = moe_out
        in_specs = _combine_specs(T)
        args = [h, y, y, gate]
    else:
        in_specs = [pl.BlockSpec((TM, D_MODEL), row)]
        args = [h]
    in_specs += [pl.BlockSpec((1, D_MODEL), const),
                 pl.BlockSpec((None, D_MODEL, N_IN_PAD), lambda i: (layer, 0, 0), pipeline_mode=pl.Buffered(1))]
    args += [g, w]
    out_shape = [jax.ShapeDtypeStruct((T, N_IN_PAD), bf16)]
    out_specs = [pl.BlockSpec((TM, N_IN_PAD), row)]
    if combine:
        out_shape.append(jax.ShapeDtypeStruct((T, D_MODEL), f32))
        out_specs.append(pl.BlockSpec((TM, D_MODEL), row))
    res = pl.pallas_call(
        functools.partial(_inproj_kernel, combine=combine),
        grid=(T // TM,), in_specs=in_specs, out_specs=out_specs, out_shape=out_shape,
        compiler_params=_cparams(1), name="inproj_c" if combine else "inproj")(*args)
    return (res[0], res[1]) if combine else (res[0], h)


def _pool_kernel(x_ref, w_ref, sc_ref, o_ref):
    g = pl.program_id(1)
    x = x_ref[...].astype(f32)
    t = lax.broadcasted_iota(i32, x.shape, 0)

    def lagged(v, k):
        return jnp.where(t >= k, pltpu.roll(v, k, axis=0), 0.0)

    s2 = x + lagged(x, 1)
    s4 = s2 + lagged(s2, 2)
    s8 = s4 + lagged(s4, 4)
    s16 = s8 + lagged(s8, 8)
    wsum = jnp.where(g == 0, s2, jnp.where(g == 1, s4, jnp.where(g == 2, s8, s16)))
    win = jnp.where(g == 0, 2, jnp.where(g == 1, 4, jnp.where(g == 2, 8, 16)))
    cnt = jnp.minimum(t + 1, win).astype(f32)
    d = (wsum / cnt - x).astype(bf16)
    y = jnp.dot(d, w_ref[...], preferred_element_type=f32) * sc_ref[...]
    o_ref[...] = y.astype(bf16)


def _pool(proj, w, scale, layer, B, S):
    T = B * S
    return pl.pallas_call(
        _pool_kernel, grid=(B, POOL_GROUPS),
        in_specs=[pl.BlockSpec((S, LANE), lambda b, g: (b, g)),
                  pl.BlockSpec((None, None, LANE, LANE), lambda b, g: (layer, g, 0, 0)),
                  pl.BlockSpec((1, LANE), lambda b, g: (0, g))],
        out_specs=pl.BlockSpec((S, LANE), lambda b, g: (b, g)),
        out_shape=jax.ShapeDtypeStruct((T, POOL_WIDTH), bf16),
        compiler_params=_cparams(2), name="pool")(proj, w, scale)


VT_ROWS = MLA_V + 16


def _nt_dot(a, b):
    return lax.dot_general(a, b, (((1,), (1,)), ((), ())), preferred_element_type=f32)


def _rope128(r, cos, sin):
    lane = lax.broadcasted_iota(i32, r.shape, 1)
    half = MLA_ROPE // 2
    partner = jnp.where(lane < half, -pltpu.roll(r, LANE - half, axis=1), pltpu.roll(r, half, axis=1))
    return r * cos + partner * sin


def _ones_row_pad(n_cols):
    row = lax.broadcasted_iota(i32, (VT_ROWS - MLA_V, n_cols), 0)
    return jnp.where(row == 0, 1.0, 0.0).astype(bf16)


def _mla_prep_kernel(cq_ref, ckv_ref, kr_ref, cos_ref, sin_ref, cost_ref, sint_ref, qn_ref, kvn_ref,
                     wqt_ref, wk_ref, wvt_ref, qt_ref, k_ref, vt_ref):
    scale = (MLA_NOPE + MLA_ROPE) ** -0.5 * LOG2E
    half = MLA_ROPE // 2
    cost, sint = cost_ref[...], sint_ref[...]
    cq = _rms(cq_ref[...].astype(f32), qn_ref[...]).astype(bf16)
    ckv = _rms(ckv_ref[...].astype(f32), kvn_ref[...]).astype(bf16)
    kr = _rope128(kr_ref[...].astype(f32), cos_ref[...], sin_ref[...]).astype(bf16)
    pad = _ones_row_pad(TM)
    for h in range(MLA_HEADS):
        r0 = h * QK_PAD
        qt = _nt_dot(wqt_ref[r0:r0 + QK_PAD, :], cq) * scale
        x1, x2 = qt[MLA_NOPE:MLA_NOPE + half], qt[MLA_NOPE + half:MLA_NOPE + MLA_ROPE]
        qt_ref[r0:r0 + MLA_NOPE, :] = qt[:MLA_NOPE].astype(bf16)
        qt_ref[r0 + MLA_NOPE:r0 + MLA_NOPE + half, :] = (x1 * cost - x2 * sint).astype(bf16)
        qt_ref[r0 + MLA_NOPE + half:r0 + MLA_NOPE + MLA_ROPE, :] = (x1 * sint + x2 * cost).astype(bf16)
        qt_ref[r0 + MLA_NOPE + MLA_ROPE:r0 + QK_PAD, :] = qt[MLA_NOPE + MLA_ROPE:].astype(bf16)
        kn = jnp.dot(ckv, wk_ref[:, h * LANE:(h + 1) * LANE], preferred_element_type=f32)
        k_ref[:, r0:r0 + LANE] = kn.astype(bf16)
        k_ref[:, r0 + LANE:r0 + QK_PAD] = kr
        vt = _nt_dot(wvt_ref[h * MLA_V:(h + 1) * MLA_V, :], ckv)
        vt_ref[h, 0:MLA_V, :] = vt.astype(bf16)
        vt_ref[h, MLA_V:VT_ROWS, :] = pad


def _mla_prep(proj, cos, sin, cost, sint, qn, kvn, wqt, wk, wvt, layer, B, S):
    T = B * S
    assert TM == TK
    nk = S // TK
    const = lambda i: (0, 0)
    per_layer = lambda i: (layer, 0, 0)
    nq, nkc = MLA_HEADS * QK_PAD, MLA_HEADS * LANE
    return pl.pallas_call(
        _mla_prep_kernel, grid=(T // TM,),
        in_specs=[pl.BlockSpec((TM, MLA_Q_RANK), lambda i: (i, COL_CQ // MLA_Q_RANK)),
                  pl.BlockSpec((TM, MLA_KV_RANK), lambda i: (i, COL_CKV // MLA_KV_RANK)),
                  pl.BlockSpec((TM, LANE), lambda i: (i, COL_KR // LANE)),
                  pl.BlockSpec((TM, LANE), lambda i: (i, 0)),
                  pl.BlockSpec((TM, LANE), lambda i: (i, 0)),
                  pl.BlockSpec((MLA_ROPE // 2, TM), lambda i: (0, i)),
                  pl.BlockSpec((MLA_ROPE // 2, TM), lambda i: (0, i)),
                  pl.BlockSpec((1, MLA_Q_RANK), const),
                  pl.BlockSpec((1, MLA_KV_RANK), const),
                  pl.BlockSpec((None, nq, MLA_Q_RANK), per_layer),
                  pl.BlockSpec((None, MLA_KV_RANK, nkc), per_layer),
                  pl.BlockSpec((None, MLA_HEADS * MLA_V, MLA_KV_RANK), per_layer)],
        out_specs=[pl.BlockSpec((nq, TM), lambda i: (0, i)),
                   pl.BlockSpec((TM, nq), lambda i: (i, 0)),
                   pl.BlockSpec((None, MLA_HEADS, None, VT_ROWS, TK), lambda i: (i // nk, 0, i % nk, 0, 0))],
        out_shape=[jax.ShapeDtypeStruct((nq, T), bf16), jax.ShapeDtypeStruct((T, nq), bf16),
                   jax.ShapeDtypeStruct((B, MLA_HEADS, nk, VT_ROWS, TK), bf16)],
        compiler_params=_cparams(1), name="mla_prep")(proj, proj, proj, cos, sin, cost, sint, qn, kvn, wqt, wk, wvt)


def _softmax_probs(s, m, shift, lo=0):
    m_old = m[:, lo:]
    m_new = jnp.maximum(m_old, jnp.max(s, axis=0, keepdims=True) + shift)
    alpha = jnp.exp2(m_old - m_new)
    p = jnp.exp2(s - (m_new - shift)).astype(bf16)
    return (m_new if lo == 0 else jnp.concatenate([m[:, :lo], m_new], axis=1)), alpha, p


def _accumulate(acc_ref, idx, alpha, vt, p, lo=0):
    acc_ref[idx, :, lo:] = alpha * acc_ref[idx, :, lo:] + jnp.dot(vt, p, preferred_element_type=f32)


def _attn_out_t(acc):
    return acc[0:MLA_V] * (1.0 / acc[MLA_V:MLA_V + 1])


def _diag_visible(lo):
    krow = lax.broadcasted_iota(i32, (TK, TQ - lo), 0)
    qcol = lax.broadcasted_iota(i32, (TK, TQ - lo), 1)
    return krow <= qcol


MLA_HEADS_PER_STEP = 6


def _mla_attn_kernel(qt_ref, k_ref, vt_ref, o_ref, acc_ref, p_ref):
    i = pl.program_id(2)
    acc_ref[...] = jnp.zeros_like(acc_ref)
    p_ref[...] = jnp.zeros_like(p_ref)
    n_full = i * (TQ // TK)
    heads = range(MLA_HEADS_PER_STEP)

    def scores(c, j, lo=0):
        k = k_ref[pl.ds(pl.multiple_of(j * TK, TK), TK), c * QK_PAD:(c + 1) * QK_PAD]
        return jnp.dot(k, qt_ref[c * QK_PAD:(c + 1) * QK_PAD, lo:], preferred_element_type=f32)

    def flush(alphas, j):
        for c in heads:
            _accumulate(acc_ref, c, alphas[c], vt_ref[c, j], p_ref[c])

    def full(j, carry):
        ms, alphas = carry
        flush(alphas, jnp.maximum(j - 1, 0))
        new = [_softmax_probs(scores(c, j), ms[c], 0.0) for c in heads]
        for c in heads:
            p_ref[c] = new[c][2]
        return tuple(n[0] for n in new), tuple(n[1] for n in new)

    minf = jnp.full((1, TQ), -jnp.inf, f32)
    one = jnp.ones((1, TQ), f32)
    ms, alphas = lax.fori_loop(0, n_full, full, ((minf,) * MLA_HEADS_PER_STEP, (one,) * MLA_HEADS_PER_STEP))
    flush(alphas, jnp.maximum(n_full - 1, 0))
    for d in range(TQ // TK):
        j, lo = n_full + d, d * TK
        for c in heads:
            m, alpha, p = _softmax_probs(jnp.where(_diag_visible(lo), scores(c, j, lo), -jnp.inf), ms[c], 0.0, lo)
            _accumulate(acc_ref, c, alpha, vt_ref[c, j], p, lo)
            ms = ms[:c] + (m,) + ms[c + 1:]
    for c in heads:
        o_ref[:, c * MLA_V:(c + 1) * MLA_V] = _attn_out_t(acc_ref[c]).T.astype(bf16)


def _mla_attn(qt, k, vt, B, S):
    T = B * S
    nq, nk = S // TQ, S // TK
    hs = MLA_HEADS_PER_STEP
    return pl.pallas_call(
        _mla_attn_kernel, grid=(B, MLA_HEADS // hs, nq),
        in_specs=[pl.BlockSpec((hs * QK_PAD, TQ), lambda b, h, i: (h, b * nq + i)),
                  pl.BlockSpec((S, hs * QK_PAD), lambda b, h, i: (b, h)),
                  pl.BlockSpec((None, hs, nk, VT_ROWS, TK), lambda b, h, i: (b, h, 0, 0, 0))],
        out_specs=pl.BlockSpec((TQ, hs * MLA_V), lambda b, h, i: (b * nq + i, h)),
        out_shape=jax.ShapeDtypeStruct((T, MLA_HEADS * MLA_V), bf16),
        scratch_shapes=[pltpu.VMEM((hs, VT_ROWS, TQ), f32), pltpu.VMEM((hs, TK, TQ), bf16)],
        compiler_params=_cparams(3), name="mla_attn")(qt, k, vt)


DIFF_HEADS_PER_STEP = 6


def _diff_attn_kernel(pmin_ref, kmax_ref, lam_ref, *refs, lam_init, nq, nk):
    hs = DIFF_HEADS_PER_STEP
    q_refs, k_refs, v_refs = refs[:hs], refs[hs:2 * hs], refs[2 * hs:3 * hs]
    posq_ref, posk_ref, tab_ref, sub_ref, o_ref, vt_ref, acc_ref, p_ref = refs[3 * hs:]
    b, hp, i = pl.program_id(0), pl.program_id(1), pl.program_id(2)
    heads = range(hs)
    chains = [(c, m) for c in heads for m in range(2)]

    @pl.when(i == 0)
    def _():
        pad = _ones_row_pad(TK)
        for c in heads:
            for j in range(nk):
                vt_ref[c, j, 0:DIFF_V, :] = v_refs[c][j * TK:(j + 1) * TK, :].astype(f32).T.astype(bf16)
                vt_ref[c, j, DIFF_V:VT_ROWS, :] = pad

    lv = lam_ref[...]
    lam = (jnp.exp(jnp.sum(lv[0:1] * lv[1:2], axis=1, keepdims=True))
           - jnp.exp(jnp.sum(lv[2:3] * lv[3:4], axis=1, keepdims=True)) + lam_init)
    frow = lax.broadcasted_iota(i32, (2 * DIFF_QK, TQ), 0)
    qm = []
    for c in heads:
        qt = q_refs[c][...].astype(f32).T
        qm.append((jnp.where(frow < DIFF_QK, qt, 0.0).astype(bf16), jnp.where(frow >= DIFF_QK, qt, 0.0).astype(bf16)))
    acc_ref[...] = jnp.zeros_like(acc_ref)
    posq = posq_ref[...]
    tabs = [jnp.broadcast_to(tab_ref[pl.ds(hp * hs + c, 1), :], (TK, LANE)) for c in heads]
    far_bias = [tab_ref[pl.ds(hp * hs + c, 1), REL_CLIP:REL_CLIP + 1] for c in heads]
    n_full = i * (TQ // TK)

    def scores(c, j, lo):
        off = pl.multiple_of(j * TK, TK)
        q_both = jnp.concatenate([qm[c][0][:, lo:], qm[c][1][:, lo:]], axis=1)
        return jnp.dot(k_refs[c][pl.ds(off, TK), :], q_both, preferred_element_type=f32)

    def distances(j, lo):
        off = pl.multiple_of(j * TK, TK)
        return jnp.clip(posq[:, lo:] - posk_ref[pl.ds(off, TK), :], 0, REL_CLIP)

    def bias_tile(c, n):
        return jnp.concatenate(
            [jnp.take_along_axis(tabs[c], n[:, t * LANE:(t + 1) * LANE], axis=1) for t in range(n.shape[1] // LANE)],
            axis=1)

    def flush(alphas, j):
        for c, m in chains:
            _accumulate(acc_ref, 2 * c + m, alphas[2 * c + m], vt_ref[c, j], p_ref[2 * c + m])

    def softmax_tile(j, ms, far_tile, lo=0, visible=None):
        n = None if far_tile else distances(j, lo)
        out = []
        for c in heads:
            s2 = scores(c, j, lo)
            bias = None if far_tile else bias_tile(c, n)
            for m in range(2):
                s = s2[:, m * (TQ - lo):(m + 1) * (TQ - lo)]
                if not far_tile:
                    s = s + bias
                if visible is not None:
                    s = jnp.where(visible, s, -jnp.inf)
                out.append(_softmax_probs(s, ms[2 * c + m], far_bias[c] if far_tile else 0.0, lo))
        return out

    def below_diagonal(far_tile, j, carry):
        ms, alphas = carry
        flush(alphas, jnp.maximum(j - 1, 0))
        new = softmax_tile(j, ms, far_tile)
        for idx in range(len(chains)):
            p_ref[idx] = new[idx][2]
        return tuple(n[0] for n in new), tuple(n[1] for n in new)

    def body(j, carry):
        is_far = pmin_ref[b * nq + i] - kmax_ref[b * nk + j] >= REL_FAR
        return lax.cond(is_far, functools.partial(below_diagonal, True), functools.partial(below_diagonal, False),
                        j, carry)

    p_ref[...] = jnp.zeros_like(p_ref)
    minf = jnp.full((1, TQ), -jnp.inf, f32)
    one = jnp.ones((1, TQ), f32)
    ms, alphas = lax.fori_loop(0, n_full, body, ((minf,) * len(chains), (one,) * len(chains)))
    flush(alphas, jnp.maximum(n_full - 1, 0))
    for d in range(TQ // TK):
        j, lo = n_full + d, d * TK
        new = softmax_tile(j, ms, False, lo, _diag_visible(lo))
        for c, m in chains:
            _accumulate(acc_ref, 2 * c + m, new[2 * c + m][1], vt_ref[c, j], new[2 * c + m][2], lo)
        ms = tuple(n[0] for n in new)
    gain = sub_ref[...] * (1.0 - lam_init)
    for c in heads:
        ot = _attn_out_t(acc_ref[2 * c]) - lam * _attn_out_t(acc_ref[2 * c + 1])
        ot = ot * lax.rsqrt(jnp.mean(ot * ot, axis=0, keepdims=True) + NORM_EPS) * gain
        o_ref[:, c * DIFF_V:(c + 1) * DIFF_V] = ot.T.astype(bf16)


def _diff_attn(proj, lam_vecs, positions, tab, subln, lam_init, B, S):
    T = B * S
    nq, nk = S // TQ, S // TK
    hs = DIFF_HEADS_PER_STEP
    pmin = positions.reshape(B * nq, TQ).min(axis=1)
    kmax = positions.reshape(B * nk, TK).max(axis=1)

    def head_specs(rows, col0, by_tile):
        def spec(c):
            if by_tile:
                return pl.BlockSpec((rows, LANE), lambda b, hp, i, *_: (b * nq + i, col0 // LANE + hp * hs + c))
            return pl.BlockSpec((rows, LANE), lambda b, hp, i, *_: (b, col0 // LANE + hp * hs + c))
        return [spec(c) for c in range(hs)]

    grid_spec = pltpu.PrefetchScalarGridSpec(
        num_scalar_prefetch=2, grid=(B, DIFF_HEADS // hs, nq),
        in_specs=([pl.BlockSpec((4, DIFF_QK), lambda b, hp, i, *_: (0, 0))]
                  + head_specs(TQ, COL_DQ, True) + head_specs(S, COL_DK, False) + head_specs(S, COL_DV, False)
                  + [pl.BlockSpec((1, TQ), lambda b, hp, i, *_: (0, b * nq + i)),
                     pl.BlockSpec((S, 1), lambda b, hp, i, *_: (b, 0)),
                     pl.BlockSpec((8, LANE), lambda b, hp, i, *_: (0, 0)),
                     pl.BlockSpec((DIFF_V, 1), lambda b, hp, i, *_: (0, 0))]),
        out_specs=pl.BlockSpec((TQ, hs * DIFF_V), lambda b, hp, i, *_: (b * nq + i, hp)),
        scratch_shapes=[pltpu.VMEM((hs, nk, VT_ROWS, TK), bf16), pltpu.VMEM((2 * hs, VT_ROWS, TQ), f32),
                        pltpu.VMEM((2 * hs, TK, TQ), bf16)])
    return pl.pallas_call(
        functools.partial(_diff_attn_kernel, lam_init=lam_init, nq=nq, nk=nk), grid_spec=grid_spec,
        out_shape=jax.ShapeDtypeStruct((T, DIFF_HEADS * DIFF_V), bf16),
        compiler_params=_cparams(3), name="diff_attn")(
            pmin, kmax, lam_vecs, *([proj] * (3 * hs)), positions.reshape(1, T), positions.reshape(T, 1), tab, subln)


ROUTER_ROWS = 64


def _outproj_router_kernel(a_ref, b_ref, c_ref, wo_ref, h_ref, g_ref, wr_ref, br_ref,
                           h2_ref, xn_ref, eid_ref, gate_ref):
    na, nb = a_ref.shape[1], b_ref.shape[1]
    mixed = (jnp.dot(a_ref[...], wo_ref[0:na, :], preferred_element_type=f32)
             + jnp.dot(b_ref[...], wo_ref[na:na + nb, :], preferred_element_type=f32)
             + jnp.dot(c_ref[...], wo_ref[na + nb:, :], preferred_element_type=f32))
    h2 = h_ref[...] + mixed
    h2_ref[...] = h2
    xn = _rms(h2, g_ref[...])
    xh = xn.astype(bf16)
    xl = (xn - xh.astype(f32)).astype(bf16)
    _to_slab(xn_ref, _pack_bf16_pairs(xn))
    wr = wr_ref[...]
    ah = jnp.dot(xh, wr, preferred_element_type=f32)
    al = jnp.dot(xl, wr, preferred_element_type=f32)
    logit = (ah + pltpu.roll(ah, ROUTER_ROWS, axis=1) + al).T[:ROUTER_ROWS] + br_ref[...]

    iota8 = lax.broadcasted_iota(i32, (8, TM_OUT), 0).astype(f32)
    gl = jnp.where(iota8 < N_GROUPS, logit[0:8], -jnp.inf)
    gmax = jnp.max(gl, axis=0, keepdims=True)
    gidx = jnp.min(jnp.where(gl == gmax, iota8, 8.0), axis=0, keepdims=True)
    ggate = 1.0 / jnp.sum(jnp.exp(gl - gmax), axis=0, keepdims=True)

    e = jnp.zeros((EXPERTS_PER_GROUP, TM_OUT), f32)
    for g in range(N_GROUPS):
        e = jnp.where(gidx == g, logit[8 + 8 * g:16 + 8 * g], e)
    m1 = jnp.max(e, axis=0, keepdims=True)
    i1 = jnp.min(jnp.where(e == m1, iota8, 8.0), axis=0, keepdims=True)
    e2 = jnp.where(iota8 == i1, -jnp.inf, e)
    m2 = jnp.max(e2, axis=0, keepdims=True)
    i2 = jnp.min(jnp.where(e2 == m2, iota8, 8.0), axis=0, keepdims=True)
    r = jnp.exp(m2 - m1)
    p1 = 1.0 / (1.0 + r)
    p2 = r / (1.0 + r)
    eid_ref[0:1, :] = (gidx * EXPERTS_PER_GROUP + i1).astype(i32)
    eid_ref[1:2, :] = (gidx * EXPERTS_PER_GROUP + i2).astype(i32)
    gate_ref[0:1, :] = ggate * p1
    gate_ref[1:2, :] = ggate * p2


def _outproj_router(a, b, c, wo, h, g, wr, br, layer):
    T = h.shape[0]
    row = lambda i: (i, 0)
    const = lambda i: (0, 0)
    per_layer = lambda i: (layer, 0, 0)
    na, nb, nc = a.shape[1], b.shape[1], c.shape[1]
    return pl.pallas_call(
        _outproj_router_kernel, grid=(T // TM_OUT,),
        in_specs=[pl.BlockSpec((TM_OUT, na), row), pl.BlockSpec((TM_OUT, nb), row), pl.BlockSpec((TM_OUT, nc), row),
                  pl.BlockSpec((None, na + nb + nc, D_MODEL), per_layer, pipeline_mode=pl.Buffered(1)),
                  pl.BlockSpec((TM_OUT, D_MODEL), row),
                  pl.BlockSpec((1, D_MODEL), const),
                  pl.BlockSpec((None, D_MODEL, 2 * ROUTER_ROWS), per_layer),
                  pl.BlockSpec((None, ROUTER_ROWS, 1), per_layer)],
        out_specs=[pl.BlockSpec((TM_OUT, D_MODEL), row), pl.BlockSpec((TM_OUT * XPACK_ROWS, LANE), row),
                   pl.BlockSpec((TOP_K, TM_OUT), lambda i: (0, i)), pl.BlockSpec((TOP_K, TM_OUT), lambda i: (0, i))],
        out_shape=[jax.ShapeDtypeStruct((T, D_MODEL), f32), jax.ShapeDtypeStruct((T * XPACK_ROWS, LANE), jnp.uint32),
                   jax.ShapeDtypeStruct((TOP_K, T), i32), jax.ShapeDtypeStruct((TOP_K, T), f32)],
        compiler_params=_cparams(1), name="outproj_router")(a, b, c, wo, h, g, wr, br)


def _moe_kernel(src_ref, dst_ref, cstart_ref, nvalid_ref, be_ref, wslot_ref, enext_ref, nused_ref,
                x_hbm, wg_hbm, wu_hbm, wd_hbm, y_hbm, xbuf, ybuf, wgbuf, wubuf, wdbuf, gsem, ssem, wsem, *, layer):
    b = pl.program_id(0)
    slot = b % 2
    n_used = nused_ref[0]

    def weight_copies(e, buf):
        return [pltpu.make_async_copy(hbm.at[layer, e], vbuf.at[buf], wsem.at[buf])
                for hbm, vbuf in ((wg_hbm, wgbuf), (wu_hbm, wubuf), (wd_hbm, wdbuf))]

    def gather_row(base, buf, r):
        return pltpu.make_async_copy(x_hbm.at[pl.ds(src_ref[base + r] * XPACK_ROWS, XPACK_ROWS)],
                                     xbuf.at[buf, pl.ds(r * XPACK_ROWS, XPACK_ROWS)], gsem.at[buf])

    def scatter_row(base, buf, r):
        return pltpu.make_async_copy(ybuf.at[buf, pl.ds(r * SLAB_ROWS, SLAB_ROWS)],
                                     y_hbm.at[pl.ds(dst_ref[base + r] * SLAB_ROWS, SLAB_ROWS)], ssem.at[buf])

    def start_group(make_row, base, buf, g, queues):
        for u in range(8):
            make_row(base, buf, 8 * g + u).start(priority=u % queues)

    def start_gather(blk, buf):
        base = cstart_ref[blk]

        def group(g, carry):
            start_group(gather_row, base, buf, g, 1)
            return carry
        lax.fori_loop(0, MOE_BLK // 8, group, 0)

    def start_scatter(blk, buf):
        base, nv = cstart_ref[blk], nvalid_ref[blk]

        def group(g, carry):
            start_group(scatter_row, base, buf, g, 2)
            return carry
        lax.fori_loop(0, nv >> 3, group, 0)

        def single(r, carry):
            scatter_row(base, buf, r).start(priority=0)
            return carry
        lax.fori_loop(nv & ~7, nv, single, 0)

    def wait_gather(buf):
        pltpu.make_async_copy(x_hbm.at[pl.ds(0, MOE_BLK * XPACK_ROWS)], xbuf.at[buf], gsem.at[buf]).wait()

    def wait_scatter(blk, buf):
        nv = nvalid_ref[blk]
        rows = MOE_BLK
        while rows >= 1:
            @pl.when((nv & rows) != 0)
            def _(rows=rows):
                n = rows * SLAB_ROWS
                pltpu.make_async_copy(ybuf.at[buf, pl.ds(0, n)], y_hbm.at[pl.ds(0, n)], ssem.at[buf]).wait()
            rows //= 2

    def per_slot(fn):
        for buf in range(2):
            pl.when(slot == buf)(functools.partial(fn, buf))

    @pl.when(b == 0)
    def _():
        start_gather(0, 0)
        for cp in weight_copies(be_ref[0], wslot_ref[0]):
            cp.start(priority=1)

    ws = wslot_ref[b]
    first_of_expert = jnp.logical_or(b == 0, be_ref[jnp.maximum(b - 1, 0)] != be_ref[b])

    @pl.when(jnp.logical_and(b < n_used, first_of_expert))
    def _():
        @pl.when(enext_ref[b] >= 0)
        def _():
            for cp in weight_copies(enext_ref[b], 1 - ws):
                cp.start(priority=1)
        for cp in weight_copies(be_ref[b], ws):
            cp.wait()

    @pl.when(b + 1 < n_used)
    def _():
        per_slot(lambda buf: start_gather(b + 1, 1 - buf))

    @pl.when(b < n_used)
    def _():
        wait_gather(slot)

        @pl.when(b >= 2)
        def _():
            wait_scatter(b - 2, slot)

        x = _unpack_bf16_pairs(_unslab(xbuf.at[slot], MOE_BLK, XPACK_ROWS))
        hg = jnp.dot(x, wgbuf[ws], preferred_element_type=f32)
        hu = jnp.dot(x, wubuf[ws], preferred_element_type=f32)
        hid = hg * jax.nn.sigmoid(hg) * hu
        _to_slab(ybuf.at[slot], jnp.dot(hid, wdbuf[ws], preferred_element_type=f32))
        per_slot(lambda buf: start_scatter(b, buf))

    @pl.when(b == n_used - 1)
    def _():
        wait_scatter(b, slot)

        @pl.when(b >= 1)
        def _():
            wait_scatter(b - 1, 1 - slot)


def _moe(xn, tables, w_gate, w_up, w_down, layer):
    T = xn.shape[0] // XPACK_ROWS
    n_blocks = tables[4].shape[0]
    anywhere = pl.BlockSpec(memory_space=pl.ANY)
    grid_spec = pltpu.PrefetchScalarGridSpec(
        num_scalar_prefetch=len(tables), grid=(n_blocks,),
        in_specs=[anywhere, anywhere, anywhere, anywhere],
        out_specs=anywhere,
        scratch_shapes=[pltpu.VMEM((2, MOE_BLK * XPACK_ROWS, LANE), jnp.uint32),
                        pltpu.VMEM((2, MOE_BLK * SLAB_ROWS, LANE), f32),
                        pltpu.VMEM((2, D_MODEL, EXPERT_HIDDEN), f32), pltpu.VMEM((2, D_MODEL, EXPERT_HIDDEN), f32),
                        pltpu.VMEM((2, EXPERT_HIDDEN, D_MODEL), f32),
                        pltpu.SemaphoreType.DMA((2,)), pltpu.SemaphoreType.DMA((2,)), pltpu.SemaphoreType.DMA((2,))])
    return pl.pallas_call(
        functools.partial(_moe_kernel, layer=layer), grid_spec=grid_spec,
        out_shape=jax.ShapeDtypeStruct((TOP_K * T * SLAB_ROWS, LANE), f32),
        compiler_params=_cparams(1), name="moe")(*tables, xn, w_gate, w_up, w_down)


def _dispatch(eid, T):
    n_assign = TOP_K * T
    n_blocks = n_assign // MOE_BLK + N_EXPERTS
    e_flat = eid.T.reshape(-1)
    keys = jnp.sort(e_flat * n_assign + jnp.arange(n_assign, dtype=i32))
    asg = jnp.concatenate([keys % n_assign, jnp.zeros((MOE_BLK,), i32)])
    src, dst = asg // TOP_K, (asg % TOP_K) * T + asg // TOP_K
    edges = jnp.arange(N_EXPERTS + 1, dtype=i32) * n_assign
    bounds = jnp.sum(keys[None, :] < edges[:, None], axis=1, dtype=i32)
    start, counts = bounds[:-1], bounds[1:] - bounds[:-1]
    nblk = (counts + MOE_BLK - 1) // MOE_BLK
    bend = jnp.cumsum(nblk)
    n_used = bend[-1]
    b = jnp.arange(n_blocks, dtype=i32)
    be = jnp.minimum(jnp.sum(bend[None, :] <= jnp.minimum(b, n_used - 1)[:, None], axis=1, dtype=i32), N_EXPERTS - 1)
    j = b - (bend - nblk)[be]
    used = b < n_used
    cstart = jnp.where(used, start[be] + j * MOE_BLK, 0).astype(i32)
    nvalid = jnp.where(used, jnp.clip(counts[be] - j * MOE_BLK, 0, MOE_BLK), 0).astype(i32)
    used_e = counts > 0
    rank = jnp.cumsum(used_e.astype(i32)) - 1
    ids = jnp.arange(N_EXPERTS, dtype=i32)
    later = jnp.where(used_e[None, :] & (ids[None, :] > ids[:, None]), ids[None, :], N_EXPERTS)
    nxt = jnp.min(later, axis=1)
    enext = jnp.where(nxt < N_EXPERTS, nxt, -1).astype(i32)
    return src, dst, cstart, nvalid, be, (rank[be] % 2).astype(i32), enext[be], n_used.reshape(1).astype(i32)


def _final_kernel(h_ref, y0_ref, y1_ref, gate_ref, g_ref, o_ref):
    o_ref[...] = _rms(_combine(h_ref, y0_ref, y1_ref, gate_ref), g_ref[...])


def _final(h, moe_out, g):
    T = h.shape[0]
    y, gate = moe_out
    return pl.pallas_call(
        _final_kernel, grid=(T // TM,),
        in_specs=_combine_specs(T) + [pl.BlockSpec((1, D_MODEL), lambda i: (0, 0))],
        out_specs=pl.BlockSpec((TM, D_MODEL), lambda i: (i, 0)),
        out_shape=jax.ShapeDtypeStruct((T, D_MODEL), f32),
        compiler_params=_cparams(1), name="final_norm")(h, y, y, gate, g)


def _t5_bucket(n):
    nf = jnp.maximum(n, 1).astype(f32)
    large = REL_MAX_EXACT + (jnp.log(nf / REL_MAX_EXACT) / math.log(REL_MAX_DISTANCE / REL_MAX_EXACT)
                             * (REL_BUCKETS - REL_MAX_EXACT)).astype(i32)
    return jnp.where(n < REL_MAX_EXACT, n, jnp.minimum(large, REL_BUCKETS - 1))


def _split_hi_lo(w):
    hi = w.astype(bf16)
    return hi, (w - hi.astype(f32)).astype(bf16)


def kernel(x, positions, attn_norm, w_in, pool_w, pool_scale, mla_q_norm, mla_w_uq, mla_kv_norm, mla_w_ukv,
           diff_lambda, diff_subln, rel_bias, w_out, ffn_norm, router_group_w, router_group_b,
           router_expert_w, router_expert_b, expert_w_gate, expert_w_up, expert_w_down, final_norm):
    B, S, D = x.shape
    T = B * S
    depth = w_in.shape[0]
    assert D == D_MODEL and S % TQ == 0 and T % TM == 0 and (TOP_K * T) % MOE_BLK == 0

    inv = 1.0 / (ROPE_THETA ** (jnp.arange(0, MLA_ROPE, 2, dtype=f32) / MLA_ROPE))
    ang = positions.astype(f32).reshape(T, 1) * inv
    zeros = jnp.zeros((T, LANE - MLA_ROPE), f32)
    cos = jnp.concatenate([jnp.cos(ang), jnp.cos(ang), zeros], axis=1)
    sin = jnp.concatenate([jnp.sin(ang), jnp.sin(ang), zeros], axis=1)
    cost, sint = jnp.cos(ang).T, jnp.sin(ang).T
    assert REL_CLIP == LANE - 1
    tab = jnp.pad(rel_bias[_t5_bucket(jnp.arange(LANE, dtype=i32))].T.astype(f32) * LOG2E,
                  ((0, 8 - DIFF_HEADS), (0, 0)))

    c_kr, c_dq = COL_KR + MLA_ROPE, COL_KR + MLA_ROPE + COL_DK - COL_DQ
    w_in_p = jnp.concatenate([w_in[:, :, :c_kr], jnp.zeros((depth, D, LANE - MLA_ROPE), f32),
                              w_in[:, :, c_kr:c_dq] * (DIFF_QK ** -0.5 * LOG2E), w_in[:, :, c_dq:]],
                             axis=2).astype(bf16)
    wq = mla_w_uq.reshape(depth, MLA_Q_RANK, MLA_HEADS, MLA_NOPE + MLA_ROPE)
    wq = jnp.pad(wq, ((0, 0), (0, 0), (0, 0), (0, QK_PAD - MLA_NOPE - MLA_ROPE)))
    wqt = wq.reshape(depth, MLA_Q_RANK, -1).transpose(0, 2, 1).astype(bf16)
    wkv = mla_w_ukv.reshape(depth, MLA_KV_RANK, MLA_HEADS, MLA_NOPE + MLA_V)
    wk = wkv[..., :MLA_NOPE].reshape(depth, MLA_KV_RANK, -1).astype(bf16)
    wvt = wkv[..., MLA_NOPE:].reshape(depth, MLA_KV_RANK, -1).transpose(0, 2, 1).astype(bf16)
    wo = w_out.astype(bf16)
    pool_wb = pool_w.astype(bf16)
    pad_g = jnp.zeros((depth, D, 8 - N_GROUPS), f32)
    pad_e = jnp.zeros((depth, D, ROUTER_ROWS - 8 - N_EXPERTS), f32)
    wr_hi, wr_lo = _split_hi_lo(jnp.concatenate([router_group_w, pad_g, router_expert_w, pad_e], axis=2))
    wr_cat = jnp.concatenate([wr_hi, wr_lo], axis=2)
    br = jnp.concatenate([router_group_b, pad_g[:, 0], router_expert_b, pad_e[:, 0]], axis=1)[:, :, None]

    h = x.reshape(T, D)
    moe_out = None
    for l in range(depth):
        lam_init = 0.8 - 0.6 * math.exp(-0.3 * l)
        proj, h = _inproj(h, moe_out, attn_norm[l].reshape(1, D), w_in_p, l)
        a_out = _pool(proj, pool_wb, pool_scale[l].reshape(1, POOL_WIDTH), l, B, S)
        qt, k, vt = _mla_prep(proj, cos, sin, cost, sint, mla_q_norm[l].reshape(1, -1), mla_kv_norm[l].reshape(1, -1),
                              wqt, wk, wvt, l, B, S)
        b_out = _mla_attn(qt, k, vt, B, S)
        c_out = _diff_attn(proj, diff_lambda[l], positions, tab, diff_subln[l].reshape(-1, 1), lam_init, B, S)
        h, xn, eid, gate = _outproj_router(a_out, b_out, c_out, wo, h, ffn_norm[l].reshape(1, D), wr_cat, br, l)
        y = _moe(xn, _dispatch(eid, T), expert_w_gate, expert_w_up, expert_w_down, l)
        moe_out = (y, gate.T)
    out = _final(h, moe_out, final_norm.reshape(1, D))
    return out.reshape(B, S, D)
```

```python
import functools
import math

import jax
import jax.numpy as jnp
from jax import lax
from jax.experimental import pallas as pl
from jax.experimental.pallas import tpu as pltpu

f32, bf16, i32 = jnp.float32, jnp.bfloat16, jnp.int32

D_MODEL = 2048
POOL_WINDOWS = (2, 4, 8, 16)
POOL_GROUPS = 4
POOL_WIDTH = 512
MLA_NOPE, MLA_ROPE, MLA_V, MLA_HEADS = 128, 64, 128, 6
MLA_Q_RANK, MLA_KV_RANK = 512, 256
ROPE_THETA = 10000.0
DIFF_QK, DIFF_V, DIFF_HEADS = 64, 128, 6
REL_BUCKETS, REL_MAX_EXACT, REL_MAX_DISTANCE = 32, 16, 128
N_GROUPS, EXPERTS_PER_GROUP, N_EXPERTS, TOP_K, EXPERT_HIDDEN = 4, 8, 32, 2, 512
NORM_EPS = 1e-6

LANE = 128
MXU_DIM = 256
VMEM_LIMIT = 56 * 1024 * 1024

N_IN_PAD = 3712
COL_CQ, COL_CKV, COL_KR, COL_DQ, COL_DK, COL_DV = 512, 1024, 1280, 1408, 2176, 2944
QK_PAD = 2 * LANE

TM = 256
TM_OUT = 512
TQ = 512
TK = 256
MOE_BLK = 256
SLAB_ROWS = D_MODEL // LANE
REL_CLIP = REL_MAX_DISTANCE - 1
REL_FAR = REL_MAX_DISTANCE
LOG2E = math.log2(math.e)


def _cparams(n_axes):
    return pltpu.CompilerParams(dimension_semantics=("arbitrary",) * n_axes, vmem_limit_bytes=VMEM_LIMIT)


def _rms(x, g):
    return x * lax.rsqrt(jnp.mean(x * x, axis=-1, keepdims=True) + NORM_EPS) * g


def _unslab(ref, rows, per_token=SLAB_ROWS):
    return jnp.concatenate([ref[pl.ds(s, rows, stride=per_token), :] for s in range(per_token)], axis=1)


def _to_slab(ref, val):
    per_token = val.shape[1] // LANE
    for s in range(per_token):
        ref[pl.ds(s, val.shape[0], stride=per_token), :] = val[:, s * LANE:(s + 1) * LANE]


XPACK_ROWS = SLAB_ROWS // 2


def _pack_bf16_pairs(x):
    bits = lax.bitcast_convert_type(x.astype(bf16).astype(f32), jnp.uint32)
    half = x.shape[1] // 2
    return bits[:, half:] | (bits[:, :half] >> 16)


def _unpack_bf16_pairs(w):
    lo = lax.bitcast_convert_type(w << 16, f32)
    hi = lax.bitcast_convert_type(w & jnp.uint32(0xFFFF0000), f32)
    return jnp.concatenate([lo, hi], axis=1)


def _combine(h_ref, y0_ref, y1_ref, gate_ref):
    gate = gate_ref[...]
    return h_ref[...] + _unslab(y0_ref, TM) * gate[:, 0:1] + _unslab(y1_ref, TM) * gate[:, 1:2]


def _combine_specs(T):
    plane1 = T // TM
    return [pl.BlockSpec((TM, D_MODEL), lambda i: (i, 0)),
            pl.BlockSpec((TM * SLAB_ROWS, LANE), lambda i: (i, 0)),
            pl.BlockSpec((TM * SLAB_ROWS, LANE), lambda i: (plane1 + i, 0)),
            pl.BlockSpec((TM, TOP_K), lambda i: (i, 0))]


def _inproj_kernel(*refs, combine):
    if combine:
        h_ref, y0_ref, y1_ref, gate_ref, g_ref, w_ref, proj_ref, hout_ref = refs
        h = _combine(h_ref, y0_ref, y1_ref, gate_ref)
        hout_ref[...] = h
    else:
        h_ref, g_ref, w_ref, proj_ref = refs
        h = h_ref[...]
    u = _rms(h, g_ref[...]).astype(bf16)
    for n0 in range(0, N_IN_PAD, 512):
        n1 = min(n0 + 512, N_IN_PAD)
        proj_ref[:, n0:n1] = jnp.dot(u, w_ref[:, n0:n1], preferred_element_type=f32).astype(bf16)


def _inproj(h, moe_out, g, w, layer):
    T = h.shape[0]
    combine = moe_out is not None
    row = lambda i: (i, 0)
    const = lambda i: (0, 0)
    if combine:
        y, gate = moe_out
        in_specs = _combine_specs(T)
        args = [h, y, y, gate]
    else:
        in_specs = [pl.BlockSpec((TM, D_MODEL), row)]
        args = [h]
    in_specs += [pl.BlockSpec((1, D_MODEL), const),
                 pl.BlockSpec((None, D_MODEL, N_IN_PAD), lambda i: (layer, 0, 0), pipeline_mode=pl.Buffered(1))]
    args += [g, w]
    out_shape = [jax.ShapeDtypeStruct((T, N_IN_PAD), bf16)]
    out_specs = [pl.BlockSpec((TM, N_IN_PAD), row)]
    if combine:
        out_shape.append(jax.ShapeDtypeStruct((T, D_MODEL), f32))
        out_specs.append(pl.BlockSpec((TM, D_MODEL), row))
    res = pl.pallas_call(
        functools.partial(_inproj_kernel, combine=combine),
        grid=(T // TM,), in_specs=in_specs, out_specs=out_specs, out_shape=out_shape,
        compiler_params=_cparams(1), name="inproj_c" if combine else "inproj")(*args)
    return (res[0], res[1]) if combine else (res[0], h)


def _pool_kernel(x_ref, w_ref, sc_ref, o_ref):
    g = pl.program_id(1)
    x = x_ref[...].astype(f32)
    t = lax.broadcasted_iota(i32, x.shape, 0)

    def lagged(v, k):
        return jnp.where(t >= k, pltpu.roll(v, k, axis=0), 0.0)

    s2 = x + lagged(x, 1)
    s4 = s2 + lagged(s2, 2)
    s8 = s4 + lagged(s4, 4)
    s16 = s8 + lagged(s8, 8)
    wsum = jnp.where(g == 0, s2, jnp.where(g == 1, s4, jnp.where(g == 2, s8, s16)))
    win = jnp.where(g == 0, 2, jnp.where(g == 1, 4, jnp.where(g == 2, 8, 16)))
    cnt = jnp.minimum(t + 1, win).astype(f32)
    d = (wsum / cnt - x).astype(bf16)
    y = jnp.dot(d, w_ref[...], preferred_element_type=f32) * sc_ref[...]
    o_ref[...] = y.astype(bf16)


def _pool(proj, w, scale, layer, B, S):
    T = B * S
    return pl.pallas_call(
        _pool_kernel, grid=(B, POOL_GROUPS),
        in_specs=[pl.BlockSpec((S, LANE), lambda b, g: (b, g)),
                  pl.BlockSpec((None, None, LANE, LANE), lambda b, g: (layer, g, 0, 0)),
                  pl.BlockSpec((1, LANE), lambda b, g: (0, g))],
        out_specs=pl.BlockSpec((S, LANE), lambda b, g: (b, g)),
        out_shape=jax.ShapeDtypeStruct((T, POOL_WIDTH), bf16),
        compiler_params=_cparams(2), name="pool")(proj, w, scale)


VT_ROWS = MLA_V + 16


def _nt_dot(a, b):
    return lax.dot_general(a, b, (((1,), (1,)), ((), ())), preferred_element_type=f32)


def _rope128(r, cos, sin):
    lane = lax.broadcasted_iota(i32, r.shape, 1)
    half = MLA_ROPE // 2
    partner = jnp.where(lane < half, -pltpu.roll(r, LANE - half, axis=1), pltpu.roll(r, half, axis=1))
    return r * cos + partner * sin


def _ones_row_pad(n_cols):
    row = lax.broadcasted_iota(i32, (VT_ROWS - MLA_V, n_cols), 0)
    return jnp.where(row == 0, 1.0, 0.0).astype(bf16)


def _mla_prep_kernel(cq_ref, ckv_ref, kr_ref, cos_ref, sin_ref, cost_ref, sint_ref, qn_ref, kvn_ref,
                     wqt_ref, wk_ref, wvt_ref, qt_ref, k_ref, vt_ref):
    scale = (MLA_NOPE + MLA_ROPE) ** -0.5 * LOG2E
    half = MLA_ROPE // 2
    cost, sint = cost_ref[...], sint_ref[...]
    cq = _rms(cq_ref[...].astype(f32), qn_ref[...]).astype(bf16)
    ckv = _rms(ckv_ref[...].astype(f32), kvn_ref[...]).astype(bf16)
    kr = _rope128(kr_ref[...].astype(f32), cos_ref[...], sin_ref[...]).astype(bf16)
    pad = _ones_row_pad(TM)
    for h in range(MLA_HEADS):
        r0 = h * QK_PAD
        qt = _nt_dot(wqt_ref[r0:r0 + QK_PAD, :], cq) * scale
        x1, x2 = qt[MLA_NOPE:MLA_NOPE + half], qt[MLA_NOPE + half:MLA_NOPE + MLA_ROPE]
        qt_ref[r0:r0 + MLA_NOPE, :] = qt[:MLA_NOPE].astype(bf16)
        qt_ref[r0 + MLA_NOPE:r0 + MLA_NOPE + half, :] = (x1 * cost - x2 * sint).astype(bf16)
        qt_ref[r0 + MLA_NOPE + half:r0 + MLA_NOPE + MLA_ROPE, :] = (x1 * sint + x2 * cost).astype(bf16)
        qt_ref[r0 + MLA_NOPE + MLA_ROPE:r0 + QK_PAD, :] = qt[MLA_NOPE + MLA_ROPE:].astype(bf16)
        kn = jnp.dot(ckv, wk_ref[:, h * LANE:(h + 1) * LANE], preferred_element_type=f32)
        k_ref[:, r0:r0 + LANE] = kn.astype(bf16)
        k_ref[:, r0 + LANE:r0 + QK_PAD] = kr
        vt = _nt_dot(wvt_ref[h * MLA_V:(h + 1) * MLA_V, :], ckv)
        vt_ref[h, 0:MLA_V, :] = vt.astype(bf16)
        vt_ref[h, MLA_V:VT_ROWS, :] = pad


def _mla_prep(proj, cos, sin, cost, sint, qn, kvn, wqt, wk, wvt, layer, B, S):
    T = B * S
    assert TM == TK
    nk = S // TK
    const = lambda i: (0, 0)
    per_layer = lambda i: (layer, 0, 0)
    nq, nkc = MLA_HEADS * QK_PAD, MLA_HEADS * LANE
    return pl.pallas_call(
        _mla_prep_kernel, grid=(T // TM,),
        in_specs=[pl.BlockSpec((TM, MLA_Q_RANK), lambda i: (i, COL_CQ // MLA_Q_RANK)),
                  pl.BlockSpec((TM, MLA_KV_RANK), lambda i: (i, COL_CKV // MLA_KV_RANK)),
                  pl.BlockSpec((TM, LANE), lambda i: (i, COL_KR // LANE)),
                  pl.BlockSpec((TM, LANE), lambda i: (i, 0)),
                  pl.BlockSpec((TM, LANE), lambda i: (i, 0)),
                  pl.BlockSpec((MLA_ROPE // 2, TM), lambda i: (0, i)),
                  pl.BlockSpec((MLA_ROPE // 2, TM), lambda i: (0, i)),
                  pl.BlockSpec((1, MLA_Q_RANK), const),
                  pl.BlockSpec((1, MLA_KV_RANK), const),
                  pl.BlockSpec((None, nq, MLA_Q_RANK), per_layer),
                  pl.BlockSpec((None, MLA_KV_RANK, nkc), per_layer),
                  pl.BlockSpec((None, MLA_HEADS * MLA_V, MLA_KV_RANK), per_layer)],
        out_specs=[pl.BlockSpec((nq, TM), lambda i: (0, i)),
                   pl.BlockSpec((TM, nq), lambda i: (i, 0)),
                   pl.BlockSpec((None, MLA_HEADS, None, VT_ROWS, TK), lambda i: (i // nk, 0, i % nk, 0, 0))],
        out_shape=[jax.ShapeDtypeStruct((nq, T), bf16), jax.ShapeDtypeStruct((T, nq), bf16),
                   jax.ShapeDtypeStruct((B, MLA_HEADS, nk, VT_ROWS, TK), bf16)],
        compiler_params=_cparams(1), name="mla_prep")(proj, proj, proj, cos, sin, cost, sint, qn, kvn, wqt, wk, wvt)


def _softmax_probs(s, m, shift, lo=0):
    m_old = m[:, lo:]
    m_new = jnp.maximum(m_old, jnp.max(s, axis=0, keepdims=True) + shift)
    alpha = jnp.exp2(m_old - m_new)
    p = jnp.exp2(s - (m_new - shift)).astype(bf16)
    return (m_new if lo == 0 else jnp.concatenate([m[:, :lo], m_new], axis=1)), alpha, p


def _accumulate(acc_ref, idx, alpha, vt, p, lo=0):
    acc_ref[idx, :, lo:] = alpha * acc_ref[idx, :, lo:] + jnp.dot(vt, p, preferred_element_type=f32)


def _attn_out_t(acc):
    return acc[0:MLA_V] * (1.0 / acc[MLA_V:MLA_V + 1])


def _diag_visible(lo):
    krow = lax.broadcasted_iota(i32, (TK, TQ - lo), 0)
    qcol = lax.broadcasted_iota(i32, (TK, TQ - lo), 1)
    return krow <= qcol


MLA_HEADS_PER_STEP = 6


def _mla_attn_kernel(qt_ref, k_ref, vt_ref, o_ref, acc_ref, p_ref):
    i = pl.program_id(2)
    acc_ref[...] = jnp.zeros_like(acc_ref)
    p_ref[...] = jnp.zeros_like(p_ref)
    n_full = i * (TQ // TK)
    heads = range(MLA_HEADS_PER_STEP)

    def scores(c, j, lo=0):
        k = k_ref[pl.ds(pl.multiple_of(j * TK, TK), TK), c * QK_PAD:(c + 1) * QK_PAD]
        return jnp.dot(k, qt_ref[c * QK_PAD:(c + 1) * QK_PAD, lo:], preferred_element_type=f32)

    def flush(alphas, j):
        for c in heads:
            _accumulate(acc_ref, c, alphas[c], vt_ref[c, j], p_ref[c])

    def full(j, carry):
        ms, alphas = carry
        flush(alphas, jnp.maximum(j - 1, 0))
        new = [_softmax_probs(scores(c, j), ms[c], 0.0) for c in heads]
        for c in heads:
            p_ref[c] = new[c][2]
        return tuple(n[0] for n in new), tuple(n[1] for n in new)

    minf = jnp.full((1, TQ), -jnp.inf, f32)
    one = jnp.ones((1, TQ), f32)
    def full_pair(jj, carry):
        for t in range(TQ // TK):
            carry = full(jj * (TQ // TK) + t, carry)
        return carry

    ms, alphas = lax.fori_loop(0, i, full_pair, ((minf,) * MLA_HEADS_PER_STEP, (one,) * MLA_HEADS_PER_STEP))
    flush(alphas, jnp.maximum(n_full - 1, 0))
    for d in range(TQ // TK):
        j, lo = n_full + d, d * TK
        for c in heads:
            m, alpha, p = _softmax_probs(jnp.where(_diag_visible(lo), scores(c, j, lo), -jnp.inf), ms[c], 0.0, lo)
            _accumulate(acc_ref, c, alpha, vt_ref[c, j], p, lo)
            ms = ms[:c] + (m,) + ms[c + 1:]
    for c in heads:
        o_ref[:, c * MLA_V:(c + 1) * MLA_V] = _attn_out_t(acc_ref[c]).T.astype(bf16)


def _mla_attn(qt, k, vt, B, S):
    T = B * S
    nq, nk = S // TQ, S // TK
    hs = MLA_HEADS_PER_STEP
    return pl.pallas_call(
        _mla_attn_kernel, grid=(B, MLA_HEADS // hs, nq),
        in_specs=[pl.BlockSpec((hs * QK_PAD, TQ), lambda b, h, i: (h, b * nq + i)),
                  pl.BlockSpec((S, hs * QK_PAD), lambda b, h, i: (b, h)),
                  pl.BlockSpec((None, hs, nk, VT_ROWS, TK), lambda b, h, i: (b, h, 0, 0, 0))],
        out_specs=pl.BlockSpec((TQ, hs * MLA_V), lambda b, h, i: (b * nq + i, h)),
        out_shape=jax.ShapeDtypeStruct((T, MLA_HEADS * MLA_V), bf16),
        scratch_shapes=[pltpu.VMEM((hs, VT_ROWS, TQ), f32), pltpu.VMEM((hs, TK, TQ), bf16)],
        compiler_params=_cparams(3), name="mla_attn")(qt, k, vt)


DIFF_HEADS_PER_STEP = 6


def _diff_attn_kernel(pmin_ref, kmax_ref, lam_ref, *refs, lam_init, nq, nk):
    hs = DIFF_HEADS_PER_STEP
    q_refs, k_refs, v_refs = refs[:hs], refs[hs:2 * hs], refs[2 * hs:3 * hs]
    posq_ref, posk_ref, tab_ref, sub_ref, o_ref, vt_ref, acc_ref, p_ref = refs[3 * hs:]
    b, hp, i = pl.program_id(0), pl.program_id(1), pl.program_id(2)
    heads = range(hs)
    chains = [(c, m) for c in heads for m in range(2)]

    @pl.when(i == 0)
    def _():
        pad = _ones_row_pad(TK)
        for c in heads:
            for j in range(nk):
                vt_ref[c, j, 0:DIFF_V, :] = v_refs[c][j * TK:(j + 1) * TK, :].astype(f32).T.astype(bf16)
                vt_ref[c, j, DIFF_V:VT_ROWS, :] = pad

    lv = lam_ref[...]
    lam = (jnp.exp(jnp.sum(lv[0:1] * lv[1:2], axis=1, keepdims=True))
           - jnp.exp(jnp.sum(lv[2:3] * lv[3:4], axis=1, keepdims=True)) + lam_init)
    frow = lax.broadcasted_iota(i32, (2 * DIFF_QK, TQ), 0)
    qm = []
    for c in heads:
        qt = q_refs[c][...].astype(f32).T
        qm.append((jnp.where(frow < DIFF_QK, qt, 0.0).astype(bf16), jnp.where(frow >= DIFF_QK, qt, 0.0).astype(bf16)))
    acc_ref[...] = jnp.zeros_like(acc_ref)
    posq = posq_ref[...]
    tabs = [jnp.broadcast_to(tab_ref[pl.ds(hp * hs + c, 1), :], (TK, LANE)) for c in heads]
    far_bias = [tab_ref[pl.ds(hp * hs + c, 1), REL_CLIP:REL_CLIP + 1] for c in heads]
    n_full = i * (TQ // TK)

    def scores(c, j, lo):
        off = pl.multiple_of(j * TK, TK)
        q_both = jnp.concatenate([qm[c][0][:, lo:], qm[c][1][:, lo:]], axis=1)
        return jnp.dot(k_refs[c][pl.ds(off, TK), :], q_both, preferred_element_type=f32)

    def distances(j, lo):
        off = pl.multiple_of(j * TK, TK)
        return jnp.clip(posq[:, lo:] - posk_ref[pl.ds(off, TK), :], 0, REL_CLIP)

    def bias_tile(c, n):
        return jnp.concatenate(
            [jnp.take_along_axis(tabs[c], n[:, t * LANE:(t + 1) * LANE], axis=1) for t in range(n.shape[1] // LANE)],
            axis=1)

    def flush(alphas, j):
        for c, m in chains:
            _accumulate(acc_ref, 2 * c + m, alphas[2 * c + m], vt_ref[c, j], p_ref[2 * c + m])

    def softmax_tile(j, ms, far_tile, lo=0, visible=None):
        n = None if far_tile else distances(j, lo)
        out = []
        for c in heads:
            s2 = scores(c, j, lo)
            bias = None if far_tile else bias_tile(c, n)
            for m in range(2):
                s = s2[:, m * (TQ - lo):(m + 1) * (TQ - lo)]
                if not far_tile:
                    s = s + bias
                if visible is not None:
                    s = jnp.where(visible, s, -jnp.inf)
                out.append(_softmax_probs(s, ms[2 * c + m], far_bias[c] if far_tile else 0.0, lo))
        return out

    def below_diagonal(far_tile, j, carry):
        ms, alphas = carry
        flush(alphas, jnp.maximum(j - 1, 0))
        new = softmax_tile(j, ms, far_tile)
        for idx in range(len(chains)):
            p_ref[idx] = new[idx][2]
        return tuple(n[0] for n in new), tuple(n[1] for n in new)

    def body(j, carry):
        is_far = pmin_ref[b * nq + i] - kmax_ref[b * nk + j] >= REL_FAR
        return lax.cond(is_far, functools.partial(below_diagonal, True), functools.partial(below_diagonal, False),
                        j, carry)

    p_ref[...] = jnp.zeros_like(p_ref)
    minf = jnp.full((1, TQ), -jnp.inf, f32)
    one = jnp.ones((1, TQ), f32)
    ms, alphas = lax.fori_loop(0, n_full, body, ((minf,) * len(chains), (one,) * len(chains)))
    flush(alphas, jnp.maximum(n_full - 1, 0))
    for d in range(TQ // TK):
        j, lo = n_full + d, d * TK
        new = softmax_tile(j, ms, False, lo, _diag_visible(lo))
        for c, m in chains:
            _accumulate(acc_ref, 2 * c + m, new[2 * c + m][1], vt_ref[c, j], new[2 * c + m][2], lo)
        ms = tuple(n[0] for n in new)
    gain = sub_ref[...] * (1.0 - lam_init)
    for c in heads:
        ot = _attn_out_t(acc_ref[2 * c]) - lam * _attn_out_t(acc_ref[2 * c + 1])
        ot = ot * lax.rsqrt(jnp.mean(ot * ot, axis=0, keepdims=True) + NORM_EPS) * gain
        o_ref[:, c * DIFF_V:(c + 1) * DIFF_V] = ot.T.astype(bf16)


def _diff_attn(proj, lam_vecs, positions, tab, subln, lam_init, B, S):
    T = B * S
    nq, nk = S // TQ, S // TK
    hs = DIFF_HEADS_PER_STEP
    pmin = positions.reshape(B * nq, TQ).min(axis=1)
    kmax = positions.reshape(B * nk, TK).max(axis=1)

    def head_specs(rows, col0, by_tile):
        def spec(c):
            if by_tile:
                return pl.BlockSpec((rows, LANE), lambda b, hp, i, *_: (b * nq + i, col0 // LANE + hp * hs + c))
            return pl.BlockSpec((rows, LANE), lambda b, hp, i, *_: (b, col0 // LANE + hp * hs + c))
        return [spec(c) for c in range(hs)]

    grid_spec = pltpu.PrefetchScalarGridSpec(
        num_scalar_prefetch=2, grid=(B, DIFF_HEADS // hs, nq),
        in_specs=([pl.BlockSpec((4, DIFF_QK), lambda b, hp, i, *_: (0, 0))]
                  + head_specs(TQ, COL_DQ, True) + head_specs(S, COL_DK, False) + head_specs(S, COL_DV, False)
                  + [pl.BlockSpec((1, TQ), lambda b, hp, i, *_: (0, b * nq + i)),
                     pl.BlockSpec((S, 1), lambda b, hp, i, *_: (b, 0)),
                     pl.BlockSpec((8, LANE), lambda b, hp, i, *_: (0, 0)),
                     pl.BlockSpec((DIFF_V, 1), lambda b, hp, i, *_: (0, 0))]),
        out_specs=pl.BlockSpec((TQ, hs * DIFF_V), lambda b, hp, i, *_: (b * nq + i, hp)),
        scratch_shapes=[pltpu.VMEM((hs, nk, VT_ROWS, TK), bf16), pltpu.VMEM((2 * hs, VT_ROWS, TQ), f32),
                        pltpu.VMEM((2 * hs, TK, TQ), bf16)])
    return pl.pallas_call(
        functools.partial(_diff_attn_kernel, lam_init=lam_init, nq=nq, nk=nk), grid_spec=grid_spec,
        out_shape=jax.ShapeDtypeStruct((T, DIFF_HEADS * DIFF_V), bf16),
        compiler_params=_cparams(3), name="diff_attn")(
            pmin, kmax, lam_vecs, *([proj] * (3 * hs)), positions.reshape(1, T), positions.reshape(T, 1), tab, subln)


ROUTER_ROWS = 64


def _outproj_router_kernel(a_ref, b_ref, c_ref, wo_ref, h_ref, g_ref, wr_ref, br_ref,
                           h2_ref, xn_ref, eid_ref, gate_ref):
    na, nb = a_ref.shape[1], b_ref.shape[1]
    mixed = (jnp.dot(a_ref[...], wo_ref[0:na, :], preferred_element_type=f32)
             + jnp.dot(b_ref[...], wo_ref[na:na + nb, :], preferred_element_type=f32)
             + jnp.dot(c_ref[...], wo_ref[na + nb:, :], preferred_element_type=f32))
    h2 = h_ref[...] + mixed
    h2_ref[...] = h2
    xn = _rms(h2, g_ref[...])
    xh = xn.astype(bf16)
    xl = (xn - xh.astype(f32)).astype(bf16)
    _to_slab(xn_ref, _pack_bf16_pairs(xn))
    wr = wr_ref[...]
    ah = jnp.dot(xh, wr, preferred_element_type=f32)
    al = jnp.dot(xl, wr, preferred_element_type=f32)
    logit = (ah + pltpu.roll(ah, ROUTER_ROWS, axis=1) + al).T[:ROUTER_ROWS] + br_ref[...]

    iota8 = lax.broadcasted_iota(i32, (8, TM_OUT), 0).astype(f32)
    gl = jnp.where(iota8 < N_GROUPS, logit[0:8], -jnp.inf)
    gmax = jnp.max(gl, axis=0, keepdims=True)
    gidx = jnp.min(jnp.where(gl == gmax, iota8, 8.0), axis=0, keepdims=True)
    ggate = 1.0 / jnp.sum(jnp.exp(gl - gmax), axis=0, keepdims=True)

    e = jnp.zeros((EXPERTS_PER_GROUP, TM_OUT), f32)
    for g in range(N_GROUPS):
        e = jnp.where(gidx == g, logit[8 + 8 * g:16 + 8 * g], e)
    m1 = jnp.max(e, axis=0, keepdims=True)
    i1 = jnp.min(jnp.where(e == m1, iota8, 8.0), axis=0, keepdims=True)
    e2 = jnp.where(iota8 == i1, -jnp.inf, e)
    m2 = jnp.max(e2, axis=0, keepdims=True)
    i2 = jnp.min(jnp.where(e2 == m2, iota8, 8.0), axis=0, keepdims=True)
    r = jnp.exp(m2 - m1)
    p1 = 1.0 / (1.0 + r)
    p2 = r / (1.0 + r)
    eid_ref[0:1, :] = (gidx * EXPERTS_PER_GROUP + i1).astype(i32)
    eid_ref[1:2, :] = (gidx * EXPERTS_PER_GROUP + i2).astype(i32)
    gate_ref[0:1, :] = ggate * p1
    gate_ref[1:2, :] = ggate * p2


def _outproj_router(a, b, c, wo, h, g, wr, br, layer):
    T = h.shape[0]
    row = lambda i: (i, 0)
    const = lambda i: (0, 0)
    per_layer = lambda i: (layer, 0, 0)
    na, nb, nc = a.shape[1], b.shape[1], c.shape[1]
    return pl.pallas_call(
        _outproj_router_kernel, grid=(T // TM_OUT,),
        in_specs=[pl.BlockSpec((TM_OUT, na), row), pl.BlockSpec((TM_OUT, nb), row), pl.BlockSpec((TM_OUT, nc), row),
                  pl.BlockSpec((None, na + nb + nc, D_MODEL), per_layer, pipeline_mode=pl.Buffered(1)),
                  pl.BlockSpec((TM_OUT, D_MODEL), row),
                  pl.BlockSpec((1, D_MODEL), const),
                  pl.BlockSpec((None, D_MODEL, 2 * ROUTER_ROWS), per_layer),
                  pl.BlockSpec((None, ROUTER_ROWS, 1), per_layer)],
        out_specs=[pl.BlockSpec((TM_OUT, D_MODEL), row), pl.BlockSpec((TM_OUT * XPACK_ROWS, LANE), row),
                   pl.BlockSpec((TOP_K, TM_OUT), lambda i: (0, i)), pl.BlockSpec((TOP_K, TM_OUT), lambda i: (0, i))],
        out_shape=[jax.ShapeDtypeStruct((T, D_MODEL), f32), jax.ShapeDtypeStruct((T * XPACK_ROWS, LANE), jnp.uint32),
                   jax.ShapeDtypeStruct((TOP_K, T), i32), jax.ShapeDtypeStruct((TOP_K, T), f32)],
        compiler_params=_cparams(1), name="outproj_router")(a, b, c, wo, h, g, wr, br)


def _moe_kernel(src_ref, dst_ref, cstart_ref, nvalid_ref, be_ref, wslot_ref, enext_ref, nused_ref,
                x_hbm, wg_hbm, wu_hbm, wd_hbm, y_hbm, xbuf, ybuf, wgbuf, wubuf, wdbuf, gsem, ssem, wsem, *, layer):
    b = pl.program_id(0)
    slot = b % 2
    n_used = nused_ref[0]

    def weight_copies(e, buf):
        return [pltpu.make_async_copy(hbm.at[layer, e], vbuf.at[buf], wsem.at[buf])
                for hbm, vbuf in ((wg_hbm, wgbuf), (wu_hbm, wubuf), (wd_hbm, wdbuf))]

    def gather_row(base, buf, r):
        return pltpu.make_async_copy(x_hbm.at[pl.ds(src_ref[base + r] * XPACK_ROWS, XPACK_ROWS)],
                                     xbuf.at[buf, pl.ds(r * XPACK_ROWS, XPACK_ROWS)], gsem.at[buf])

    def scatter_row(base, buf, r):
        return pltpu.make_async_copy(ybuf.at[buf, pl.ds(r * SLAB_ROWS, SLAB_ROWS)],
                                     y_hbm.at[pl.ds(dst_ref[base + r] * SLAB_ROWS, SLAB_ROWS)], ssem.at[buf])

    def start_group(make_row, base, buf, g, queues):
        for u in range(8):
            make_row(base, buf, 8 * g + u).start(priority=u % queues)

    def start_gather(blk, buf):
        base = cstart_ref[blk]

        def group(g, carry):
            start_group(gather_row, base, buf, g, 1)
            return carry
        lax.fori_loop(0, MOE_BLK // 8, group, 0)

    def start_scatter(blk, buf):
        base, nv = cstart_ref[blk], nvalid_ref[blk]

        def group(g, carry):
            start_group(scatter_row, base, buf, g, 2)
            return carry
        lax.fori_loop(0, nv >> 3, group, 0)

        def single(r, carry):
            scatter_row(base, buf, r).start(priority=0)
            return carry
        lax.fori_loop(nv & ~7, nv, single, 0)

    def wait_gather(buf):
        pltpu.make_async_copy(x_hbm.at[pl.ds(0, MOE_BLK * XPACK_ROWS)], xbuf.at[buf], gsem.at[buf]).wait()

    def wait_scatter(blk, buf):
        nv = nvalid_ref[blk]
        rows = MOE_BLK
        while rows >= 1:
            @pl.when((nv & rows) != 0)
            def _(rows=rows):
                n = rows * SLAB_ROWS
                pltpu.make_async_copy(ybuf.at[buf, pl.ds(0, n)], y_hbm.at[pl.ds(0, n)], ssem.at[buf]).wait()
            rows //= 2

    def per_slot(fn):
        for buf in range(2):
            pl.when(slot == buf)(functools.partial(fn, buf))

    @pl.when(b == 0)
    def _():
        start_gather(0, 0)
        for cp in weight_copies(be_ref[0], wslot_ref[0]):
            cp.start(priority=1)

    ws = wslot_ref[b]
    first_of_expert = jnp.logical_or(b == 0, be_ref[jnp.maximum(b - 1, 0)] != be_ref[b])

    @pl.when(jnp.logical_and(b < n_used, first_of_expert))
    def _():
        @pl.when(enext_ref[b] >= 0)
        def _():
            for cp in weight_copies(enext_ref[b], 1 - ws):
                cp.start(priority=1)
        for cp in weight_copies(be_ref[b], ws):
            cp.wait()

    @pl.when(b + 1 < n_used)
    def _():
        per_slot(lambda buf: start_gather(b + 1, 1 - buf))

    @pl.when(b < n_used)
    def _():
        wait_gather(slot)

        @pl.when(b >= 2)
        def _():
            wait_scatter(b - 2, slot)

        x = _unpack_bf16_pairs(_unslab(xbuf.at[slot], MOE_BLK, XPACK_ROWS))
        hg = jnp.dot(x, wgbuf[ws], preferred_element_type=f32)
        hu = jnp.dot(x, wubuf[ws], preferred_element_type=f32)
        hid = hg * jax.nn.sigmoid(hg) * hu
        _to_slab(ybuf.at[slot], jnp.dot(hid, wdbuf[ws], preferred_element_type=f32))
        per_slot(lambda buf: start_scatter(b, buf))

    @pl.when(b == n_used - 1)
    def _():
        wait_scatter(b, slot)

        @pl.when(b >= 1)
        def _():
            wait_scatter(b - 1, 1 - slot)


def _moe(xn, tables, w_gate, w_up, w_down, layer):
    T = xn.shape[0] // XPACK_ROWS
    n_blocks = tables[4].shape[0]
    anywhere = pl.BlockSpec(memory_space=pl.ANY)
    grid_spec = pltpu.PrefetchScalarGridSpec(
        num_scalar_prefetch=len(tables), grid=(n_blocks,),
        in_specs=[anywhere, anywhere, anywhere, anywhere],
        out_specs=anywhere,
        scratch_shapes=[pltpu.VMEM((2, MOE_BLK * XPACK_ROWS, LANE), jnp.uint32),
                        pltpu.VMEM((2, MOE_BLK * SLAB_ROWS, LANE), f32),
                        pltpu.VMEM((2, D_MODEL, EXPERT_HIDDEN), f32), pltpu.VMEM((2, D_MODEL, EXPERT_HIDDEN), f32),
                        pltpu.VMEM((2, EXPERT_HIDDEN, D_MODEL), f32),
                        pltpu.SemaphoreType.DMA((2,)), pltpu.SemaphoreType.DMA((2,)), pltpu.SemaphoreType.DMA((2,))])
    return pl.pallas_call(
        functools.partial(_moe_kernel, layer=layer), grid_spec=grid_spec,
        out_shape=jax.ShapeDtypeStruct((TOP_K * T * SLAB_ROWS, LANE), f32),
        compiler_params=_cparams(1), name="moe")(*tables, xn, w_gate, w_up, w_down)


def _dispatch(eid, T):
    n_assign = TOP_K * T
    n_blocks = n_assign // MOE_BLK + N_EXPERTS
    e_flat = eid.T.reshape(-1)
    keys = jnp.sort(e_flat * n_assign + jnp.arange(n_assign, dtype=i32))
    asg = jnp.concatenate([keys % n_assign, jnp.zeros((MOE_BLK,), i32)])
    src, dst = asg // TOP_K, (asg % TOP_K) * T + asg // TOP_K
    edges = jnp.arange(N_EXPERTS + 1, dtype=i32) * n_assign
    bounds = jnp.sum(keys[None, :] < edges[:, None], axis=1, dtype=i32)
    start, counts = bounds[:-1], bounds[1:] - bounds[:-1]
    nblk = (counts + MOE_BLK - 1) // MOE_BLK
    bend = jnp.cumsum(nblk)
    n_used = bend[-1]
    b = jnp.arange(n_blocks, dtype=i32)
    be = jnp.minimum(jnp.sum(bend[None, :] <= jnp.minimum(b, n_used - 1)[:, None], axis=1, dtype=i32), N_EXPERTS - 1)
    j = b - (bend - nblk)[be]
    used = b < n_used
    cstart = jnp.where(used, start[be] + j * MOE_BLK, 0).astype(i32)
    nvalid = jnp.where(used, jnp.clip(counts[be] - j * MOE_BLK, 0, MOE_BLK), 0).astype(i32)
    used_e = counts > 0
    rank = jnp.cumsum(used_e.astype(i32)) - 1
    ids = jnp.arange(N_EXPERTS, dtype=i32)
    later = jnp.where(used_e[None, :] & (ids[None, :] > ids[:, None]), ids[None, :], N_EXPERTS)
    nxt = jnp.min(later, axis=1)
    enext = jnp.where(nxt < N_EXPERTS, nxt, -1).astype(i32)
    return src, dst, cstart, nvalid, be, (rank[be] % 2).astype(i32), enext[be], n_used.reshape(1).astype(i32)


def _final_kernel(h_ref, y0_ref, y1_ref, gate_ref, g_ref, o_ref):
    o_ref[...] = _rms(_combine(h_ref, y0_ref, y1_ref, gate_ref), g_ref[...])


def _final(h, moe_out, g):
    T = h.shape[0]
    y, gate = moe_out
    return pl.pallas_call(
        _final_kernel, grid=(T // TM,),
        in_specs=_combine_specs(T) + [pl.BlockSpec((1, D_MODEL), lambda i: (0, 0))],
        out_specs=pl.BlockSpec((TM, D_MODEL), lambda i: (i, 0)),
        out_shape=jax.ShapeDtypeStruct((T, D_MODEL), f32),
        compiler_params=_cparams(1), name="final_norm")(h, y, y, gate, g)


def _t5_bucket(n):
    nf = jnp.maximum(n, 1).astype(f32)
    large = REL_MAX_EXACT + (jnp.log(nf / REL_MAX_EXACT) / math.log(REL_MAX_DISTANCE / REL_MAX_EXACT)
                             * (REL_BUCKETS - REL_MAX_EXACT)).astype(i32)
    return jnp.where(n < REL_MAX_EXACT, n, jnp.minimum(large, REL_BUCKETS - 1))


def _split_hi_lo(w):
    hi = w.astype(bf16)
    return hi, (w - hi.astype(f32)).astype(bf16)


def kernel(x, positions, attn_norm, w_in, pool_w, pool_scale, mla_q_norm, mla_w_uq, mla_kv_norm, mla_w_ukv,
           diff_lambda, diff_subln, rel_bias, w_out, ffn_norm, router_group_w, router_group_b,
           router_expert_w, router_expert_b, expert_w_gate, expert_w_up, expert_w_down, final_norm):
    B, S, D = x.shape
    T = B * S
    depth = w_in.shape[0]
    assert D == D_MODEL and S % TQ == 0 and T % TM == 0 and (TOP_K * T) % MOE_BLK == 0

    inv = 1.0 / (ROPE_THETA ** (jnp.arange(0, MLA_ROPE, 2, dtype=f32) / MLA_ROPE))
    ang = positions.astype(f32).reshape(T, 1) * inv
    zeros = jnp.zeros((T, LANE - MLA_ROPE), f32)
    cos = jnp.concatenate([jnp.cos(ang), jnp.cos(ang), zeros], axis=1)
    sin = jnp.concatenate([jnp.sin(ang), jnp.sin(ang), zeros], axis=1)
    cost, sint = jnp.cos(ang).T, jnp.sin(ang).T
    assert REL_CLIP == LANE - 1
    tab = jnp.pad(rel_bias[_t5_bucket(jnp.arange(LANE, dtype=i32))].T.astype(f32) * LOG2E,
                  ((0, 8 - DIFF_HEADS), (0, 0)))

    c_kr, c_dq = COL_KR + MLA_ROPE, COL_KR + MLA_ROPE + COL_DK - COL_DQ
    w_in_p = jnp.concatenate([w_in[:, :, :c_kr], jnp.zeros((depth, D, LANE - MLA_ROPE), f32),
                              w_in[:, :, c_kr:c_dq] * (DIFF_QK ** -0.5 * LOG2E), w_in[:, :, c_dq:]],
                             axis=2).astype(bf16)
    wq = mla_w_uq.reshape(depth, MLA_Q_RANK, MLA_HEADS, MLA_NOPE + MLA_ROPE)
    wq = jnp.pad(wq, ((0, 0), (0, 0), (0, 0), (0, QK_PAD - MLA_NOPE - MLA_ROPE)))
    wqt = wq.reshape(depth, MLA_Q_RANK, -1).transpose(0, 2, 1).astype(bf16)
    wkv = mla_w_ukv.reshape(depth, MLA_KV_RANK, MLA_HEADS, MLA_NOPE + MLA_V)
    wk = wkv[..., :MLA_NOPE].reshape(depth, MLA_KV_RANK, -1).astype(bf16)
    wvt = wkv[..., MLA_NOPE:].reshape(depth, MLA_KV_RANK, -1).transpose(0, 2, 1).astype(bf16)
    wo = w_out.astype(bf16)
    pool_wb = pool_w.astype(bf16)
    pad_g = jnp.zeros((depth, D, 8 - N_GROUPS), f32)
    pad_e = jnp.zeros((depth, D, ROUTER_ROWS - 8 - N_EXPERTS), f32)
    wr_hi, wr_lo = _split_hi_lo(jnp.concatenate([router_group_w, pad_g, router_expert_w, pad_e], axis=2))
    wr_cat = jnp.concatenate([wr_hi, wr_lo], axis=2)
    br = jnp.concatenate([router_group_b, pad_g[:, 0], router_expert_b, pad_e[:, 0]], axis=1)[:, :, None]

    h = x.reshape(T, D)
    moe_out = None
    for l in range(depth):
        lam_init = 0.8 - 0.6 * math.exp(-0.3 * l)
        proj, h = _inproj(h, moe_out, attn_norm[l].reshape(1, D), w_in_p, l)
        a_out = _pool(proj, pool_wb, pool_scale[l].reshape(1, POOL_WIDTH), l, B, S)
        qt, k, vt = _mla_prep(proj, cos, sin, cost, sint, mla_q_norm[l].reshape(1, -1), mla_kv_norm[l].reshape(1, -1),
                              wqt, wk, wvt, l, B, S)
        b_out = _mla_attn(qt, k, vt, B, S)
        c_out = _diff_attn(proj, diff_lambda[l], positions, tab, diff_subln[l].reshape(-1, 1), lam_init, B, S)
        h, xn, eid, gate = _outproj_router(a_out, b_out, c_out, wo, h, ffn_norm[l].reshape(1, D), wr_cat, br, l)
        y = _moe(xn, _dispatch(eid, T), expert_w_gate, expert_w_up, expert_w_down, l)
        moe_out = (y, gate.T)
    out = _final(h, moe_out, final_norm.reshape(1, D))
    return out.reshape(B, S, D)
```
